```python
import jax, jax.numpy as jnp
from jax import lax
import numpy as np

D_MODEL = 2048
BATCH = 4
SEQ = 2048
DEPTH = 1
DEC_BATCH = 128
DEC_SEQ = 4
PAST_LEN = 16384
PAGE_SIZE = 128

CHUNK = 128
D_A = D_MODEL // 2
G_A = 8
D_B = D_MODEL // 2
CONV_B = 31
N_MEM = 256
N_XHEADS = 4
XHEAD_DIM = D_MODEL // 8
D_C = N_XHEADS * XHEAD_DIM
D_FF = ((8 * D_MODEL // 3 + 127) // 128) * 128
CONV_F = 3
N_BRANCH = 3
D_IN = 2 * D_A + 2 * D_B + D_C + N_BRANCH * D_MODEL
EPS = 1e-6

kernel_name = 'gated_gmlp_conformer_xattn_decoder_step'


def rms_norm(x, g):
    xf = x.astype(jnp.float32)
    y = xf * lax.rsqrt(jnp.mean(xf * xf, axis=-1, keepdims=True) + EPS)
    return (y * g.astype(jnp.float32)).astype(x.dtype)


def layer_norm(x, g, b):
    xf = x.astype(jnp.float32)
    mu = jnp.mean(xf, axis=-1, keepdims=True)
    var = jnp.mean(jnp.square(xf - mu), axis=-1, keepdims=True)
    y = (xf - mu) * lax.rsqrt(var + EPS)
    return (y * g.astype(jnp.float32) + b.astype(jnp.float32)).astype(x.dtype)


def causal_dwconv(x, hist, w, b):
    xc = jnp.concatenate([hist, x], axis=1)
    y = lax.conv_general_dilated(xc, w[:, None, :].astype(xc.dtype), (1,), 'VALID',
                                 dimension_numbers=('NWC', 'WIO', 'NWC'),
                                 feature_group_count=x.shape[-1])
    return y + b, xc[:, -(w.shape[0] - 1):]


def chunk_spatial_gate(v, w_s, b_s):
    B, L, _ = v.shape
    c = min(L, CHUNK)
    n = L // c
    vr = v.reshape(B, n, c, G_A, D_A // G_A)
    mask = jnp.tril(jnp.ones((c, c), dtype=bool))
    wm = jnp.where(mask, w_s[:, :c, :c], 0)
    s = jnp.einsum('gts,bnsgd->bntgd', wm, vr) + b_s[:, :c].T[None, None, :, :, None]
    return s.reshape(B, L, D_A)


def cross_attend(q, k, v):
    B, L = q.shape[0], q.shape[1]
    s = jnp.einsum('blhe,bmhe->bhlm', q, k).astype(jnp.float32) * (XHEAD_DIM ** -0.5)
    p = jax.nn.softmax(s, axis=-1).astype(v.dtype)
    o = jnp.einsum('bhlm,bmhe->blhe', p, v)
    return o.reshape(B, L, D_C)


def memory_kv(mem, g_mem, w_k, w_v):
    B = mem.shape[0]
    m = rms_norm(mem, g_mem)
    k = (m @ w_k).reshape(B, N_MEM, N_XHEADS, XHEAD_DIM)
    v = (m @ w_v).reshape(B, N_MEM, N_XHEADS, XHEAD_DIM)
    return k, v


def decoder_layer(x, mem_k, mem_v, conv_hist, ffn_hist, g_mix, w_in, ln_v_g, ln_v_b, w_s, b_s, w_pa,
                  conv_w, conv_b, ln_b_g, ln_b_b, w_pb, w_pc, w_o, g_ffn, w_up, ffn_conv_w,
                  ffn_conv_b, w_down):
    B, L, _ = x.shape
    h = rms_norm(x, g_mix)
    proj = h @ w_in
    zA, zB, q, gates = jnp.split(proj, [2 * D_A, 2 * D_A + 2 * D_B, 2 * D_A + 2 * D_B + D_C], axis=-1)
    u, v = jnp.split(jax.nn.gelu(zA), 2, axis=-1)
    v = layer_norm(v, ln_v_g, ln_v_b)
    a_out = (u * chunk_spatial_gate(v, w_s, b_s)) @ w_pa
    glu_a, glu_b = jnp.split(zB, 2, axis=-1)
    cv, new_conv = causal_dwconv(glu_a * jax.nn.sigmoid(glu_b), conv_hist, conv_w, conv_b)
    b_out = jax.nn.silu(layer_norm(cv, ln_b_g, ln_b_b)) @ w_pb
    c_out = cross_attend(q.reshape(B, L, N_XHEADS, XHEAD_DIM), mem_k, mem_v) @ w_pc
    g_a, g_b, g_c = jnp.split(jax.nn.sigmoid(gates), N_BRANCH, axis=-1)
    x = x + (g_a * a_out + g_b * b_out + g_c * c_out) @ w_o
    f_a, f_g = jnp.split(rms_norm(x, g_ffn) @ w_up, 2, axis=-1)
    f_c, new_ffn = causal_dwconv(f_a, ffn_hist, ffn_conv_w, ffn_conv_b)
    x = x + (jax.nn.gelu(f_c) * f_g) @ w_down
    return x, new_conv, new_ffn, v


def setup_inputs(seed: int = 0) -> dict:
    key = jax.random.key(seed)
    ks = jax.random.split(key, 40)
    f32 = jnp.float32

    def nrm(k, shape, scale=1.0):
        return jax.random.normal(k, shape, f32) * scale

    return {
        'x_prompt': nrm(ks[0], (BATCH, SEQ, D_MODEL)),
        'x_sample': nrm(ks[1], (DEC_BATCH, DEC_SEQ, D_MODEL)),
        'mem_prompt': nrm(ks[2], (BATCH, N_MEM, D_MODEL)),
        'cache_mem_k': nrm(ks[3], (DEPTH, DEC_BATCH, N_MEM, N_XHEADS, XHEAD_DIM)),
        'cache_mem_v': nrm(ks[4], (DEPTH, DEC_BATCH, N_MEM, N_XHEADS, XHEAD_DIM)),
        'state_conv': nrm(ks[5], (DEPTH, DEC_BATCH, CONV_B - 1, D_B), 0.5),
        'state_ffn_conv': nrm(ks[6], (DEPTH, DEC_BATCH, CONV_F - 1, D_FF), 0.5),
        'g_mix': 1.0 + nrm(ks[7], (DEPTH, D_MODEL), 0.02),
        'w_in': nrm(ks[8], (DEPTH, D_MODEL, D_IN), D_MODEL ** -0.5),
        'ln_v_g': 1.0 + nrm(ks[9], (DEPTH, D_A), 0.02),
        'ln_v_b': nrm(ks[10], (DEPTH, D_A), 0.02),
        'w_s': nrm(ks[11], (DEPTH, G_A, CHUNK, CHUNK), CHUNK ** -0.5),
        'b_s': 1.0 + nrm(ks[12], (DEPTH, G_A, CHUNK), 0.02),
        'w_pa': nrm(ks[13], (DEPTH, D_A, D_MODEL), D_A ** -0.5),
        'conv_w': nrm(ks[14], (DEPTH, CONV_B, D_B), CONV_B ** -0.5),
        'conv_b': nrm(ks[15], (DEPTH, D_B), 0.02),
        'ln_b_g': 1.0 + nrm(ks[16], (DEPTH, D_B), 0.02),
        'ln_b_b': nrm(ks[17], (DEPTH, D_B), 0.02),
        'w_pb': nrm(ks[18], (DEPTH, D_B, D_MODEL), D_B ** -0.5),
        'g_mem': 1.0 + nrm(ks[19], (DEPTH, D_MODEL), 0.02),
        'w_k': nrm(ks[20], (DEPTH, D_MODEL, D_C), D_MODEL ** -0.5),
        'w_v': nrm(ks[21], (DEPTH, D_MODEL, D_C), D_MODEL ** -0.5),
        'w_pc': nrm(ks[22], (DEPTH, D_C, D_MODEL), D_C ** -0.5),
        'w_o': nrm(ks[23], (DEPTH, D_MODEL, D_MODEL), D_MODEL ** -0.5),
        'g_ffn': 1.0 + nrm(ks[24], (DEPTH, D_MODEL), 0.02),
        'w_up': nrm(ks[25], (DEPTH, D_MODEL, 2 * D_FF), D_MODEL ** -0.5),
        'ffn_conv_w': nrm(ks[26], (DEPTH, CONV_F, D_FF), CONV_F ** -0.5),
        'ffn_conv_b': nrm(ks[27], (DEPTH, D_FF), 0.02),
        'w_down': nrm(ks[28], (DEPTH, D_FF, D_MODEL), D_FF ** -0.5),
        'g_final': 1.0 + nrm(ks[29], (D_MODEL,), 0.02),
    }


def reference(x_prompt, x_sample, mem_prompt, cache_mem_k, cache_mem_v, state_conv, state_ffn_conv,
              g_mix, w_in, ln_v_g, ln_v_b, w_s, b_s, w_pa, conv_w, conv_b, ln_b_g, ln_b_b, w_pb,
              g_mem, w_k, w_v, w_pc, w_o, g_ffn, w_up, ffn_conv_w, ffn_conv_b, w_down, g_final):
    hp, hs = x_prompt, x_sample
    Bp = x_prompt.shape[0]
    mk_p, mv_p, cv_p, ff_p, cv_s, ff_s, v_s = [], [], [], [], [], [], []
    for l in range(DEPTH):
        lw = (g_mix[l], w_in[l], ln_v_g[l], ln_v_b[l], w_s[l], b_s[l], w_pa[l], conv_w[l], conv_b[l],
              ln_b_g[l], ln_b_b[l], w_pb[l], w_pc[l], w_o[l], g_ffn[l], w_up[l], ffn_conv_w[l],
              ffn_conv_b[l], w_down[l])
        k_p, v_p = memory_kv(mem_prompt, g_mem[l], w_k[l], w_v[l])
        zc = jnp.zeros((Bp, CONV_B - 1, D_B), hp.dtype)
        zf = jnp.zeros((Bp, CONV_F - 1, D_FF), hp.dtype)
        hp, c_p, f_p, _ = decoder_layer(hp, k_p, v_p, zc, zf, *lw)
        hs, c_s, f_s, vr_s = decoder_layer(hs, cache_mem_k[l], cache_mem_v[l], state_conv[l],
                                           state_ffn_conv[l], *lw)
        mk_p.append(k_p); mv_p.append(v_p); cv_p.append(c_p); ff_p.append(f_p)
        cv_s.append(c_s); ff_s.append(f_s); v_s.append(vr_s)
    y_prompt = rms_norm(hp, g_final)
    y_sample = rms_norm(hs, g_final)
    return (y_prompt, y_sample, jnp.stack(mk_p), jnp.stack(mv_p), jnp.stack(cv_p), jnp.stack(ff_p),
            jnp.stack(cv_s), jnp.stack(ff_s), jnp.stack(v_s))
```

```python
import functools

import jax
import jax.numpy as jnp
from jax import lax
from jax.experimental import pallas as pl
from jax.experimental.pallas import tpu as pltpu

F32 = jnp.float32
BF16 = jnp.bfloat16

D_MODEL = 2048
CHUNK = 128
D_A = D_MODEL // 2
G_A = 8
D_B = D_MODEL // 2
CONV_B = 31
N_MEM = 256
N_XHEADS = 4
XHEAD_DIM = D_MODEL // 8
D_C = N_XHEADS * XHEAD_DIM
D_FF = ((8 * D_MODEL // 3 + 127) // 128) * 128
CONV_F = 3
EPS = 1e-6

OFF_ZA = 0
OFF_ZB = 2 * D_A
OFF_Q = OFF_ZB + 2 * D_B
OFF_GATES = OFF_Q + D_C

LANE = 128
SUBLANE = 8
TM = 512
TF = 512
D_FF_PAD = ((D_FF + TF - 1) // TF) * TF
HALO = 32
VMEM_LIMIT = 56 * 1024 * 1024


def _params(*sem):
    return pltpu.CompilerParams(dimension_semantics=sem, vmem_limit_bytes=VMEM_LIMIT)


def _rms(x, g):
    return x * lax.rsqrt(jnp.mean(x * x, axis=-1, keepdims=True) + EPS) * g


def _layer_norm(x, g, b):
    mu = jnp.mean(x, axis=-1, keepdims=True)
    d = x - mu
    var = jnp.mean(d * d, axis=-1, keepdims=True)
    return d * lax.rsqrt(var + EPS) * g + b


def _dot(a, b):
    return jnp.dot(a, b, preferred_element_type=F32)


def _norm_kernel(x_ref, g_ref, o_ref):
    o_ref[...] = _rms(x_ref[...], g_ref[...]).astype(o_ref.dtype)


def _norm_call(x, g, out_dtype, name):
    m, d = x.shape
    tm = min(TM, m)
    return pl.pallas_call(
        _norm_kernel,
        grid=(m // tm,),
        in_specs=[pl.BlockSpec((tm, d), lambda i: (i, 0)),
                  pl.BlockSpec((1, d), lambda i: (0, 0))],
        out_specs=pl.BlockSpec((tm, d), lambda i: (i, 0)),
        out_shape=jax.ShapeDtypeStruct((m, d), out_dtype),
        compiler_params=_params("parallel"),
        name=name,
    )(x, g.reshape(1, d))


def _gmlp_kernel(h_ref, w_ref, lng_ref, lnb_ref, ws_ref, bs_ref, a_ref, vn_ref, *, sample):
    tm = h_ref.shape[0]
    zg = jax.nn.gelu(_dot(h_ref[...], w_ref[...]))
    u = zg[:, :D_A]
    vn = _layer_norm(zg[:, D_A:], lng_ref[...], lnb_ref[...])
    vn_ref[...] = vn
    vb = vn.astype(BF16)
    r = lax.broadcasted_iota(jnp.int32, (CHUNK, CHUNK), 0)
    c = lax.broadcasted_iota(jnp.int32, (CHUNK, CHUNK), 1)
    mask = r >= c
    if sample:
        mask = mask & ((r >> 2) == (c >> 2))
    for g in range(G_A):
        cs = slice(g * LANE, (g + 1) * LANE)
        wm = jnp.where(mask, ws_ref[g], 0.0).astype(BF16)
        for ch in range(tm // CHUNK):
            rs = slice(ch * CHUNK, (ch + 1) * CHUNK)
            s = _dot(wm, vb[rs, cs]) + bs_ref[:, cs]
            a_ref[rs, cs] = (u[rs, cs] * s).astype(BF16)


def _gmlp_call(h, w_in, ln_g, ln_b, ws, bs, sample):
    m = h.shape[0]
    tm = min(TM, m)
    return pl.pallas_call(
        functools.partial(_gmlp_kernel, sample=sample),
        grid=(m // tm,),
        in_specs=[pl.BlockSpec((tm, D_MODEL), lambda i: (i, 0)),
                  pl.BlockSpec((D_MODEL, 2 * D_A), lambda i: (0, OFF_ZA // (2 * D_A))),
                  pl.BlockSpec((1, D_A), lambda i: (0, 0)),
                  pl.BlockSpec((1, D_A), lambda i: (0, 0)),
                  pl.BlockSpec((G_A, CHUNK, CHUNK), lambda i: (0, 0, 0)),
                  pl.BlockSpec((CHUNK, D_A), lambda i: (0, 0))],
        out_specs=[pl.BlockSpec((tm, D_A), lambda i: (i, 0)),
                   pl.BlockSpec((tm, D_A), lambda i: (i, 0))],
        out_shape=[jax.ShapeDtypeStruct((m, D_A), BF16),
                   jax.ShapeDtypeStruct((m, D_A), F32)],
        compiler_params=_params("parallel"),
        name="gmlp_sample" if sample else "gmlp_prompt",
    )(h, w_in, ln_g.reshape(1, D_A), ln_b.reshape(1, D_A), ws, bs)


def _glu_kernel(h_ref, w_ref, o_ref):
    z = _dot(h_ref[...], w_ref[...])
    o_ref[...] = z[:, :D_B] * jax.nn.sigmoid(z[:, D_B:])


def _glu_call(h, w_in, name):
    m = h.shape[0]
    tm = min(TM, m)
    return pl.pallas_call(
        _glu_kernel,
        grid=(m // tm,),
        in_specs=[pl.BlockSpec((tm, D_MODEL), lambda i: (i, 0)),
                  pl.BlockSpec((D_MODEL, 2 * D_B), lambda i: (0, OFF_ZB // (2 * D_B)))],
        out_specs=pl.BlockSpec((tm, D_B), lambda i: (i, 0)),
        out_shape=jax.ShapeDtypeStruct((m, D_B), F32),
        compiler_params=_params("parallel"),
        name=name,
    )(h, w_in)


CONV_ROWS = 128


def _convb_prompt_kernel(x_ref, halo_ref, w_ref, cb_ref, lng_ref, lnb_ref, o_ref, xc_ref, cv_ref):
    tm = x_ref.shape[0]
    first = pl.program_id(1) == 0
    xc_ref[0:HALO, :] = jnp.where(first, 0.0, halo_ref[...])
    xc_ref[HALO:, :] = x_ref[...]
    lead = HALO - (CONV_B - 1)

    def col_body(cb, carry):
        cs = pl.ds(pl.multiple_of(cb * LANE, LANE), LANE)
        for rb in range(tm // CONV_ROWS):
            r0 = rb * CONV_ROWS
            acc = jnp.broadcast_to(cb_ref[:, cs], (CONV_ROWS, LANE))
            for s in range(SUBLANE):
                qs = [q for q in range(HALO // SUBLANE + 1) if lead <= SUBLANE * q + s < lead + CONV_B]
                win = xc_ref[pl.ds(r0 + s, CONV_ROWS + SUBLANE * qs[-1]), cs]
                for q in qs:
                    k = SUBLANE * q + s - lead
                    acc = acc + w_ref[k:k + 1, cs] * win[SUBLANE * q:SUBLANE * q + CONV_ROWS]
            cv_ref[r0:r0 + CONV_ROWS, cs] = acc
        return carry

    lax.fori_loop(0, D_B // LANE, col_body, 0)
    y = _layer_norm(cv_ref[...], lng_ref[...], lnb_ref[...])
    o_ref[...] = (y * jax.nn.sigmoid(y)).astype(BF16)


def _convb_prompt_call(glu, nb, conv_w, conv_b, ln_g, ln_b):
    m = glu.shape[0]
    tm = TM
    nt = m // nb // tm
    hb = tm // HALO
    return pl.pallas_call(
        _convb_prompt_kernel,
        grid=(nb, nt),
        in_specs=[pl.BlockSpec((tm, D_B), lambda b, t: (b * nt + t, 0)),
                  pl.BlockSpec((HALO, D_B), lambda b, t: (jnp.maximum((b * nt + t) * hb - 1, 0), 0)),
                  pl.BlockSpec((CONV_B, D_B), lambda b, t: (0, 0)),
                  pl.BlockSpec((1, D_B), lambda b, t: (0, 0)),
                  pl.BlockSpec((1, D_B), lambda b, t: (0, 0)),
                  pl.BlockSpec((1, D_B), lambda b, t: (0, 0))],
        out_specs=pl.BlockSpec((tm, D_B), lambda b, t: (b * nt + t, 0)),
        out_shape=jax.ShapeDtypeStruct((m, D_B), BF16),
        scratch_shapes=[pltpu.VMEM((tm + HALO, D_B), F32), pltpu.VMEM((tm, D_B), F32)],
        compiler_params=_params("parallel", "arbitrary"),
        name="convb_prompt",
    )(glu, glu, conv_w, conv_b.reshape(1, D_B), ln_g.reshape(1, D_B), ln_b.reshape(1, D_B))


def _convb_sample_kernel(hist_ref, x_ref, w_ref, cb_ref, lng_ref, lnb_ref, o_ref):
    bt, nl, _ = x_ref.shape
    nh = hist_ref.shape[1]
    for t in range(nl):
        acc = jnp.broadcast_to(cb_ref[...], (bt, D_B))
        for k in range(CONV_B):
            j = t + k
            row = hist_ref[:, j, :] if j < nh else x_ref[:, j - nh, :]
            acc = acc + w_ref[k:k + 1, :] * row
        y = _layer_norm(acc, lng_ref[...], lnb_ref[...])
        o_ref[:, t, :] = (y * jax.nn.sigmoid(y)).astype(o_ref.dtype)


def _convb_sample_call(hist, x, conv_w, conv_b, ln_g, ln_b):
    nb, nl, _ = x.shape
    nh = hist.shape[1]
    bt = 16
    return pl.pallas_call(
        _convb_sample_kernel,
        grid=(nb // bt,),
        in_specs=[pl.BlockSpec((bt, nh, D_B), lambda i: (i, 0, 0)),
                  pl.BlockSpec((bt, nl, D_B), lambda i: (i, 0, 0)),
                  pl.BlockSpec((CONV_B, D_B), lambda i: (0, 0)),
                  pl.BlockSpec((1, D_B), lambda i: (0, 0)),
                  pl.BlockSpec((1, D_B), lambda i: (0, 0)),
                  pl.BlockSpec((1, D_B), lambda i: (0, 0))],
        out_specs=pl.BlockSpec((bt, nl, D_B), lambda i: (i, 0, 0)),
        out_shape=jax.ShapeDtypeStruct((nb, nl, D_B), F32),
        compiler_params=_params("parallel"),
        name="convb_sample",
    )(hist, x, conv_w, conv_b.reshape(1, D_B), ln_g.reshape(1, D_B), ln_b.reshape(1, D_B))


def _q_kernel(h_ref, w_ref, o_ref):
    o_ref[...] = _dot(h_ref[...], w_ref[...]).astype(o_ref.dtype)


def _q_call(h, w_in, out_dtype, name):
    m = h.shape[0]
    tm = min(TM, m)
    return pl.pallas_call(
        _q_kernel,
        grid=(m // tm,),
        in_specs=[pl.BlockSpec((tm, D_MODEL), lambda i: (i, 0)),
                  pl.BlockSpec((D_MODEL, D_C), lambda i: (0, OFF_Q // D_C))],
        out_specs=pl.BlockSpec((tm, D_C), lambda i: (i, 0)),
        out_shape=jax.ShapeDtypeStruct((m, D_C), out_dtype),
        compiler_params=_params("parallel"),
        name=name,
    )(h, w_in)


def _memkv_kernel(m_ref, g_ref, wk_ref, wv_ref, k_ref, v_ref):
    mn = _rms(m_ref[...], g_ref[...]).astype(BF16)
    k_ref[...] = _dot(mn, wk_ref[...])
    v_ref[...] = _dot(mn, wv_ref[...])


def _memkv_call(mem, g_mem, w_k, w_v):
    m = mem.shape[0]
    tm = min(TM, m)
    return pl.pallas_call(
        _memkv_kernel,
        grid=(m // tm,),
        in_specs=[pl.BlockSpec((tm, D_MODEL), lambda i: (i, 0)),
                  pl.BlockSpec((1, D_MODEL), lambda i: (0, 0)),
                  pl.BlockSpec((D_MODEL, D_C), lambda i: (0, 0)),
                  pl.BlockSpec((D_MODEL, D_C), lambda i: (0, 0))],
        out_specs=[pl.BlockSpec((tm, D_C), lambda i: (i, 0)),
                   pl.BlockSpec((tm, D_C), lambda i: (i, 0))],
        out_shape=[jax.ShapeDtypeStruct((m, D_C), F32),
                   jax.ShapeDtypeStruct((m, D_C), F32)],
        compiler_params=_params("parallel"),
        name="memory_kv",
    )(mem, g_mem.reshape(1, D_MODEL), w_k, w_v)


def _softmax(s):
    e = jnp.exp(s - jnp.max(s, axis=-1, keepdims=True))
    return e / jnp.sum(e, axis=-1, keepdims=True)


def _attn_prompt_kernel(q_ref, k_ref, v_ref, o_ref):
    for h in range(N_XHEADS):
        cs = slice(h * XHEAD_DIM, (h + 1) * XHEAD_DIM)
        kh = k_ref[0, :, cs].astype(BF16)
        vh = v_ref[0, :, cs].astype(BF16)
        s = lax.dot_general(q_ref[:, cs], kh, (((1,), (1,)), ((), ())), preferred_element_type=F32)
        p = _softmax(s * (XHEAD_DIM ** -0.5)).astype(BF16)
        o_ref[:, cs] = _dot(p, vh).astype(BF16)


def _attn_prompt_call(q, k, v, nb):
    m = q.shape[0]
    tm = TM
    nt = m // nb // tm
    return pl.pallas_call(
        _attn_prompt_kernel,
        grid=(nb, nt),
        in_specs=[pl.BlockSpec((tm, D_C), lambda b, t: (b * nt + t, 0)),
                  pl.BlockSpec((1, N_MEM, D_C), lambda b, t: (b, 0, 0)),
                  pl.BlockSpec((1, N_MEM, D_C), lambda b, t: (b, 0, 0))],
        out_specs=pl.BlockSpec((tm, D_C), lambda b, t: (b * nt + t, 0)),
        out_shape=jax.ShapeDtypeStruct((m, D_C), BF16),
        compiler_params=_params("parallel", "parallel"),
        name="attn_prompt",
    )(q, k, v)


def _attn_sample_kernel(q_ref, k_ref, v_ref, o_ref):
    for h in range(N_XHEADS):
        cs = slice(h * XHEAD_DIM, (h + 1) * XHEAD_DIM)
        qh = q_ref[:, :, cs].astype(BF16)
        kh = k_ref[:, :, cs].astype(BF16)
        vh = v_ref[:, :, cs].astype(BF16)
        s = jnp.einsum("bqe,bme->bqm", qh, kh, preferred_element_type=F32)
        p = _softmax(s * (XHEAD_DIM ** -0.5)).astype(BF16)
        o_ref[:, :, cs] = jnp.einsum("bqm,bme->bqe", p, vh, preferred_element_type=F32)


def _attn_sample_call(q, k, v):
    nb, nl, _ = q.shape
    bt = 4
    return pl.pallas_call(
        _attn_sample_kernel,
        grid=(nb // bt,),
        in_specs=[pl.BlockSpec((bt, nl, D_C), lambda i: (i, 0, 0)),
                  pl.BlockSpec((bt, N_MEM, D_C), lambda i: (i, 0, 0)),
                  pl.BlockSpec((bt, N_MEM, D_C), lambda i: (i, 0, 0))],
        out_specs=pl.BlockSpec((bt, nl, D_C), lambda i: (i, 0, 0)),
        out_shape=jax.ShapeDtypeStruct((nb, nl, D_C), F32),
        compiler_params=_params("parallel"),
        name="attn_sample",
    )(q, k, v)


def _mix_kernel(h_ref, a_ref, b_ref, c_ref, wpa_ref, wpb_ref, wpc_ref, wga_ref, wgb_ref, wgc_ref, o_ref):
    h = h_ref[...]
    mix = jax.nn.sigmoid(_dot(h, wga_ref[...])) * _dot(a_ref[...], wpa_ref[...])
    mix = mix + jax.nn.sigmoid(_dot(h, wgb_ref[...])) * _dot(b_ref[...], wpb_ref[...])
    mix = mix + jax.nn.sigmoid(_dot(h, wgc_ref[...])) * _dot(c_ref[...], wpc_ref[...])
    o_ref[...] = mix.astype(BF16)


def _mix_call(h, a, b, c, w_pa, w_pb, w_pc, w_in, name):
    m = h.shape[0]
    tm = min(TM, m)
    tn = 512
    gate_blk = OFF_GATES // tn
    per_gate = D_MODEL // tn
    row = lambda i, j: (i, 0)
    col = lambda i, j: (0, j)
    return pl.pallas_call(
        _mix_kernel,
        grid=(m // tm, D_MODEL // tn),
        in_specs=[pl.BlockSpec((tm, D_MODEL), row),
                  pl.BlockSpec((tm, D_A), row),
                  pl.BlockSpec((tm, D_B), row),
                  pl.BlockSpec((tm, D_C), row),
                  pl.BlockSpec((D_A, tn), col),
                  pl.BlockSpec((D_B, tn), col),
                  pl.BlockSpec((D_C, tn), col),
                  pl.BlockSpec((D_MODEL, tn), lambda i, j: (0, gate_blk + j)),
                  pl.BlockSpec((D_MODEL, tn), lambda i, j: (0, gate_blk + per_gate + j)),
                  pl.BlockSpec((D_MODEL, tn), lambda i, j: (0, gate_blk + 2 * per_gate + j))],
        out_specs=pl.BlockSpec((tm, tn), lambda i, j: (i, j)),
        out_shape=jax.ShapeDtypeStruct((m, D_MODEL), BF16),
        compiler_params=_params("parallel", "parallel"),
        name=name,
    )(h, a, b, c, w_pa, w_pb, w_pc, w_in, w_in, w_in)


def _oproj_kernel(x_ref, mix_ref, wo_ref, g_ref, x1_ref, h2_ref):
    x1 = x_ref[...] + _dot(mix_ref[...], wo_ref[...])
    x1_ref[...] = x1
    h2_ref[...] = _rms(x1, g_ref[...]).astype(BF16)


def _oproj_call(x, mix, w_o, g_ffn, name):
    m = x.shape[0]
    tm = min(TM, m)
    return pl.pallas_call(
        _oproj_kernel,
        grid=(m // tm,),
        in_specs=[pl.BlockSpec((tm, D_MODEL), lambda i: (i, 0)),
                  pl.BlockSpec((tm, D_MODEL), lambda i: (i, 0)),
                  pl.BlockSpec((D_MODEL, D_MODEL), lambda i: (0, 0)),
                  pl.BlockSpec((1, D_MODEL), lambda i: (0, 0))],
        out_specs=[pl.BlockSpec((tm, D_MODEL), lambda i: (i, 0)),
                   pl.BlockSpec((tm, D_MODEL), lambda i: (i, 0))],
        out_shape=[jax.ShapeDtypeStruct((m, D_MODEL), F32),
                   jax.ShapeDtypeStruct((m, D_MODEL), BF16)],
        compiler_params=_params("parallel"),
        name=name,
    )(x, mix, w_o, g_ffn.reshape(1, D_MODEL))


def _ffn_act(fa, fg, prev1, prev2, cw_ref, cb_ref):
    fc = cw_ref[2:3, :] * fa + cw_ref[1:2, :] * prev1 + cw_ref[0:1, :] * prev2 + cb_ref[...]
    return (jax.nn.gelu(fc) * fg).astype(BF16)


def _ffn_up_prompt_kernel(h_ref, wa_ref, wg_ref, cw_ref, cb_ref, p_ref, tail_ref, fx_ref, carry_ref, *, nt):
    tm = h_ref.shape[0]
    i = pl.program_id(0)
    j = pl.program_id(1)
    h = h_ref[...]
    fa = _dot(h, wa_ref[...])
    fg = _dot(h, wg_ref[...])
    last = fa[tm - SUBLANE:, :]
    fx_ref[SUBLANE:, :] = fa

    @pl.when(i % nt == 0)
    def _():
        fx_ref[0:SUBLANE, :] = jnp.zeros((SUBLANE, fa.shape[1]), F32)

    @pl.when(i % nt != 0)
    def _():
        fx_ref[0:SUBLANE, :] = carry_ref[j]

    carry_ref[j] = last
    tail_ref[0] = last
    prev1 = fx_ref[pl.ds(SUBLANE - 1, tm), :]
    prev2 = fx_ref[pl.ds(SUBLANE - 2, tm), :]
    p_ref[...] = _ffn_act(fa, fg, prev1, prev2, cw_ref, cb_ref)


def _ffn_up_sample_kernel(h_ref, wa_ref, wg_ref, cw_ref, cb_ref, h1_ref, h2_ref, p_ref, fa_ref, fx_ref, *, nl):
    tm = h_ref.shape[0]
    h = h_ref[...]
    fa = _dot(h, wa_ref[...])
    fg = _dot(h, wg_ref[...])
    fa_ref[...] = fa
    fx_ref[SUBLANE:, :] = fa
    fx_ref[0:SUBLANE, :] = jnp.zeros((SUBLANE, fa.shape[1]), F32)
    t = lax.broadcasted_iota(jnp.int32, fa.shape, 0) % nl
    prev1 = jnp.where(t >= 1, fx_ref[pl.ds(SUBLANE - 1, tm), :], h1_ref[...])
    prev2 = jnp.where(t >= 2, fx_ref[pl.ds(SUBLANE - 2, tm), :], h2_ref[...])
    p_ref[...] = _ffn_act(fa, fg, prev1, prev2, cw_ref, cb_ref)


def _ffn_up_prompt_call(h2, wa, wg, cw, cb, nb):
    m = h2.shape[0]
    tm = TM
    nt = m // nb // tm
    nj = D_FF_PAD // TF
    col = lambda i, j: (0, j)
    return pl.pallas_call(
        functools.partial(_ffn_up_prompt_kernel, nt=nt),
        grid=(m // tm, nj),
        in_specs=[pl.BlockSpec((tm, D_MODEL), lambda i, j: (i, 0)),
                  pl.BlockSpec((D_MODEL, TF), col),
                  pl.BlockSpec((D_MODEL, TF), col),
                  pl.BlockSpec((CONV_F, TF), col),
                  pl.BlockSpec((1, TF), col)],
        out_specs=[pl.BlockSpec((tm, TF), lambda i, j: (i, j)),
                   pl.BlockSpec((1, SUBLANE, TF), lambda i, j: (i, 0, j))],
        out_shape=[jax.ShapeDtypeStruct((m, D_FF_PAD), BF16),
                   jax.ShapeDtypeStruct((m // tm, SUBLANE, D_FF_PAD), F32)],
        scratch_shapes=[pltpu.VMEM((tm + SUBLANE, TF), F32), pltpu.VMEM((nj, SUBLANE, TF), F32)],
        compiler_params=_params("arbitrary", "arbitrary"),
        name="ffn_up_prompt",
    )(h2, wa, wg, cw, cb)


def _ffn_up_sample_call(h2, wa, wg, cw, cb, hist1, hist2, nl):
    m = h2.shape[0]
    tm = min(TM, m)
    nj = D_FF_PAD // TF
    col = lambda i, j: (0, j)
    tile = lambda i, j: (i, j)
    return pl.pallas_call(
        functools.partial(_ffn_up_sample_kernel, nl=nl),
        grid=(m // tm, nj),
        in_specs=[pl.BlockSpec((tm, D_MODEL), lambda i, j: (i, 0)),
                  pl.BlockSpec((D_MODEL, TF), col),
                  pl.BlockSpec((D_MODEL, TF), col),
                  pl.BlockSpec((CONV_F, TF), col),
                  pl.BlockSpec((1, TF), col),
                  pl.BlockSpec((tm, TF), tile),
                  pl.BlockSpec((tm, TF), tile)],
        out_specs=[pl.BlockSpec((tm, TF), tile),
                   pl.BlockSpec((tm, TF), tile)],
        out_shape=[jax.ShapeDtypeStruct((m, D_FF_PAD), BF16),
                   jax.ShapeDtypeStruct((m, D_FF_PAD), F32)],
        scratch_shapes=[pltpu.VMEM((tm + SUBLANE, TF), F32)],
        compiler_params=_params("parallel", "parallel"),
        name="ffn_up_sample",
    )(h2, wa, wg, cw, cb, hist1, hist2)


def _ffn_down_kernel(p_ref, wd_ref, x1_ref, g_ref, y_ref, acc_ref):
    k = pl.program_id(1)

    @pl.when(k == 0)
    def _():
        acc_ref[...] = jnp.zeros_like(acc_ref)

    acc_ref[...] += _dot(p_ref[...], wd_ref[...])

    @pl.when(k == pl.num_programs(1) - 1)
    def _():
        y_ref[...] = _rms(x1_ref[...] + acc_ref[...], g_ref[...])


def _ffn_down_call(p, wd, x1, g_final, name):
    m = p.shape[0]
    tm = min(TM, m)
    return pl.pallas_call(
        _ffn_down_kernel,
        grid=(m // tm, D_FF_PAD // TF),
        in_specs=[pl.BlockSpec((tm, TF), lambda i, k: (i, k)),
                  pl.BlockSpec((TF, D_MODEL), lambda i, k: (k, 0)),
                  pl.BlockSpec((tm, D_MODEL), lambda i, k: (i, 0)),
                  pl.BlockSpec((1, D_MODEL), lambda i, k: (0, 0))],
        out_specs=pl.BlockSpec((tm, D_MODEL), lambda i, k: (i, 0)),
        out_shape=jax.ShapeDtypeStruct((m, D_MODEL), F32),
        scratch_shapes=[pltpu.VMEM((tm, D_MODEL), F32)],
        compiler_params=_params("parallel", "arbitrary"),
        name=name,
    )(p, wd, x1, g_final.reshape(1, D_MODEL))


def kernel(x_prompt, x_sample, mem_prompt, cache_mem_k, cache_mem_v, state_conv, state_ffn_conv, g_mix, w_in, ln_v_g, ln_v_b, w_s, b_s, w_pa, conv_w, conv_b, ln_b_g, ln_b_b, w_pb, g_mem, w_k, w_v, w_pc, w_o, g_ffn, w_up, ffn_conv_w, ffn_conv_b, w_down, g_final):
    depth = g_mix.shape[0]
    assert depth == 1
    l = 0
    nbp, lp, _ = x_prompt.shape
    nbs, ls, _ = x_sample.shape
    pad_ff = D_FF_PAD - D_FF

    w_in_b = w_in[l].astype(BF16)
    w_pa_b = w_pa[l].astype(BF16)
    w_pb_b = w_pb[l].astype(BF16)
    w_pc_b = w_pc[l].astype(BF16)
    w_k_b = w_k[l].astype(BF16)
    w_v_b = w_v[l].astype(BF16)
    w_o_b = w_o[l].astype(BF16)
    w_up_a = jnp.pad(w_up[l][:, :D_FF].astype(BF16), ((0, 0), (0, pad_ff)))
    w_up_g = jnp.pad(w_up[l][:, D_FF:].astype(BF16), ((0, 0), (0, pad_ff)))
    w_down_b = jnp.pad(w_down[l].astype(BF16), ((0, pad_ff), (0, 0)))
    cw = jnp.pad(ffn_conv_w[l], ((0, 0), (0, pad_ff)))
    cb = jnp.pad(ffn_conv_b[l], (0, pad_ff)).reshape(1, D_FF_PAD)

    reps = CHUNK // ls
    ws_p = w_s[l]
    ws_s = jnp.tile(w_s[l][:, :ls, :ls], (1, reps, reps))
    bs_p = jnp.repeat(b_s[l].T, D_A // G_A, axis=1)
    bs_s = jnp.repeat(jnp.tile(b_s[l][:, :ls], (1, reps)).T, D_A // G_A, axis=1)

    def layer(x, sample):
        m = x.shape[0]
        tag = "sample" if sample else "prompt"
        h = _norm_call(x, g_mix[l], BF16, "norm_in_" + tag)
        a, vn = _gmlp_call(h, w_in_b, ln_v_g[l], ln_v_b[l], ws_s if sample else ws_p,
                           bs_s if sample else bs_p, sample)
        glu = _glu_call(h, w_in_b, "glu_" + tag)
        if sample:
            bact = _convb_sample_call(state_conv[l], glu.reshape(nbs, ls, D_B), conv_w[l], conv_b[l],
                                      ln_b_g[l], ln_b_b[l]).astype(BF16).reshape(m, D_B)
        else:
            bact = _convb_prompt_call(glu, nbp, conv_w[l], conv_b[l], ln_b_g[l], ln_b_b[l])
        if sample:
            q = _q_call(h, w_in_b, F32, "q_sample").reshape(nbs, ls, D_C)
            kmem = vmem = None
            cact = _attn_sample_call(q, cache_mem_k[l].reshape(nbs, N_MEM, D_C),
                                     cache_mem_v[l].reshape(nbs, N_MEM, D_C)).astype(BF16).reshape(m, D_C)
        else:
            q = _q_call(h, w_in_b, BF16, "q_prompt")
            kmem, vmem = _memkv_call(mem_prompt.reshape(nbp * N_MEM, D_MODEL), g_mem[l], w_k_b, w_v_b)
            cact = _attn_prompt_call(q, kmem.reshape(nbp, N_MEM, D_C), vmem.reshape(nbp, N_MEM, D_C), nbp)
        mix = _mix_call(h, a, bact, cact, w_pa_b, w_pb_b, w_pc_b, w_in_b, "mix_" + tag)
        x1, h2 = _oproj_call(x, mix, w_o_b, g_ffn[l], "oproj_" + tag)
        if sample:
            hist = jnp.pad(state_ffn_conv[l], ((0, 0), (0, 0), (0, pad_ff)))
            zeros = jnp.zeros((nbs, ls - 1, D_FF_PAD), F32)
            hist1 = jnp.concatenate([hist[:, 1:2], zeros], axis=1).reshape(m, D_FF_PAD)
            hist2 = jnp.concatenate([hist, zeros[:, 1:]], axis=1).reshape(m, D_FF_PAD)
            p, fa = _ffn_up_sample_call(h2, w_up_a, w_up_g, cw, cb, hist1, hist2, ls)
            ffn_state = fa.reshape(nbs, ls, D_FF_PAD)[:, ls - (CONV_F - 1):, :D_FF]
        else:
            p, tail = _ffn_up_prompt_call(h2, w_up_a, w_up_g, cw, cb, nbp)
            nt = lp // TM
            ffn_state = tail.reshape(nbp, nt, SUBLANE, D_FF_PAD)[:, nt - 1, SUBLANE - (CONV_F - 1):, :D_FF]
        y = _ffn_down_call(p, w_down_b, x1, g_final, "ffn_down_" + tag)
        return y, kmem, vmem, glu, ffn_state, vn

    yp, kp, vp, glu_p, ffn_p, _ = layer(x_prompt.reshape(nbp * lp, D_MODEL), False)
    ys, _, _, glu_s, ffn_s, vn_s = layer(x_sample.reshape(nbs * ls, D_MODEL), True)

    keep = CONV_B - 1
    conv_p = glu_p.reshape(nbp, lp, D_B)[:, lp - keep:]
    conv_s = jnp.concatenate([state_conv[l], glu_s.reshape(nbs, ls, D_B)], axis=1)[:, ls:]
    return (yp.reshape(nbp, lp, D_MODEL),
            ys.reshape(nbs, ls, D_MODEL),
            kp.reshape(1, nbp, N_MEM, N_XHEADS, XHEAD_DIM),
            vp.reshape(1, nbp, N_MEM, N_XHEADS, XHEAD_DIM),
            conv_p[None],
            ffn_p[None],
            conv_s[None],
            ffn_s[None],
            vn_s.reshape(1, nbs, ls, D_A))
```

```python
import functools

import jax
import jax.numpy as jnp
from jax import lax
from jax.experimental import pallas as pl
from jax.experimental.pallas import tpu as pltpu

F32 = jnp.float32
BF16 = jnp.bfloat16

D_MODEL = 2048
CHUNK = 128
D_A = D_MODEL // 2
G_A = 8
D_B = D_MODEL // 2
CONV_B = 31
N_MEM = 256
N_XHEADS = 4
XHEAD_DIM = D_MODEL // 8
D_C = N_XHEADS * XHEAD_DIM
D_FF = ((8 * D_MODEL // 3 + 127) // 128) * 128
CONV_F = 3
EPS = 1e-6

OFF_ZA = 0
OFF_ZB = 2 * D_A
OFF_Q = OFF_ZB + 2 * D_B
OFF_GATES = OFF_Q + D_C

LANE = 128
SUBLANE = 8
TM = 512
TF = 512
D_FF_PAD = ((D_FF + TF - 1) // TF) * TF
HALO = 32
VMEM_LIMIT = 56 * 1024 * 1024


def _params(*sem):
    return pltpu.CompilerParams(dimension_semantics=sem, vmem_limit_bytes=VMEM_LIMIT)


def _rms(x, g):
    return x * lax.rsqrt(jnp.mean(x * x, axis=-1, keepdims=True) + EPS) * g


def _layer_norm(x, g, b):
    mu = jnp.mean(x, axis=-1, keepdims=True)
    d = x - mu
    var = jnp.mean(d * d, axis=-1, keepdims=True)
    return d * lax.rsqrt(var + EPS) * g + b


def _dot(a, b):
    return jnp.dot(a, b, preferred_element_type=F32)


def _norm_kernel(x_ref, g_ref, o_ref):
    o_ref[...] = _rms(x_ref[...], g_ref[...]).astype(o_ref.dtype)


def _norm_call(x, g, out_dtype, name):
    m, d = x.shape
    tm = min(TM, m)
    return pl.pallas_call(
        _norm_kernel,
        grid=(m // tm,),
        in_specs=[pl.BlockSpec((tm, d), lambda i: (i, 0)),
                  pl.BlockSpec((1, d), lambda i: (0, 0))],
        out_specs=pl.BlockSpec((tm, d), lambda i: (i, 0)),
        out_shape=jax.ShapeDtypeStruct((m, d), out_dtype),
        compiler_params=_params("parallel"),
        name=name,
    )(x, g.reshape(1, d))


def _gmlp_kernel(h_ref, w_ref, lng_ref, lnb_ref, ws_ref, bs_ref, a_ref, vn_ref, *, sample):
    tm = h_ref.shape[0]
    zg = jax.nn.gelu(_dot(h_ref[...], w_ref[...]))
    u = zg[:, :D_A]
    vn = _layer_norm(zg[:, D_A:], lng_ref[...], lnb_ref[...])
    vn_ref[...] = vn
    vb = vn.astype(BF16)
    r = lax.broadcasted_iota(jnp.int32, (CHUNK, CHUNK), 0)
    c = lax.broadcasted_iota(jnp.int32, (CHUNK, CHUNK), 1)
    mask = r >= c
    if sample:
        mask = mask & ((r >> 2) == (c >> 2))
    for g in range(G_A):
        cs = slice(g * LANE, (g + 1) * LANE)
        wm = jnp.where(mask, ws_ref[g], 0.0).astype(BF16)
        for ch in range(tm // CHUNK):
            rs = slice(ch * CHUNK, (ch + 1) * CHUNK)
            s = _dot(wm, vb[rs, cs]) + bs_ref[:, cs]
            a_ref[rs, cs] = (u[rs, cs] * s).astype(BF16)


def _gmlp_call(h, w_in, ln_g, ln_b, ws, bs, sample):
    m = h.shape[0]
    tm = min(TM, m)
    return pl.pallas_call(
        functools.partial(_gmlp_kernel, sample=sample),
        grid=(m // tm,),
        in_specs=[pl.BlockSpec((tm, D_MODEL), lambda i: (i, 0)),
                  pl.BlockSpec((D_MODEL, 2 * D_A), lambda i: (0, OFF_ZA // (2 * D_A))),
                  pl.BlockSpec((1, D_A), lambda i: (0, 0)),
                  pl.BlockSpec((1, D_A), lambda i: (0, 0)),
                  pl.BlockSpec((G_A, CHUNK, CHUNK), lambda i: (0, 0, 0)),
                  pl.BlockSpec((CHUNK, D_A), lambda i: (0, 0))],
        out_specs=[pl.BlockSpec((tm, D_A), lambda i: (i, 0)),
                   pl.BlockSpec((tm, D_A), lambda i: (i, 0))],
        out_shape=[jax.ShapeDtypeStruct((m, D_A), BF16),
                   jax.ShapeDtypeStruct((m, D_A), F32)],
        compiler_params=_params("parallel"),
        name="gmlp_sample" if sample else "gmlp_prompt",
    )(h, w_in, ln_g.reshape(1, D_A), ln_b.reshape(1, D_A), ws, bs)


def _glu_kernel(h_ref, w_ref, o_ref):
    z = _dot(h_ref[...], w_ref[...])
    o_ref[...] = z[:, :D_B] * jax.nn.sigmoid(z[:, D_B:])


def _glu_call(h, w_in, name):
    m = h.shape[0]
    tm = min(TM, m)
    return pl.pallas_call(
        _glu_kernel,
        grid=(m // tm,),
        in_specs=[pl.BlockSpec((tm, D_MODEL), lambda i: (i, 0)),
                  pl.BlockSpec((D_MODEL, 2 * D_B), lambda i: (0, OFF_ZB // (2 * D_B)))],
        out_specs=pl.BlockSpec((tm, D_B), lambda i: (i, 0)),
        out_shape=jax.ShapeDtypeStruct((m, D_B), F32),
        compiler_params=_params("parallel"),
        name=name,
    )(h, w_in)


CONV_ROWS = 128


def _convb_prompt_kernel(x_ref, halo_ref, w_ref, cb_ref, lng_ref, lnb_ref, o_ref, xc_ref, cv_ref):
    tm = x_ref.shape[0]
    first = pl.program_id(1) == 0
    xc_ref[0:HALO, :] = jnp.where(first, 0.0, halo_ref[...])
    xc_ref[HALO:, :] = x_ref[...]
    lead = HALO - (CONV_B - 1)

    def col_body(cb, carry):
        cs = pl.ds(pl.multiple_of(cb * LANE, LANE), LANE)
        for rb in range(tm // CONV_ROWS):
            r0 = rb * CONV_ROWS
            acc = jnp.broadcast_to(cb_ref[:, cs], (CONV_ROWS, LANE))
            for s in range(SUBLANE):
                qs = [q for q in range(HALO // SUBLANE + 1) if lead <= SUBLANE * q + s < lead + CONV_B]
                win = xc_ref[pl.ds(r0 + s, CONV_ROWS + SUBLANE * qs[-1]), cs]
                for q in qs:
                    k = SUBLANE * q + s - lead
                    acc = acc + w_ref[k:k + 1, cs] * win[SUBLANE * q:SUBLANE * q + CONV_ROWS]
            cv_ref[r0:r0 + CONV_ROWS, cs] = acc
        return carry

    lax.fori_loop(0, D_B // LANE, col_body, 0)
    y = _layer_norm(cv_ref[...], lng_ref[...], lnb_ref[...])
    o_ref[...] = (y * jax.nn.sigmoid(y)).astype(BF16)


def _convb_prompt_call(glu, nb, conv_w, conv_b, ln_g, ln_b):
    m = glu.shape[0]
    tm = TM
    nt = m // nb // tm
    hb = tm // HALO
    return pl.pallas_call(
        _convb_prompt_kernel,
        grid=(nb, nt),
        in_specs=[pl.BlockSpec((tm, D_B), lambda b, t: (b * nt + t, 0)),
                  pl.BlockSpec((HALO, D_B), lambda b, t: (jnp.maximum((b * nt + t) * hb - 1, 0), 0)),
                  pl.BlockSpec((CONV_B, D_B), lambda b, t: (0, 0)),
                  pl.BlockSpec((1, D_B), lambda b, t: (0, 0)),
                  pl.BlockSpec((1, D_B), lambda b, t: (0, 0)),
                  pl.BlockSpec((1, D_B), lambda b, t: (0, 0))],
        out_specs=pl.BlockSpec((tm, D_B), lambda b, t: (b * nt + t, 0)),
        out_shape=jax.ShapeDtypeStruct((m, D_B), BF16),
        scratch_shapes=[pltpu.VMEM((tm + HALO, D_B), F32), pltpu.VMEM((tm, D_B), F32)],
        compiler_params=_params("parallel", "arbitrary"),
        name="convb_prompt",
    )(glu, glu, conv_w, conv_b.reshape(1, D_B), ln_g.reshape(1, D_B), ln_b.reshape(1, D_B))


def _convb_sample_kernel(hist_ref, x_ref, w_ref, cb_ref, lng_ref, lnb_ref, o_ref):
    bt, nl, _ = x_ref.shape
    nh = hist_ref.shape[1]
    for t in range(nl):
        acc = jnp.broadcast_to(cb_ref[...], (bt, D_B))
        for k in range(CONV_B):
            j = t + k
            row = hist_ref[:, j, :] if j < nh else x_ref[:, j - nh, :]
            acc = acc + w_ref[k:k + 1, :] * row
        y = _layer_norm(acc, lng_ref[...], lnb_ref[...])
        o_ref[:, t, :] = (y * jax.nn.sigmoid(y)).astype(o_ref.dtype)


def _convb_sample_call(hist, x, conv_w, conv_b, ln_g, ln_b):
    nb, nl, _ = x.shape
    nh = hist.shape[1]
    bt = 16
    return pl.pallas_call(
        _convb_sample_kernel,
        grid=(nb // bt,),
        in_specs=[pl.BlockSpec((bt, nh, D_B), lambda i: (i, 0, 0)),
                  pl.BlockSpec((bt, nl, D_B), lambda i: (i, 0, 0)),
                  pl.BlockSpec((CONV_B, D_B), lambda i: (0, 0)),
                  pl.BlockSpec((1, D_B), lambda i: (0, 0)),
                  pl.BlockSpec((1, D_B), lambda i: (0, 0)),
                  pl.BlockSpec((1, D_B), lambda i: (0, 0))],
        out_specs=pl.BlockSpec((bt, nl, D_B), lambda i: (i, 0, 0)),
        out_shape=jax.ShapeDtypeStruct((nb, nl, D_B), F32),
        compiler_params=_params("parallel"),
        name="convb_sample",
    )(hist, x, conv_w, conv_b.reshape(1, D_B), ln_g.reshape(1, D_B), ln_b.reshape(1, D_B))


def _q_kernel(h_ref, w_ref, o_ref):
    o_ref[...] = _dot(h_ref[...], w_ref[...]).astype(o_ref.dtype)


def _q_call(h, w_in, out_dtype, name):
    m = h.shape[0]
    tm = min(TM, m)
    return pl.pallas_call(
        _q_kernel,
        grid=(m // tm,),
        in_specs=[pl.BlockSpec((tm, D_MODEL), lambda i: (i, 0)),
                  pl.BlockSpec((D_MODEL, D_C), lambda i: (0, OFF_Q // D_C))],
        out_specs=pl.BlockSpec((tm, D_C), lambda i: (i, 0)),
        out_shape=jax.ShapeDtypeStruct((m, D_C), out_dtype),
        compiler_params=_params("parallel"),
        name=name,
    )(h, w_in)


def _memkv_kernel(m_ref, g_ref, wk_ref, wv_ref, k_ref, v_ref):
    mn = _rms(m_ref[...], g_ref[...]).astype(BF16)
    k_ref[...] = _dot(mn, wk_ref[...])
    v_ref[...] = _dot(mn, wv_ref[...])


def _memkv_call(mem, g_mem, w_k, w_v):
    m = mem.shape[0]
    tm = min(TM, m)
    return pl.pallas_call(
        _memkv_kernel,
        grid=(m // tm,),
        in_specs=[pl.BlockSpec((tm, D_MODEL), lambda i: (i, 0)),
                  pl.BlockSpec((1, D_MODEL), lambda i: (0, 0)),
                  pl.BlockSpec((D_MODEL, D_C), lambda i: (0, 0)),
                  pl.BlockSpec((D_MODEL, D_C), lambda i: (0, 0))],
        out_specs=[pl.BlockSpec((tm, D_C), lambda i: (i, 0)),
                   pl.BlockSpec((tm, D_C), lambda i: (i, 0))],
        out_shape=[jax.ShapeDtypeStruct((m, D_C), F32),
                   jax.ShapeDtypeStruct((m, D_C), F32)],
        compiler_params=_params("parallel"),
        name="memory_kv",
    )(mem, g_mem.reshape(1, D_MODEL), w_k, w_v)


def _softmax(s):
    e = jnp.exp(s - jnp.max(s, axis=-1, keepdims=True))
    return e / jnp.sum(e, axis=-1, keepdims=True)


def _attn_prompt_kernel(q_ref, k_ref, v_ref, o_ref):
    for h in range(N_XHEADS):
        cs = slice(h * XHEAD_DIM, (h + 1) * XHEAD_DIM)
        kh = k_ref[0, :, cs].astype(BF16)
        vh = v_ref[0, :, cs].astype(BF16)
        s = lax.dot_general(q_ref[:, cs], kh, (((1,), (1,)), ((), ())), preferred_element_type=F32)
        p = _softmax(s * (XHEAD_DIM ** -0.5)).astype(BF16)
        o_ref[:, cs] = _dot(p, vh).astype(BF16)


def _attn_prompt_call(q, k, v, nb):
    m = q.shape[0]
    tm = TM
    nt = m // nb // tm
    return pl.pallas_call(
        _attn_prompt_kernel,
        grid=(nb, nt),
        in_specs=[pl.BlockSpec((tm, D_C), lambda b, t: (b * nt + t, 0)),
                  pl.BlockSpec((1, N_MEM, D_C), lambda b, t: (b, 0, 0)),
                  pl.BlockSpec((1, N_MEM, D_C), lambda b, t: (b, 0, 0))],
        out_specs=pl.BlockSpec((tm, D_C), lambda b, t: (b * nt + t, 0)),
        out_shape=jax.ShapeDtypeStruct((m, D_C), BF16),
        compiler_params=_params("parallel", "parallel"),
        name="attn_prompt",
    )(q, k, v)


def _attn_sample_kernel(q_ref, k_ref, v_ref, o_ref):
    for h in range(N_XHEADS):
        cs = slice(h * XHEAD_DIM, (h + 1) * XHEAD_DIM)
        qh = q_ref[:, :, cs].astype(BF16)
        kh = k_ref[:, :, h, :].astype(BF16)
        vh = v_ref[:, :, h, :].astype(BF16)
        s = jnp.einsum("bqe,bme->bqm", qh, kh, preferred_element_type=F32)
        p = _softmax(s * (XHEAD_DIM ** -0.5)).astype(BF16)
        o_ref[:, :, cs] = jnp.einsum("bqm,bme->bqe", p, vh, preferred_element_type=F32)


def _attn_sample_call(q, k, v, l):
    nb, nl, _ = q.shape
    bt = 4
    cache_spec = pl.BlockSpec((None, bt, N_MEM, N_XHEADS, XHEAD_DIM), lambda i: (l, i, 0, 0, 0))
    return pl.pallas_call(
        _attn_sample_kernel,
        grid=(nb // bt,),
        in_specs=[pl.BlockSpec((bt, nl, D_C), lambda i: (i, 0, 0)), cache_spec, cache_spec],
        out_specs=pl.BlockSpec((bt, nl, D_C), lambda i: (i, 0, 0)),
        out_shape=jax.ShapeDtypeStruct((nb, nl, D_C), F32),
        compiler_params=_params("parallel"),
        name="attn_sample",
    )(q, k, v)


def _mix_kernel(h_ref, a_ref, b_ref, c_ref, wpa_ref, wpb_ref, wpc_ref, wga_ref, wgb_ref, wgc_ref, o_ref):
    h = h_ref[...]
    mix = jax.nn.sigmoid(_dot(h, wga_ref[...])) * _dot(a_ref[...], wpa_ref[...])
    mix = mix + jax.nn.sigmoid(_dot(h, wgb_ref[...])) * _dot(b_ref[...], wpb_ref[...])
    mix = mix + jax.nn.sigmoid(_dot(h, wgc_ref[...])) * _dot(c_ref[...], wpc_ref[...])
    o_ref[...] = mix.astype(BF16)


def _mix_call(h, a, b, c, w_pa, w_pb, w_pc, w_in, name):
    m = h.shape[0]
    tm = min(TM, m)
    tn = 512
    gate_blk = OFF_GATES // tn
    per_gate = D_MODEL // tn
    row = lambda i, j: (i, 0)
    col = lambda i, j: (0, j)
    return pl.pallas_call(
        _mix_kernel,
        grid=(m // tm, D_MODEL // tn),
        in_specs=[pl.BlockSpec((tm, D_MODEL), row),
                  pl.BlockSpec((tm, D_A), row),
                  pl.BlockSpec((tm, D_B), row),
                  pl.BlockSpec((tm, D_C), row),
                  pl.BlockSpec((D_A, tn), col),
                  pl.BlockSpec((D_B, tn), col),
                  pl.BlockSpec((D_C, tn), col),
                  pl.BlockSpec((D_MODEL, tn), lambda i, j: (0, gate_blk + j)),
                  pl.BlockSpec((D_MODEL, tn), lambda i, j: (0, gate_blk + per_gate + j)),
                  pl.BlockSpec((D_MODEL, tn), lambda i, j: (0, gate_blk + 2 * per_gate + j))],
        out_specs=pl.BlockSpec((tm, tn), lambda i, j: (i, j)),
        out_shape=jax.ShapeDtypeStruct((m, D_MODEL), BF16),
        compiler_params=_params("parallel", "parallel"),
        name=name,
    )(h, a, b, c, w_pa, w_pb, w_pc, w_in, w_in, w_in)


def _oproj_kernel(x_ref, mix_ref, wo_ref, g_ref, x1_ref, h2_ref):
    x1 = x_ref[...] + _dot(mix_ref[...], wo_ref[...])
    x1_ref[...] = x1
    h2_ref[...] = _rms(x1, g_ref[...]).astype(BF16)


def _oproj_call(x, mix, w_o, g_ffn, name):
    m = x.shape[0]
    tm = min(TM, m)
    return pl.pallas_call(
        _oproj_kernel,
        grid=(m // tm,),
        in_specs=[pl.BlockSpec((tm, D_MODEL), lambda i: (i, 0)),
                  pl.BlockSpec((tm, D_MODEL), lambda i: (i, 0)),
                  pl.BlockSpec((D_MODEL, D_MODEL), lambda i: (0, 0)),
                  pl.BlockSpec((1, D_MODEL), lambda i: (0, 0))],
        out_specs=[pl.BlockSpec((tm, D_MODEL), lambda i: (i, 0)),
                   pl.BlockSpec((tm, D_MODEL), lambda i: (i, 0))],
        out_shape=[jax.ShapeDtypeStruct((m, D_MODEL), F32),
                   jax.ShapeDtypeStruct((m, D_MODEL), BF16)],
        compiler_params=_params("parallel"),
        name=name,
    )(x, mix, w_o, g_ffn.reshape(1, D_MODEL))


FFN_TM = 1024
FFN_SUB = 256


def _ffn_kernel(*refs, sample, nt, nl):
    if sample:
        h_ref, x1_ref, wa_ref, wg_ref, wd_ref, cw_ref, cb_ref, g_ref, h1_ref, h2_ref, y_ref, fa_ref, fx_ref = refs
    else:
        h_ref, x1_ref, wa_ref, wg_ref, wd_ref, cw_ref, cb_ref, g_ref, y_ref, tail_ref, fx_ref, carry_ref = refs
    tm = h_ref.shape[0]
    tf = wa_ref.shape[1]
    i = pl.program_id(0)
    j = pl.program_id(1)

    @pl.when(j == 0)
    def _():
        y_ref[...] = x1_ref[...]

    if sample:
        fx_ref[0:SUBLANE, :] = jnp.zeros((SUBLANE, tf), F32)
    else:
        @pl.when(i % nt == 0)
        def _():
            fx_ref[0:SUBLANE, :] = jnp.zeros((SUBLANE, tf), F32)

        @pl.when(i % nt != 0)
        def _():
            fx_ref[0:SUBLANE, :] = carry_ref[j]

    h = h_ref[...]
    ps = []
    for c in range(tf // FFN_SUB):
        cs = slice(c * FFN_SUB, (c + 1) * FFN_SUB)
        fa = _dot(h, wa_ref[:, cs])
        fg = _dot(h, wg_ref[:, cs])
        fx_ref[SUBLANE:, cs] = fa
        prev1 = fx_ref[pl.ds(SUBLANE - 1, tm), cs]
        prev2 = fx_ref[pl.ds(SUBLANE - 2, tm), cs]
        if sample:
            fa_ref[:, cs] = fa
            t = lax.broadcasted_iota(jnp.int32, fa.shape, 0) % nl
            prev1 = jnp.where(t >= 1, prev1, h1_ref[:, cs])
            prev2 = jnp.where(t >= 2, prev2, h2_ref[:, cs])
        fc = cw_ref[2:3, cs] * fa + cw_ref[1:2, cs] * prev1 + cw_ref[0:1, cs] * prev2 + cb_ref[:, cs]
        ps.append((jax.nn.gelu(fc) * fg).astype(BF16))
    if not sample:
        last = fx_ref[tm:tm + SUBLANE, :]
        carry_ref[j] = last
        tail_ref[0] = last
    y_ref[...] += _dot(jnp.concatenate(ps, axis=1), wd_ref[...])

    @pl.when(j == pl.num_programs(1) - 1)
    def _():
        y_ref[...] = _rms(y_ref[...], g_ref[...])


def _ffn_call(h2, x1, wa, wg, wd, cw, cb, g_final, nb, hists, sample):
    m = h2.shape[0]
    tm = min(FFN_TM, m)
    nl = m // nb
    nt = max(nl // tm, 1)
    nj = D_FF_PAD // TF
    row = lambda i, j: (i, 0)
    col = lambda i, j: (0, j)
    tile = lambda i, j: (i, j)
    in_specs = [pl.BlockSpec((tm, D_MODEL), row),
                pl.BlockSpec((tm, D_MODEL), row, pipeline_mode=pl.Buffered(1)),
                pl.BlockSpec((D_MODEL, TF), col),
                pl.BlockSpec((D_MODEL, TF), col),
                pl.BlockSpec((TF, D_MODEL), lambda i, j: (j, 0)),
                pl.BlockSpec((CONV_F, TF), col),
                pl.BlockSpec((1, TF), col),
                pl.BlockSpec((1, D_MODEL), lambda i, j: (0, 0))]
    args = [h2, x1, wa, wg, wd, cw, cb, g_final.reshape(1, D_MODEL)]
    scratch = [pltpu.VMEM((tm + SUBLANE, TF), F32)]
    if sample:
        in_specs += [pl.BlockSpec((tm, TF), tile), pl.BlockSpec((tm, TF), tile)]
        args += list(hists)
        out_specs = [pl.BlockSpec((tm, D_MODEL), row), pl.BlockSpec((tm, TF), tile)]
        out_shape = [jax.ShapeDtypeStruct((m, D_MODEL), F32), jax.ShapeDtypeStruct((m, D_FF_PAD), F32)]
    else:
        out_specs = [pl.BlockSpec((tm, D_MODEL), row), pl.BlockSpec((1, SUBLANE, TF), lambda i, j: (i, 0, j))]
        out_shape = [jax.ShapeDtypeStruct((m, D_MODEL), F32),
                     jax.ShapeDtypeStruct((m // tm, SUBLANE, D_FF_PAD), F32)]
        scratch.append(pltpu.VMEM((nj, SUBLANE, TF), F32))
    return pl.pallas_call(
        functools.partial(_ffn_kernel, sample=sample, nt=nt, nl=nl),
        grid=(m // tm, nj),
        in_specs=in_specs,
        out_specs=out_specs,
        out_shape=out_shape,
        scratch_shapes=scratch,
        compiler_params=_params("arbitrary", "arbitrary"),
        name="ffn_sample" if sample else "ffn_prompt",
    )(*args)


def kernel(x_prompt, x_sample, mem_prompt, cache_mem_k, cache_mem_v, state_conv, state_ffn_conv, g_mix, w_in, ln_v_g, ln_v_b, w_s, b_s, w_pa, conv_w, conv_b, ln_b_g, ln_b_b, w_pb, g_mem, w_k, w_v, w_pc, w_o, g_ffn, w_up, ffn_conv_w, ffn_conv_b, w_down, g_final):
    depth = g_mix.shape[0]
    assert depth == 1
    l = 0
    nbp, lp, _ = x_prompt.shape
    nbs, ls, _ = x_sample.shape
    pad_ff = D_FF_PAD - D_FF

    w_in_b = w_in[l].astype(BF16)
    w_pa_b = w_pa[l].astype(BF16)
    w_pb_b = w_pb[l].astype(BF16)
    w_pc_b = w_pc[l].astype(BF16)
    w_k_b = w_k[l].astype(BF16)
    w_v_b = w_v[l].astype(BF16)
    w_o_b = w_o[l].astype(BF16)
    w_up_a = jnp.pad(w_up[l][:, :D_FF].astype(BF16), ((0, 0), (0, pad_ff)))
    w_up_g = jnp.pad(w_up[l][:, D_FF:].astype(BF16), ((0, 0), (0, pad_ff)))
    w_down_b = jnp.pad(w_down[l].astype(BF16), ((0, pad_ff), (0, 0)))
    cw = jnp.pad(ffn_conv_w[l], ((0, 0), (0, pad_ff)))
    cb = jnp.pad(ffn_conv_b[l], (0, pad_ff)).reshape(1, D_FF_PAD)

    reps = CHUNK // ls
    ws_p = w_s[l]
    ws_s = jnp.tile(w_s[l][:, :ls, :ls], (1, reps, reps))
    bs_p = jnp.repeat(b_s[l].T, D_A // G_A, axis=1)
    bs_s = jnp.repeat(jnp.tile(b_s[l][:, :ls], (1, reps)).T, D_A // G_A, axis=1)

    def layer(x, sample):
        m = x.shape[0]
        tag = "sample" if sample else "prompt"
        h = _norm_call(x, g_mix[l], BF16, "norm_in_" + tag)
        a, vn = _gmlp_call(h, w_in_b, ln_v_g[l], ln_v_b[l], ws_s if sample else ws_p,
                           bs_s if sample else bs_p, sample)
        glu = _glu_call(h, w_in_b, "glu_" + tag)
        if sample:
            bact = _convb_sample_call(state_conv[l], glu.reshape(nbs, ls, D_B), conv_w[l], conv_b[l],
                                      ln_b_g[l], ln_b_b[l]).astype(BF16).reshape(m, D_B)
        else:
            bact = _convb_prompt_call(glu, nbp, conv_w[l], conv_b[l], ln_b_g[l], ln_b_b[l])
        if sample:
            q = _q_call(h, w_in_b, F32, "q_sample").reshape(nbs, ls, D_C)
            kmem = vmem = None
            cact = _attn_sample_call(q, cache_mem_k, cache_mem_v, l).astype(BF16).reshape(m, D_C)
        else:
            q = _q_call(h, w_in_b, BF16, "q_prompt")
            kmem, vmem = _memkv_call(mem_prompt.reshape(nbp * N_MEM, D_MODEL), g_mem[l], w_k_b, w_v_b)
            cact = _attn_prompt_call(q, kmem.reshape(nbp, N_MEM, D_C), vmem.reshape(nbp, N_MEM, D_C), nbp)
        mix = _mix_call(h, a, bact, cact, w_pa_b, w_pb_b, w_pc_b, w_in_b, "mix_" + tag)
        x1, h2 = _oproj_call(x, mix, w_o_b, g_ffn[l], "oproj_" + tag)
        if sample:
            hist = jnp.pad(state_ffn_conv[l], ((0, 0), (0, 0), (0, pad_ff)))
            zeros = jnp.zeros((nbs, ls - 1, D_FF_PAD), F32)
            hist1 = jnp.concatenate([hist[:, 1:2], zeros], axis=1).reshape(m, D_FF_PAD)
            hist2 = jnp.concatenate([hist, zeros[:, 1:]], axis=1).reshape(m, D_FF_PAD)
            y, fa = _ffn_call(h2, x1, w_up_a, w_up_g, w_down_b, cw, cb, g_final, nbs, (hist1, hist2), True)
            ffn_state = fa.reshape(nbs, ls, D_FF_PAD)[:, ls - (CONV_F - 1):, :D_FF]
        else:
            y, tail = _ffn_call(h2, x1, w_up_a, w_up_g, w_down_b, cw, cb, g_final, nbp, None, False)
            nt = tail.shape[0] // nbp
            ffn_state = tail.reshape(nbp, nt, SUBLANE, D_FF_PAD)[:, nt - 1, SUBLANE - (CONV_F - 1):, :D_FF]
        return y, kmem, vmem, glu, ffn_state, vn

    yp, kp, vp, glu_p, ffn_p, _ = layer(x_prompt.reshape(nbp * lp, D_MODEL), False)
    ys, _, _, glu_s, ffn_s, vn_s = layer(x_sample.reshape(nbs * ls, D_MODEL), True)

    keep = CONV_B - 1
    conv_p = glu_p.reshape(nbp, lp, D_B)[:, lp - keep:]
    conv_s = jnp.concatenate([state_conv[l], glu_s.reshape(nbs, ls, D_B)], axis=1)[:, ls:]
    return (yp.reshape(nbp, lp, D_MODEL),
            ys.reshape(nbs, ls, D_MODEL),
            kp.reshape(1, nbp, N_MEM, N_XHEADS, XHEAD_DIM),
            vp.reshape(1, nbp, N_MEM, N_XHEADS, XHEAD_DIM),
            conv_p[None],
            ffn_p[None],
            conv_s[None],
            ffn_s[None],
            vn_s.reshape(1, nbs, ls, D_A))
```

```python
import functools

import jax
import jax.numpy as jnp
from jax import lax
from jax.experimental import pallas as pl
from jax.experimental.pallas import tpu as pltpu

F32 = jnp.float32
BF16 = jnp.bfloat16

D_MODEL = 2048
CHUNK = 128
D_A = D_MODEL // 2
G_A = 8
D_B = D_MODEL // 2
CONV_B = 31
N_MEM = 256
N_XHEADS = 4
XHEAD_DIM = D_MODEL // 8
D_C = N_XHEADS * XHEAD_DIM
D_FF = ((8 * D_MODEL // 3 + 127) // 128) * 128
CONV_F = 3
EPS = 1e-6

OFF_ZA = 0
OFF_ZB = 2 * D_A
OFF_Q = OFF_ZB + 2 * D_B
OFF_GATES = OFF_Q + D_C

LANE = 128
SUBLANE = 8
TM = 512
TF = 512
D_FF_PAD = ((D_FF + TF - 1) // TF) * TF
HALO = 32
VMEM_LIMIT = 56 * 1024 * 1024


def _params(*sem):
    return pltpu.CompilerParams(dimension_semantics=sem, vmem_limit_bytes=VMEM_LIMIT)


def _rms(x, g):
    return x * lax.rsqrt(jnp.mean(x * x, axis=-1, keepdims=True) + EPS) * g


def _layer_norm(x, g, b):
    mu = jnp.mean(x, axis=-1, keepdims=True)
    d = x - mu
    var = jnp.mean(d * d, axis=-1, keepdims=True)
    return d * lax.rsqrt(var + EPS) * g + b


def _dot(a, b):
    return jnp.dot(a, b, preferred_element_type=F32)


def _softmax(s):
    e = jnp.exp(s - jnp.max(s, axis=-1, keepdims=True))
    return e / jnp.sum(e, axis=-1, keepdims=True)


def _spatial_gate(u, vb, ws_ref, bs_ref, a_ref, sample):
    tm = vb.shape[0]
    r = lax.broadcasted_iota(jnp.int32, (CHUNK, CHUNK), 0)
    c = lax.broadcasted_iota(jnp.int32, (CHUNK, CHUNK), 1)
    mask = r >= c
    if sample:
        mask = mask & ((r >> 2) == (c >> 2))
    for g in range(G_A):
        cs = slice(g * LANE, (g + 1) * LANE)
        wm = jnp.where(mask, ws_ref[g], 0.0).astype(BF16)
        for ch in range(tm // CHUNK):
            rs = slice(ch * CHUNK, (ch + 1) * CHUNK)
            s = _dot(wm, vb[rs, cs]) + bs_ref[:, cs]
            a_ref[rs, cs] = (u[rs, cs] * s).astype(BF16)


IN_SUB = 256
CONV_ROWS = 128


def _conv_taps(xc_ref, sh_ref, w_ref, bias, r0, cs):
    lead = HALO - (CONV_B - 1)
    acc = jnp.broadcast_to(bias, (CONV_ROWS, bias.shape[1]))
    for s in range(SUBLANE):
        qs = [q for q in range(HALO // SUBLANE + 1) if lead <= SUBLANE * q + s < lead + CONV_B]
        n = CONV_ROWS + SUBLANE * qs[-1]
        if s:
            sh_ref[0:n, :] = xc_ref[pl.ds(r0 + s, n), cs]
        for q in qs:
            k = SUBLANE * q + s - lead
            if s:
                win = sh_ref[SUBLANE * q:SUBLANE * q + CONV_ROWS, :]
            else:
                win = xc_ref[r0 + SUBLANE * q:r0 + SUBLANE * q + CONV_ROWS, cs]
            acc = acc + w_ref[k:k + 1, cs] * win
    return acc


def _inproj_prompt_kernel(x_ref, gm_ref, wa_ref, wb_ref, wq_ref, lvg_ref, lvb_ref, ws_ref, bs_ref,
                          cw_ref, cb_ref, lbg_ref, lbb_ref,
                          h_ref, a_ref, b_ref, q_ref, ctail_ref,
                          xc_ref, cv_ref, sh_ref, u_ref, v_ref, *, nt):
    tm = x_ref.shape[0]
    t = pl.program_id(0) % nt

    @pl.when(t == 0)
    def _():
        xc_ref[0:HALO, :] = jnp.zeros((HALO, D_B), F32)

    @pl.when(t != 0)
    def _():
        xc_ref[0:HALO, :] = xc_ref[tm:tm + HALO, :]

    h = _rms(x_ref[...], gm_ref[...]).astype(BF16)
    h_ref[...] = h

    for c in range(D_B // IN_SUB):
        cs = slice(c * IN_SUB, (c + 1) * IN_SUB)
        za = _dot(h, wb_ref[:, cs])
        zb = _dot(h, wb_ref[:, D_B + c * IN_SUB:D_B + (c + 1) * IN_SUB])
        xc_ref[HALO:, cs] = za * jax.nn.sigmoid(zb)
        for rb in range(tm // CONV_ROWS):
            r0 = rb * CONV_ROWS
            cv_ref[r0:r0 + CONV_ROWS, cs] = _conv_taps(xc_ref, sh_ref, cw_ref, cb_ref[:, cs], r0, cs)
    ctail_ref[0] = xc_ref[tm:tm + HALO, :]
    y = _layer_norm(cv_ref[...], lbg_ref[...], lbb_ref[...])
    b_ref[...] = (y * jax.nn.sigmoid(y)).astype(BF16)

    for c in range(D_A // IN_SUB):
        cs = slice(c * IN_SUB, (c + 1) * IN_SUB)
        u_ref[:, cs] = jax.nn.gelu(_dot(h, wa_ref[:, cs]))
        v_ref[:, cs] = jax.nn.gelu(_dot(h, wa_ref[:, D_A + c * IN_SUB:D_A + (c + 1) * IN_SUB]))
    q_ref[...] = _dot(h, wq_ref[...]).astype(BF16)
    vb = _layer_norm(v_ref[...], lvg_ref[...], lvb_ref[...]).astype(BF16)
    _spatial_gate(u_ref, vb, ws_ref, bs_ref, a_ref, False)


def _inproj_prompt_call(x, nb, g_mix, w_in, ln_v_g, ln_v_b, ws, bs, conv_w, conv_b, ln_b_g, ln_b_b):
    m = x.shape[0]
    tm = TM
    nt = m // nb // tm
    const = lambda i: (0, 0)
    row = lambda i: (i, 0)
    single = pl.Buffered(1)
    vec = lambda n: pl.BlockSpec((1, n), const)
    return pl.pallas_call(
        functools.partial(_inproj_prompt_kernel, nt=nt),
        grid=(m // tm,),
        in_specs=[pl.BlockSpec((tm, D_MODEL), row),
                  vec(D_MODEL),
                  pl.BlockSpec((D_MODEL, 2 * D_A), lambda i: (0, OFF_ZA // (2 * D_A)), pipeline_mode=single),
                  pl.BlockSpec((D_MODEL, 2 * D_B), lambda i: (0, OFF_ZB // (2 * D_B)), pipeline_mode=single),
                  pl.BlockSpec((D_MODEL, D_C), lambda i: (0, OFF_Q // D_C), pipeline_mode=single),
                  vec(D_A), vec(D_A),
                  pl.BlockSpec((G_A, CHUNK, CHUNK), lambda i: (0, 0, 0)),
                  pl.BlockSpec((CHUNK, D_A), const),
                  pl.BlockSpec((CONV_B, D_B), const),
                  vec(D_B), vec(D_B), vec(D_B)],
        out_specs=[pl.BlockSpec((tm, D_MODEL), row),
                   pl.BlockSpec((tm, D_A), row),
                   pl.BlockSpec((tm, D_B), row),
                   pl.BlockSpec((tm, D_C), row),
                   pl.BlockSpec((1, HALO, D_B), lambda i: (i, 0, 0))],
        out_shape=[jax.ShapeDtypeStruct((m, D_MODEL), BF16),
                   jax.ShapeDtypeStruct((m, D_A), BF16),
                   jax.ShapeDtypeStruct((m, D_B), BF16),
                   jax.ShapeDtypeStruct((m, D_C), BF16),
                   jax.ShapeDtypeStruct((m // tm, HALO, D_B), F32)],
        scratch_shapes=[pltpu.VMEM((tm + HALO, D_B), F32),
                        pltpu.VMEM((tm, D_B), F32),
                        pltpu.VMEM((CONV_ROWS + HALO, IN_SUB), F32),
                        pltpu.VMEM((tm, D_A), F32),
                        pltpu.VMEM((tm, D_A), F32)],
        compiler_params=_params("arbitrary"),
        name="inproj_prompt",
    )(x, g_mix.reshape(1, D_MODEL), w_in, w_in, w_in, ln_v_g.reshape(1, D_A), ln_v_b.reshape(1, D_A), ws, bs,
      conv_w, conv_b.reshape(1, D_B), ln_b_g.reshape(1, D_B), ln_b_b.reshape(1, D_B))


def _norm_kernel(x_ref, g_ref, o_ref):
    o_ref[...] = _rms(x_ref[...], g_ref[...]).astype(o_ref.dtype)


def _norm_call(x, g, out_dtype, name):
    m, d = x.shape
    tm = min(TM, m)
    return pl.pallas_call(
        _norm_kernel,
        grid=(m // tm,),
        in_specs=[pl.BlockSpec((tm, d), lambda i: (i, 0)),
                  pl.BlockSpec((1, d), lambda i: (0, 0))],
        out_specs=pl.BlockSpec((tm, d), lambda i: (i, 0)),
        out_shape=jax.ShapeDtypeStruct((m, d), out_dtype),
        compiler_params=_params("parallel"),
        name=name,
    )(x, g.reshape(1, d))


def _gmlp_sample_kernel(h_ref, w_ref, lng_ref, lnb_ref, ws_ref, bs_ref, a_ref, vn_ref):
    zg = jax.nn.gelu(_dot(h_ref[...], w_ref[...]))
    vn = _layer_norm(zg[:, D_A:], lng_ref[...], lnb_ref[...])
    vn_ref[...] = vn
    _spatial_gate(zg[:, :D_A], vn.astype(BF16), ws_ref, bs_ref, a_ref, True)


def _gmlp_sample_call(h, w_in, ln_g, ln_b, ws, bs):
    m = h.shape[0]
    tm = min(TM, m)
    return pl.pallas_call(
        _gmlp_sample_kernel,
        grid=(m // tm,),
        in_specs=[pl.BlockSpec((tm, D_MODEL), lambda i: (i, 0)),
                  pl.BlockSpec((D_MODEL, 2 * D_A), lambda i: (0, OFF_ZA // (2 * D_A))),
                  pl.BlockSpec((1, D_A), lambda i: (0, 0)),
                  pl.BlockSpec((1, D_A), lambda i: (0, 0)),
                  pl.BlockSpec((G_A, CHUNK, CHUNK), lambda i: (0, 0, 0)),
                  pl.BlockSpec((CHUNK, D_A), lambda i: (0, 0))],
        out_specs=[pl.BlockSpec((tm, D_A), lambda i: (i, 0)),
                   pl.BlockSpec((tm, D_A), lambda i: (i, 0))],
        out_shape=[jax.ShapeDtypeStruct((m, D_A), BF16),
                   jax.ShapeDtypeStruct((m, D_A), F32)],
        compiler_params=_params("parallel"),
        name="gmlp_sample",
    )(h, w_in, ln_g.reshape(1, D_A), ln_b.reshape(1, D_A), ws, bs)


def _glu_kernel(h_ref, w_ref, o_ref):
    z = _dot(h_ref[...], w_ref[...])
    o_ref[...] = z[:, :D_B] * jax.nn.sigmoid(z[:, D_B:])


def _glu_call(h, w_in, name):
    m = h.shape[0]
    tm = min(TM, m)
    return pl.pallas_call(
        _glu_kernel,
        grid=(m // tm,),
        in_specs=[pl.BlockSpec((tm, D_MODEL), lambda i: (i, 0)),
                  pl.BlockSpec((D_MODEL, 2 * D_B), lambda i: (0, OFF_ZB // (2 * D_B)))],
        out_specs=pl.BlockSpec((tm, D_B), lambda i: (i, 0)),
        out_shape=jax.ShapeDtypeStruct((m, D_B), F32),
        compiler_params=_params("parallel"),
        name=name,
    )(h, w_in)


def _convb_sample_kernel(hist_ref, x_ref, w_ref, cb_ref, lng_ref, lnb_ref, o_ref):
    bt, nl, _ = x_ref.shape
    nh = hist_ref.shape[1]
    for t in range(nl):
        acc = jnp.broadcast_to(cb_ref[...], (bt, D_B))
        for k in range(CONV_B):
            j = t + k
            row = hist_ref[:, j, :] if j < nh else x_ref[:, j - nh, :]
            acc = acc + w_ref[k:k + 1, :] * row
        y = _layer_norm(acc, lng_ref[...], lnb_ref[...])
        o_ref[:, t, :] = (y * jax.nn.sigmoid(y)).astype(o_ref.dtype)


def _convb_sample_call(hist, x, conv_w, conv_b, ln_g, ln_b):
    nb, nl, _ = x.shape
    nh = hist.shape[1]
    bt = 16
    return pl.pallas_call(
        _convb_sample_kernel,
        grid=(nb // bt,),
        in_specs=[pl.BlockSpec((bt, nh, D_B), lambda i: (i, 0, 0)),
                  pl.BlockSpec((bt, nl, D_B), lambda i: (i, 0, 0)),
                  pl.BlockSpec((CONV_B, D_B), lambda i: (0, 0)),
                  pl.BlockSpec((1, D_B), lambda i: (0, 0)),
                  pl.BlockSpec((1, D_B), lambda i: (0, 0)),
                  pl.BlockSpec((1, D_B), lambda i: (0, 0))],
        out_specs=pl.BlockSpec((bt, nl, D_B), lambda i: (i, 0, 0)),
        out_shape=jax.ShapeDtypeStruct((nb, nl, D_B), F32),
        compiler_params=_params("parallel"),
        name="convb_sample",
    )(hist, x, conv_w, conv_b.reshape(1, D_B), ln_g.reshape(1, D_B), ln_b.reshape(1, D_B))


def _q_kernel(h_ref, w_ref, o_ref):
    o_ref[...] = _dot(h_ref[...], w_ref[...]).astype(o_ref.dtype)


def _q_call(h, w_in, out_dtype, name):
    m = h.shape[0]
    tm = min(TM, m)
    return pl.pallas_call(
        _q_kernel,
        grid=(m // tm,),
        in_specs=[pl.BlockSpec((tm, D_MODEL), lambda i: (i, 0)),
                  pl.BlockSpec((D_MODEL, D_C), lambda i: (0, OFF_Q // D_C))],
        out_specs=pl.BlockSpec((tm, D_C), lambda i: (i, 0)),
        out_shape=jax.ShapeDtypeStruct((m, D_C), out_dtype),
        compiler_params=_params("parallel"),
        name=name,
    )(h, w_in)


def _memkv_kernel(m_ref, g_ref, wk_ref, wv_ref, k5_ref, v5_ref, kb_ref, vb_ref):
    mn = _rms(m_ref[...], g_ref[...]).astype(BF16)
    k = _dot(mn, wk_ref[...])
    v = _dot(mn, wv_ref[...])
    k5_ref[...] = k.reshape(N_MEM, N_XHEADS, XHEAD_DIM)
    v5_ref[...] = v.reshape(N_MEM, N_XHEADS, XHEAD_DIM)
    kb_ref[...] = k.astype(BF16)
    vb_ref[...] = v.astype(BF16)


def _memkv_call(mem, g_mem, w_k, w_v):
    nb = mem.shape[0]
    const = lambda i: (0, 0)
    cache_spec = pl.BlockSpec((None, None, N_MEM, N_XHEADS, XHEAD_DIM), lambda i: (0, i, 0, 0, 0))
    seq_spec = pl.BlockSpec((None, N_MEM, D_C), lambda i: (i, 0, 0))
    cache_shape = jax.ShapeDtypeStruct((1, nb, N_MEM, N_XHEADS, XHEAD_DIM), F32)
    return pl.pallas_call(
        _memkv_kernel,
        grid=(nb,),
        in_specs=[pl.BlockSpec((None, N_MEM, D_MODEL), lambda i: (i, 0, 0)),
                  pl.BlockSpec((1, D_MODEL), const),
                  pl.BlockSpec((D_MODEL, D_C), const),
                  pl.BlockSpec((D_MODEL, D_C), const)],
        out_specs=[cache_spec, cache_spec, seq_spec, seq_spec],
        out_shape=[cache_shape, cache_shape,
                   jax.ShapeDtypeStruct((nb, N_MEM, D_C), BF16),
                   jax.ShapeDtypeStruct((nb, N_MEM, D_C), BF16)],
        compiler_params=_params("parallel"),
        name="memory_kv",
    )(mem, g_mem.reshape(1, D_MODEL), w_k, w_v)


def _attn_prompt_kernel(q_ref, k_ref, v_ref, o_ref):
    for h in range(N_XHEADS):
        cs = slice(h * XHEAD_DIM, (h + 1) * XHEAD_DIM)
        s = lax.dot_general(q_ref[:, cs], k_ref[:, cs], (((1,), (1,)), ((), ())), preferred_element_type=F32)
        p = _softmax(s * (XHEAD_DIM ** -0.5)).astype(BF16)
        o_ref[:, cs] = _dot(p, v_ref[:, cs]).astype(BF16)


def _attn_prompt_call(q, k, v):
    m = q.shape[0]
    nb = k.shape[0]
    tm = TM
    nt = m // nb // tm
    return pl.pallas_call(
        _attn_prompt_kernel,
        grid=(nb, nt),
        in_specs=[pl.BlockSpec((tm, D_C), lambda b, t: (b * nt + t, 0)),
                  pl.BlockSpec((None, N_MEM, D_C), lambda b, t: (b, 0, 0)),
                  pl.BlockSpec((None, N_MEM, D_C), lambda b, t: (b, 0, 0))],
        out_specs=pl.BlockSpec((tm, D_C), lambda b, t: (b * nt + t, 0)),
        out_shape=jax.ShapeDtypeStruct((m, D_C), BF16),
        compiler_params=_params("parallel", "parallel"),
        name="attn_prompt",
    )(q, k, v)


def _attn_sample_kernel(q_ref, k_ref, v_ref, o_ref):
    bt, nl, _ = q_ref.shape
    rows = lax.broadcasted_iota(jnp.int32, (N_XHEADS * nl, N_MEM * N_XHEADS), 0)
    cols = lax.broadcasted_iota(jnp.int32, (N_XHEADS * nl, N_MEM * N_XHEADS), 1)
    same_head = (cols % N_XHEADS) == (rows // nl)
    for b in range(bt):
        q = q_ref[b]
        qs = jnp.concatenate([q[:, h * XHEAD_DIM:(h + 1) * XHEAD_DIM] for h in range(N_XHEADS)], axis=0)
        k2 = k_ref[b].reshape(N_MEM * N_XHEADS, XHEAD_DIM).astype(BF16)
        v2 = v_ref[b].reshape(N_MEM * N_XHEADS, XHEAD_DIM).astype(BF16)
        s = lax.dot_general(qs.astype(BF16), k2, (((1,), (1,)), ((), ())), preferred_element_type=F32)
        p = _softmax(jnp.where(same_head, s * (XHEAD_DIM ** -0.5), -jnp.inf)).astype(BF16)
        o = _dot(p, v2)
        for h in range(N_XHEADS):
            o_ref[b, :, h * XHEAD_DIM:(h + 1) * XHEAD_DIM] = o[h * nl:(h + 1) * nl, :]


def _attn_sample_call(q, k, v, l):
    nb, nl, _ = q.shape
    bt = 4
    cache_spec = pl.BlockSpec((None, bt, N_MEM, N_XHEADS, XHEAD_DIM), lambda i: (l, i, 0, 0, 0))
    return pl.pallas_call(
        _attn_sample_kernel,
        grid=(nb // bt,),
        in_specs=[pl.BlockSpec((bt, nl, D_C), lambda i: (i, 0, 0)), cache_spec, cache_spec],
        out_specs=pl.BlockSpec((bt, nl, D_C), lambda i: (i, 0, 0)),
        out_shape=jax.ShapeDtypeStruct((nb, nl, D_C), F32),
        compiler_params=_params("parallel"),
        name="attn_sample",
    )(q, k, v)


def _mix_kernel(h_ref, a_ref, b_ref, c_ref, wpa_ref, wpb_ref, wpc_ref, wga_ref, wgb_ref, wgc_ref, o_ref):
    h = h_ref[...]
    mix = jax.nn.sigmoid(_dot(h, wga_ref[...])) * _dot(a_ref[...], wpa_ref[...])
    mix = mix + jax.nn.sigmoid(_dot(h, wgb_ref[...])) * _dot(b_ref[...], wpb_ref[...])
    mix = mix + jax.nn.sigmoid(_dot(h, wgc_ref[...])) * _dot(c_ref[...], wpc_ref[...])
    o_ref[...] = mix.astype(BF16)


def _mix_call(h, a, b, c, w_pa, w_pb, w_pc, w_in, name):
    m = h.shape[0]
    tm = min(TM, m)
    tn = 512
    gate_blk = OFF_GATES // tn
    per_gate = D_MODEL // tn
    row = lambda i, j: (i, 0)
    col = lambda i, j: (0, j)
    return pl.pallas_call(
        _mix_kernel,
        grid=(m // tm, D_MODEL // tn),
        in_specs=[pl.BlockSpec((tm, D_MODEL), row),
                  pl.BlockSpec((tm, D_A), row),
                  pl.BlockSpec((tm, D_B), row),
                  pl.BlockSpec((tm, D_C), row),
                  pl.BlockSpec((D_A, tn), col),
                  pl.BlockSpec((D_B, tn), col),
                  pl.BlockSpec((D_C, tn), col),
                  pl.BlockSpec((D_MODEL, tn), lambda i, j: (0, gate_blk + j)),
                  pl.BlockSpec((D_MODEL, tn), lambda i, j: (0, gate_blk + per_gate + j)),
                  pl.BlockSpec((D_MODEL, tn), lambda i, j: (0, gate_blk + 2 * per_gate + j))],
        out_specs=pl.BlockSpec((tm, tn), lambda i, j: (i, j)),
        out_shape=jax.ShapeDtypeStruct((m, D_MODEL), BF16),
        compiler_params=_params("parallel", "parallel"),
        name=name,
    )(h, a, b, c, w_pa, w_pb, w_pc, w_in, w_in, w_in)


def _oproj_kernel(x_ref, mix_ref, wo_ref, g_ref, x1_ref, h2_ref):
    x1 = x_ref[...] + _dot(mix_ref[...], wo_ref[...])
    x1_ref[...] = x1
    h2_ref[...] = _rms(x1, g_ref[...]).astype(BF16)


def _oproj_call(x, mix, w_o, g_ffn, name):
    m = x.shape[0]
    tm = min(TM, m)
    return pl.pallas_call(
        _oproj_kernel,
        grid=(m // tm,),
        in_specs=[pl.BlockSpec((tm, D_MODEL), lambda i: (i, 0)),
                  pl.BlockSpec((tm, D_MODEL), lambda i: (i, 0)),
                  pl.BlockSpec((D_MODEL, D_MODEL), lambda i: (0, 0)),
                  pl.BlockSpec((1, D_MODEL), lambda i: (0, 0))],
        out_specs=[pl.BlockSpec((tm, D_MODEL), lambda i: (i, 0)),
                   pl.BlockSpec((tm, D_MODEL), lambda i: (i, 0))],
        out_shape=[jax.ShapeDtypeStruct((m, D_MODEL), F32),
                   jax.ShapeDtypeStruct((m, D_MODEL), BF16)],
        compiler_params=_params("parallel"),
        name=name,
    )(x, mix, w_o, g_ffn.reshape(1, D_MODEL))


FFN_TM = 1024
FFN_SUB = 256


def _ffn_kernel(*refs, sample, nt, nl):
    if sample:
        h_ref, x1_ref, wa_ref, wg_ref, wd_ref, cw_ref, cb_ref, g_ref, h1_ref, h2_ref, y_ref, fa_ref, fx_ref = refs
    else:
        h_ref, x1_ref, wa_ref, wg_ref, wd_ref, cw_ref, cb_ref, g_ref, y_ref, tail_ref, fx_ref, carry_ref = refs
    tm = h_ref.shape[0]
    tf = wa_ref.shape[1]
    i = pl.program_id(0)
    j = pl.program_id(1)

    @pl.when(j == 0)
    def _():
        y_ref[...] = x1_ref[...]

    if sample:
        fx_ref[0:SUBLANE, :] = jnp.zeros((SUBLANE, tf), F32)
    else:
        @pl.when(i % nt == 0)
        def _():
            fx_ref[0:SUBLANE, :] = jnp.zeros((SUBLANE, tf), F32)

        @pl.when(i % nt != 0)
        def _():
            fx_ref[0:SUBLANE, :] = carry_ref[j]

    h = h_ref[...]
    ps = []
    for c in range(tf // FFN_SUB):
        cs = slice(c * FFN_SUB, (c + 1) * FFN_SUB)
        fa = _dot(h, wa_ref[:, cs])
        fg = _dot(h, wg_ref[:, cs])
        fx_ref[SUBLANE:, cs] = fa
        prev1 = fx_ref[pl.ds(SUBLANE - 1, tm), cs]
        prev2 = fx_ref[pl.ds(SUBLANE - 2, tm), cs]
        if sample:
            fa_ref[:, cs] = fa
            t = lax.broadcasted_iota(jnp.int32, fa.shape, 0) % nl
            prev1 = jnp.where(t >= 1, prev1, h1_ref[:, cs])
            prev2 = jnp.where(t >= 2, prev2, h2_ref[:, cs])
        fc = cw_ref[2:3, cs] * fa + cw_ref[1:2, cs] * prev1 + cw_ref[0:1, cs] * prev2 + cb_ref[:, cs]
        ps.append((jax.nn.gelu(fc) * fg).astype(BF16))
    if not sample:
        last = fx_ref[tm:tm + SUBLANE, :]
        carry_ref[j] = last
        tail_ref[0] = last
    y_ref[...] += _dot(jnp.concatenate(ps, axis=1), wd_ref[...])

    @pl.when(j == pl.num_programs(1) - 1)
    def _():
        y_ref[...] = _rms(y_ref[...], g_ref[...])


def _ffn_call(h2, x1, wa, wg, wd, cw, cb, g_final, nb, hists, sample):
    m = h2.shape[0]
    tm = min(FFN_TM, m)
    nl = m // nb
    nt = max(nl // tm, 1)
    nj = D_FF_PAD // TF
    row = lambda i, j: (i, 0)
    col = lambda i, j: (0, j)
    tile = lambda i, j: (i, j)
    in_specs = [pl.BlockSpec((tm, D_MODEL), row),
                pl.BlockSpec((tm, D_MODEL), row, pipeline_mode=pl.Buffered(1)),
                pl.BlockSpec((D_MODEL, TF), col),
                pl.BlockSpec((D_MODEL, TF), col),
                pl.BlockSpec((TF, D_MODEL), lambda i, j: (j, 0)),
                pl.BlockSpec((CONV_F, TF), col),
                pl.BlockSpec((1, TF), col),
                pl.BlockSpec((1, D_MODEL), lambda i, j: (0, 0))]
    args = [h2, x1, wa, wg, wd, cw, cb, g_final.reshape(1, D_MODEL)]
    scratch = [pltpu.VMEM((tm + SUBLANE, TF), F32)]
    if sample:
        in_specs += [pl.BlockSpec((tm, TF), tile), pl.BlockSpec((tm, TF), tile)]
        args += list(hists)
        out_specs = [pl.BlockSpec((tm, D_MODEL), row), pl.BlockSpec((tm, TF), tile)]
        out_shape = [jax.ShapeDtypeStruct((m, D_MODEL), F32), jax.ShapeDtypeStruct((m, D_FF_PAD), F32)]
    else:
        out_specs = [pl.BlockSpec((tm, D_MODEL), row), pl.BlockSpec((1, SUBLANE, TF), lambda i, j: (i, 0, j))]
        out_shape = [jax.ShapeDtypeStruct((m, D_MODEL), F32),
                     jax.ShapeDtypeStruct((m // tm, SUBLANE, D_FF_PAD), F32)]
        scratch.append(pltpu.VMEM((nj, SUBLANE, TF), F32))
    return pl.pallas_call(
        functools.partial(_ffn_kernel, sample=sample, nt=nt, nl=nl),
        grid=(m // tm, nj),
        in_specs=in_specs,
        out_specs=out_specs,
        out_shape=out_shape,
        scratch_shapes=scratch,
        compiler_params=_params("arbitrary", "arbitrary"),
        name="ffn_sample" if sample else "ffn_prompt",
    )(*args)


def kernel(x_prompt, x_sample, mem_prompt, cache_mem_k, cache_mem_v, state_conv, state_ffn_conv, g_mix, w_in, ln_v_g, ln_v_b, w_s, b_s, w_pa, conv_w, conv_b, ln_b_g, ln_b_b, w_pb, g_mem, w_k, w_v, w_pc, w_o, g_ffn, w_up, ffn_conv_w, ffn_conv_b, w_down, g_final):
    depth = g_mix.shape[0]
    assert depth == 1
    l = 0
    nbp, lp, _ = x_prompt.shape
    nbs, ls, _ = x_sample.shape
    mp, ms = nbp * lp, nbs * ls
    pad_ff = D_FF_PAD - D_FF

    w_in_b = w_in[l].astype(BF16)
    w_pa_b = w_pa[l].astype(BF16)
    w_pb_b = w_pb[l].astype(BF16)
    w_pc_b = w_pc[l].astype(BF16)
    w_k_b = w_k[l].astype(BF16)
    w_v_b = w_v[l].astype(BF16)
    w_o_b = w_o[l].astype(BF16)
    w_up_a = jnp.pad(w_up[l][:, :D_FF], ((0, 0), (0, pad_ff))).astype(BF16)
    w_up_g = jnp.pad(w_up[l][:, D_FF:], ((0, 0), (0, pad_ff))).astype(BF16)
    w_down_b = jnp.pad(w_down[l], ((0, pad_ff), (0, 0))).astype(BF16)
    cw = jnp.pad(ffn_conv_w[l], ((0, 0), (0, pad_ff)))
    cb = jnp.pad(ffn_conv_b[l], (0, pad_ff)).reshape(1, D_FF_PAD)

    reps = CHUNK // ls
    ws_p = w_s[l]
    ws_s = jnp.tile(w_s[l][:, :ls, :ls], (1, reps, reps))
    bs_p = jnp.repeat(b_s[l].T, D_A // G_A, axis=1)
    bs_s = jnp.repeat(jnp.tile(b_s[l][:, :ls], (1, reps)).T, D_A // G_A, axis=1)

    xp = x_prompt.reshape(mp, D_MODEL)
    h, a, bact, q, ctail = _inproj_prompt_call(xp, nbp, g_mix[l], w_in_b, ln_v_g[l], ln_v_b[l], ws_p, bs_p,
                                               conv_w[l], conv_b[l], ln_b_g[l], ln_b_b[l])
    k5, v5, kb, vb = _memkv_call(mem_prompt, g_mem[l], w_k_b, w_v_b)
    cact = _attn_prompt_call(q, kb, vb)
    mix = _mix_call(h, a, bact, cact, w_pa_b, w_pb_b, w_pc_b, w_in_b, "mix_prompt")
    x1, h2 = _oproj_call(xp, mix, w_o_b, g_ffn[l], "oproj_prompt")
    yp, tail = _ffn_call(h2, x1, w_up_a, w_up_g, w_down_b, cw, cb, g_final, nbp, None, False)
    nt = tail.shape[0] // nbp
    ffn_p = tail.reshape(nbp, nt, SUBLANE, D_FF_PAD)[:, nt - 1, SUBLANE - (CONV_F - 1):, :D_FF]
    nt = ctail.shape[0] // nbp
    conv_p = ctail.reshape(nbp, nt, HALO, D_B)[:, nt - 1, HALO - (CONV_B - 1):]

    xs = x_sample.reshape(ms, D_MODEL)
    h = _norm_call(xs, g_mix[l], BF16, "norm_in_sample")
    a, vn_s = _gmlp_sample_call(h, w_in_b, ln_v_g[l], ln_v_b[l], ws_s, bs_s)
    glu_s = _glu_call(h, w_in_b, "glu_sample").reshape(nbs, ls, D_B)
    bact = _convb_sample_call(state_conv[l], glu_s, conv_w[l], conv_b[l], ln_b_g[l], ln_b_b[l])
    q = _q_call(h, w_in_b, F32, "q_sample").reshape(nbs, ls, D_C)
    cact = _attn_sample_call(q, cache_mem_k, cache_mem_v, l)
    mix = _mix_call(h, a, bact.astype(BF16).reshape(ms, D_B), cact.astype(BF16).reshape(ms, D_C),
                    w_pa_b, w_pb_b, w_pc_b, w_in_b, "mix_sample")
    x1, h2 = _oproj_call(xs, mix, w_o_b, g_ffn[l], "oproj_sample")
    hist = jnp.pad(state_ffn_conv[l], ((0, 0), (0, 0), (0, pad_ff)))
    zeros = jnp.zeros((nbs, ls - 1, D_FF_PAD), F32)
    hist1 = jnp.concatenate([hist[:, 1:2], zeros], axis=1).reshape(ms, D_FF_PAD)
    hist2 = jnp.concatenate([hist, zeros[:, 1:]], axis=1).reshape(ms, D_FF_PAD)
    ys, fa = _ffn_call(h2, x1, w_up_a, w_up_g, w_down_b, cw, cb, g_final, nbs, (hist1, hist2), True)
    ffn_s = fa.reshape(nbs, ls, D_FF_PAD)[:, ls - (CONV_F - 1):, :D_FF]
    conv_s = jnp.concatenate([state_conv[l], glu_s], axis=1)[:, ls:]

    return (yp.reshape(nbp, lp, D_MODEL),
            ys.reshape(nbs, ls, D_MODEL),
            k5,
            v5,
            conv_p[None],
            ffn_p[None],
            conv_s[None],
            ffn_s[None],
            vn_s.reshape(1, nbs, ls, D_A))
```

```python
import functools

import jax
import jax.numpy as jnp
from jax import lax
from jax.experimental import pallas as pl
from jax.experimental.pallas import tpu as pltpu

F32 = jnp.float32
BF16 = jnp.bfloat16

D_MODEL = 2048
CHUNK = 128
D_A = D_MODEL // 2
G_A = 8
D_B = D_MODEL // 2
CONV_B = 31
N_MEM = 256
N_XHEADS = 4
XHEAD_DIM = D_MODEL // 8
D_C = N_XHEADS * XHEAD_DIM
D_FF = ((8 * D_MODEL // 3 + 127) // 128) * 128
CONV_F = 3
EPS = 1e-6

OFF_ZA = 0
OFF_ZB = 2 * D_A
OFF_Q = OFF_ZB + 2 * D_B
OFF_GATES = OFF_Q + D_C

LANE = 128
SUBLANE = 8
TM = 512
TF = 512
D_FF_PAD = ((D_FF + TF - 1) // TF) * TF
HALO = 32
VMEM_LIMIT = 56 * 1024 * 1024


def _params(*sem):
    return pltpu.CompilerParams(dimension_semantics=sem, vmem_limit_bytes=VMEM_LIMIT)


def _rms(x, g):
    return x * lax.rsqrt(jnp.mean(x * x, axis=-1, keepdims=True) + EPS) * g


def _layer_norm(x, g, b):
    mu = jnp.mean(x, axis=-1, keepdims=True)
    d = x - mu
    var = jnp.mean(d * d, axis=-1, keepdims=True)
    return d * lax.rsqrt(var + EPS) * g + b


def _dot(a, b):
    return jnp.dot(a, b, preferred_element_type=F32)


def _softmax(s):
    e = jnp.exp(s - jnp.max(s, axis=-1, keepdims=True))
    return e / jnp.sum(e, axis=-1, keepdims=True)


def _spatial_gate(u, vb, ws_ref, bs_ref, a_ref, sample):
    tm = vb.shape[0]
    r = lax.broadcasted_iota(jnp.int32, (CHUNK, CHUNK), 0)
    c = lax.broadcasted_iota(jnp.int32, (CHUNK, CHUNK), 1)
    mask = r >= c
    if sample:
        mask = mask & ((r >> 2) == (c >> 2))
    for g in range(G_A):
        cs = slice(g * LANE, (g + 1) * LANE)
        wm = jnp.where(mask, ws_ref[g], 0.0).astype(BF16)
        for ch in range(tm // CHUNK):
            rs = slice(ch * CHUNK, (ch + 1) * CHUNK)
            s = _dot(wm, vb[rs, cs]) + bs_ref[:, cs]
            a_ref[rs, cs] = (u[rs, cs] * s).astype(BF16)


IN_SUB = 256
CONV_ROWS = 128


def _conv_taps(xc_ref, sh_ref, w_ref, bias, r0, cs):
    lead = HALO - (CONV_B - 1)
    acc = jnp.broadcast_to(bias, (CONV_ROWS, bias.shape[1]))
    for s in range(SUBLANE):
        qs = [q for q in range(HALO // SUBLANE + 1) if lead <= SUBLANE * q + s < lead + CONV_B]
        n = CONV_ROWS + SUBLANE * qs[-1]
        if s:
            sh_ref[0:n, :] = xc_ref[pl.ds(r0 + s, n), cs]
        for q in qs:
            k = SUBLANE * q + s - lead
            if s:
                win = sh_ref[SUBLANE * q:SUBLANE * q + CONV_ROWS, :]
            else:
                win = xc_ref[r0 + SUBLANE * q:r0 + SUBLANE * q + CONV_ROWS, cs]
            acc = acc + w_ref[k:k + 1, cs] * win
    return acc


def _inproj_prompt_kernel(x_ref, gm_ref, wa_ref, wb_ref, wq_ref, lvg_ref, lvb_ref, ws_ref, bs_ref,
                          cw_ref, cb_ref, lbg_ref, lbb_ref,
                          h_ref, a_ref, b_ref, q_ref, ctail_ref,
                          xc_ref, cv_ref, sh_ref, u_ref, v_ref, *, nt):
    tm = x_ref.shape[0]
    t = pl.program_id(0) % nt

    @pl.when(t == 0)
    def _():
        xc_ref[0:HALO, :] = jnp.zeros((HALO, D_B), F32)

    @pl.when(t != 0)
    def _():
        xc_ref[0:HALO, :] = xc_ref[tm:tm + HALO, :]

    h = _rms(x_ref[...], gm_ref[...]).astype(BF16)
    h_ref[...] = h

    for c in range(D_B // IN_SUB):
        cs = slice(c * IN_SUB, (c + 1) * IN_SUB)
        za = _dot(h, wb_ref[:, cs])
        zb = _dot(h, wb_ref[:, D_B + c * IN_SUB:D_B + (c + 1) * IN_SUB])
        xc_ref[HALO:, cs] = za * jax.nn.sigmoid(zb)
        for rb in range(tm // CONV_ROWS):
            r0 = rb * CONV_ROWS
            cv_ref[r0:r0 + CONV_ROWS, cs] = _conv_taps(xc_ref, sh_ref, cw_ref, cb_ref[:, cs], r0, cs)
    ctail_ref[0] = xc_ref[tm:tm + HALO, :]
    y = _layer_norm(cv_ref[...], lbg_ref[...], lbb_ref[...])
    b_ref[...] = (y * jax.nn.sigmoid(y)).astype(BF16)

    for c in range(D_A // IN_SUB):
        cs = slice(c * IN_SUB, (c + 1) * IN_SUB)
        u_ref[:, cs] = jax.nn.gelu(_dot(h, wa_ref[:, cs]))
        v_ref[:, cs] = jax.nn.gelu(_dot(h, wa_ref[:, D_A + c * IN_SUB:D_A + (c + 1) * IN_SUB]))
    q_ref[...] = _dot(h, wq_ref[...]).astype(BF16)
    vb = _layer_norm(v_ref[...], lvg_ref[...], lvb_ref[...]).astype(BF16)
    _spatial_gate(u_ref, vb, ws_ref, bs_ref, a_ref, False)


def _inproj_weight_specs(const_map):
    single = pl.Buffered(1)
    return [pl.BlockSpec((D_MODEL, 2 * D_A), lambda *i: (0, OFF_ZA // (2 * D_A)), pipeline_mode=single),
            pl.BlockSpec((D_MODEL, 2 * D_B), lambda *i: (0, OFF_ZB // (2 * D_B)), pipeline_mode=single),
            pl.BlockSpec((D_MODEL, D_C), lambda *i: (0, OFF_Q // D_C), pipeline_mode=single)]


def _inproj_prompt_call(x, nb, g_mix, w_in, ln_v_g, ln_v_b, ws, bs, conv_w, conv_b, ln_b_g, ln_b_b):
    m = x.shape[0]
    tm = TM
    nt = m // nb // tm
    const = lambda i: (0, 0)
    row = lambda i: (i, 0)
    vec = lambda n: pl.BlockSpec((1, n), const)
    return pl.pallas_call(
        functools.partial(_inproj_prompt_kernel, nt=nt),
        grid=(m // tm,),
        in_specs=[pl.BlockSpec((tm, D_MODEL), row), vec(D_MODEL)] + _inproj_weight_specs(const) + [
                  vec(D_A), vec(D_A),
                  pl.BlockSpec((G_A, CHUNK, CHUNK), lambda i: (0, 0, 0)),
                  pl.BlockSpec((CHUNK, D_A), const),
                  pl.BlockSpec((CONV_B, D_B), const),
                  vec(D_B), vec(D_B), vec(D_B)],
        out_specs=[pl.BlockSpec((tm, D_MODEL), row),
                   pl.BlockSpec((tm, D_A), row),
                   pl.BlockSpec((tm, D_B), row),
                   pl.BlockSpec((tm, D_C), row),
                   pl.BlockSpec((1, HALO, D_B), lambda i: (i, 0, 0))],
        out_shape=[jax.ShapeDtypeStruct((m, D_MODEL), BF16),
                   jax.ShapeDtypeStruct((m, D_A), BF16),
                   jax.ShapeDtypeStruct((m, D_B), BF16),
                   jax.ShapeDtypeStruct((m, D_C), BF16),
                   jax.ShapeDtypeStruct((m // tm, HALO, D_B), F32)],
        scratch_shapes=[pltpu.VMEM((tm + HALO, D_B), F32),
                        pltpu.VMEM((tm, D_B), F32),
                        pltpu.VMEM((CONV_ROWS + HALO, IN_SUB), F32),
                        pltpu.VMEM((tm, D_A), F32),
                        pltpu.VMEM((tm, D_A), F32)],
        compiler_params=_params("arbitrary"),
        name="inproj_prompt",
    )(x, g_mix.reshape(1, D_MODEL), w_in, w_in, w_in, ln_v_g.reshape(1, D_A), ln_v_b.reshape(1, D_A), ws, bs,
      conv_w, conv_b.reshape(1, D_B), ln_b_g.reshape(1, D_B), ln_b_b.reshape(1, D_B))


def _inproj_sample_kernel(x_ref, gm_ref, wa_ref, wb_ref, wq_ref, lvg_ref, lvb_ref, ws_ref, bs_ref,
                          h_ref, a_ref, q_ref, glu_ref, vn_ref):
    nb, nl, _ = x_ref.shape
    m = nb * nl
    h = _rms(x_ref[...].reshape(m, D_MODEL), gm_ref[...]).astype(BF16)
    h_ref[...] = h
    zg = jax.nn.gelu(_dot(h, wa_ref[...]))
    vn = _layer_norm(zg[:, D_A:], lvg_ref[...], lvb_ref[...])
    vn_ref[...] = vn.reshape(nb, nl, D_A)
    _spatial_gate(zg[:, :D_A], vn.astype(BF16), ws_ref, bs_ref, a_ref, True)
    z = _dot(h, wb_ref[...])
    glu_ref[...] = (z[:, :D_B] * jax.nn.sigmoid(z[:, D_B:])).reshape(nb, nl, D_B)
    q_ref[...] = _dot(h, wq_ref[...])


def _inproj_sample_call(x, g_mix, w_in, ln_v_g, ln_v_b, ws, bs):
    nb, nl, _ = x.shape
    m = nb * nl
    bt = CHUNK // nl
    const = lambda i: (0, 0)
    vec = lambda n: pl.BlockSpec((1, n), const)
    flat = lambda n: pl.BlockSpec((bt * nl, n), lambda i: (i, 0))
    return pl.pallas_call(
        _inproj_sample_kernel,
        grid=(nb // bt,),
        in_specs=[pl.BlockSpec((bt, nl, D_MODEL), lambda i: (i, 0, 0)), vec(D_MODEL)]
                 + _inproj_weight_specs(const) + [
                  vec(D_A), vec(D_A),
                  pl.BlockSpec((G_A, CHUNK, CHUNK), lambda i: (0, 0, 0)),
                  pl.BlockSpec((CHUNK, D_A), const)],
        out_specs=[flat(D_MODEL), flat(D_A), flat(D_C),
                   pl.BlockSpec((bt, nl, D_B), lambda i: (i, 0, 0)),
                   pl.BlockSpec((None, bt, nl, D_A), lambda i: (0, i, 0, 0))],
        out_shape=[jax.ShapeDtypeStruct((m, D_MODEL), BF16),
                   jax.ShapeDtypeStruct((m, D_A), BF16),
                   jax.ShapeDtypeStruct((m, D_C), F32),
                   jax.ShapeDtypeStruct((nb, nl, D_B), F32),
                   jax.ShapeDtypeStruct((1, nb, nl, D_A), F32)],
        compiler_params=_params("arbitrary"),
        name="inproj_sample",
    )(x, g_mix.reshape(1, D_MODEL), w_in, w_in, w_in, ln_v_g.reshape(1, D_A), ln_v_b.reshape(1, D_A), ws, bs)


def _convb_sample_kernel(hist_ref, x_ref, w_ref, cb_ref, lng_ref, lnb_ref, o_ref, nh_ref):
    bt, nl, _ = x_ref.shape
    nh = hist_ref.shape[1]
    window = lambda j: hist_ref[:, j, :] if j < nh else x_ref[:, j - nh, :]
    for t in range(nl):
        acc = jnp.broadcast_to(cb_ref[...], (bt, D_B))
        for k in range(CONV_B):
            acc = acc + w_ref[k:k + 1, :] * window(t + k)
        y = _layer_norm(acc, lng_ref[...], lnb_ref[...])
        o_ref[:, t, :] = y * jax.nn.sigmoid(y)
    for j in range(nh):
        nh_ref[:, j, :] = window(j + nl)


def _convb_sample_call(hist, l, x, conv_w, conv_b, ln_g, ln_b):
    nb, nl, _ = x.shape
    nh = hist.shape[2]
    bt = 16
    const = lambda i: (0, 0)
    state_spec = pl.BlockSpec((None, bt, nh, D_B), lambda i: (l, i, 0, 0))
    new_spec = pl.BlockSpec((bt, nl, D_B), lambda i: (i, 0, 0))
    return pl.pallas_call(
        _convb_sample_kernel,
        grid=(nb // bt,),
        in_specs=[state_spec, new_spec,
                  pl.BlockSpec((CONV_B, D_B), const),
                  pl.BlockSpec((1, D_B), const),
                  pl.BlockSpec((1, D_B), const),
                  pl.BlockSpec((1, D_B), const)],
        out_specs=[new_spec, pl.BlockSpec((None, bt, nh, D_B), lambda i: (0, i, 0, 0))],
        out_shape=[jax.ShapeDtypeStruct((nb, nl, D_B), F32),
                   jax.ShapeDtypeStruct((1, nb, nh, D_B), F32)],
        compiler_params=_params("parallel"),
        name="convb_sample",
    )(hist, x, conv_w, conv_b.reshape(1, D_B), ln_g.reshape(1, D_B), ln_b.reshape(1, D_B))


def _memkv_kernel(m_ref, g_ref, wk_ref, wv_ref, k5_ref, v5_ref, kb_ref, vb_ref):
    mn = _rms(m_ref[...], g_ref[...]).astype(BF16)
    k = _dot(mn, wk_ref[...])
    v = _dot(mn, wv_ref[...])
    k5_ref[...] = k.reshape(N_MEM, N_XHEADS, XHEAD_DIM)
    v5_ref[...] = v.reshape(N_MEM, N_XHEADS, XHEAD_DIM)
    kb_ref[...] = k.astype(BF16)
    vb_ref[...] = v.astype(BF16)


def _memkv_call(mem, g_mem, w_k, w_v):
    nb = mem.shape[0]
    const = lambda i: (0, 0)
    cache_spec = pl.BlockSpec((None, None, N_MEM, N_XHEADS, XHEAD_DIM), lambda i: (0, i, 0, 0, 0))
    seq_spec = pl.BlockSpec((None, N_MEM, D_C), lambda i: (i, 0, 0))
    cache_shape = jax.ShapeDtypeStruct((1, nb, N_MEM, N_XHEADS, XHEAD_DIM), F32)
    return pl.pallas_call(
        _memkv_kernel,
        grid=(nb,),
        in_specs=[pl.BlockSpec((None, N_MEM, D_MODEL), lambda i: (i, 0, 0)),
                  pl.BlockSpec((1, D_MODEL), const),
                  pl.BlockSpec((D_MODEL, D_C), const),
                  pl.BlockSpec((D_MODEL, D_C), const)],
        out_specs=[cache_spec, cache_spec, seq_spec, seq_spec],
        out_shape=[cache_shape, cache_shape,
                   jax.ShapeDtypeStruct((nb, N_MEM, D_C), BF16),
                   jax.ShapeDtypeStruct((nb, N_MEM, D_C), BF16)],
        compiler_params=_params("parallel"),
        name="memory_kv",
    )(mem, g_mem.reshape(1, D_MODEL), w_k, w_v)


def _attn_prompt_kernel(q_ref, k_ref, v_ref, o_ref):
    for h in range(N_XHEADS):
        cs = slice(h * XHEAD_DIM, (h + 1) * XHEAD_DIM)
        s = lax.dot_general(q_ref[:, cs], k_ref[:, cs], (((1,), (1,)), ((), ())), preferred_element_type=F32)
        p = _softmax(s * (XHEAD_DIM ** -0.5)).astype(BF16)
        o_ref[:, cs] = _dot(p, v_ref[:, cs]).astype(BF16)


def _attn_prompt_call(q, k, v):
    m = q.shape[0]
    nb = k.shape[0]
    tm = TM
    nt = m // nb // tm
    return pl.pallas_call(
        _attn_prompt_kernel,
        grid=(nb, nt),
        in_specs=[pl.BlockSpec((tm, D_C), lambda b, t: (b * nt + t, 0)),
                  pl.BlockSpec((None, N_MEM, D_C), lambda b, t: (b, 0, 0)),
                  pl.BlockSpec((None, N_MEM, D_C), lambda b, t: (b, 0, 0))],
        out_specs=pl.BlockSpec((tm, D_C), lambda b, t: (b * nt + t, 0)),
        out_shape=jax.ShapeDtypeStruct((m, D_C), BF16),
        compiler_params=_params("parallel", "parallel"),
        name="attn_prompt",
    )(q, k, v)


def _attn_sample_kernel(q_ref, k_ref, v_ref, o_ref):
    bt = k_ref.shape[0]
    nl = q_ref.shape[0] // bt
    rows = lax.broadcasted_iota(jnp.int32, (N_XHEADS * nl, N_MEM * N_XHEADS), 0)
    cols = lax.broadcasted_iota(jnp.int32, (N_XHEADS * nl, N_MEM * N_XHEADS), 1)
    same_head = (cols % N_XHEADS) == (rows // nl)
    outs = []
    for b in range(bt):
        q = q_ref[b * nl:(b + 1) * nl, :]
        qs = jnp.concatenate([q[:, h * XHEAD_DIM:(h + 1) * XHEAD_DIM] for h in range(N_XHEADS)], axis=0)
        k2 = k_ref[b].reshape(N_MEM * N_XHEADS, XHEAD_DIM).astype(BF16)
        v2 = v_ref[b].reshape(N_MEM * N_XHEADS, XHEAD_DIM).astype(BF16)
        s = lax.dot_general(qs.astype(BF16), k2, (((1,), (1,)), ((), ())), preferred_element_type=F32)
        p = _softmax(jnp.where(same_head, s * (XHEAD_DIM ** -0.5), -jnp.inf)).astype(BF16)
        o = _dot(p, v2)
        outs.append(jnp.concatenate([o[h * nl:(h + 1) * nl, :] for h in range(N_XHEADS)], axis=1))
    o_ref[...] = jnp.concatenate(outs, axis=0).astype(BF16)


def _attn_sample_call(q, k, v, l, nl):
    m = q.shape[0]
    bt = 4
    cache_spec = pl.BlockSpec((None, bt, N_MEM, N_XHEADS, XHEAD_DIM), lambda i: (l, i, 0, 0, 0))
    return pl.pallas_call(
        _attn_sample_kernel,
        grid=(m // (bt * nl),),
        in_specs=[pl.BlockSpec((bt * nl, D_C), lambda i: (i, 0)), cache_spec, cache_spec],
        out_specs=pl.BlockSpec((bt * nl, D_C), lambda i: (i, 0)),
        out_shape=jax.ShapeDtypeStruct((m, D_C), BF16),
        compiler_params=_params("parallel"),
        name="attn_sample",
    )(q, k, v)


def _mix_kernel(h_ref, a_ref, b_ref, c_ref, wpa_ref, wpb_ref, wpc_ref, wga_ref, wgb_ref, wgc_ref, o_ref):
    h = h_ref[...]
    b = b_ref[...].reshape(h.shape[0], D_B).astype(BF16)
    mix = jax.nn.sigmoid(_dot(h, wga_ref[...])) * _dot(a_ref[...], wpa_ref[...])
    mix = mix + jax.nn.sigmoid(_dot(h, wgb_ref[...])) * _dot(b, wpb_ref[...])
    mix = mix + jax.nn.sigmoid(_dot(h, wgc_ref[...])) * _dot(c_ref[...], wpc_ref[...])
    o_ref[...] = mix.astype(BF16)


def _mix_call(h, a, b, c, w_pa, w_pb, w_pc, w_in, name):
    m = h.shape[0]
    tm = min(TM, m)
    tn = 512
    gate_blk = OFF_GATES // tn
    per_gate = D_MODEL // tn
    row = lambda i, j: (i, 0)
    col = lambda i, j: (0, j)
    if b.ndim == 3:
        assert m == tm
        b_spec = pl.BlockSpec(b.shape, lambda i, j: (0, 0, 0))
    else:
        b_spec = pl.BlockSpec((tm, D_B), row)
    return pl.pallas_call(
        _mix_kernel,
        grid=(m // tm, D_MODEL // tn),
        in_specs=[pl.BlockSpec((tm, D_MODEL), row),
                  pl.BlockSpec((tm, D_A), row),
                  b_spec,
                  pl.BlockSpec((tm, D_C), row),
                  pl.BlockSpec((D_A, tn), col),
                  pl.BlockSpec((D_B, tn), col),
                  pl.BlockSpec((D_C, tn), col),
                  pl.BlockSpec((D_MODEL, tn), lambda i, j: (0, gate_blk + j)),
                  pl.BlockSpec((D_MODEL, tn), lambda i, j: (0, gate_blk + per_gate + j)),
                  pl.BlockSpec((D_MODEL, tn), lambda i, j: (0, gate_blk + 2 * per_gate + j))],
        out_specs=pl.BlockSpec((tm, tn), lambda i, j: (i, j)),
        out_shape=jax.ShapeDtypeStruct((m, D_MODEL), BF16),
        compiler_params=_params("parallel", "parallel"),
        name=name,
    )(h, a, b, c, w_pa, w_pb, w_pc, w_in, w_in, w_in)


def _oproj_kernel(x_ref, mix_ref, wo_ref, g_ref, x1_ref, h2_ref):
    x1 = x_ref[...].reshape(mix_ref.shape) + _dot(mix_ref[...], wo_ref[...])
    x1_ref[...] = x1
    h2_ref[...] = _rms(x1, g_ref[...]).astype(BF16)


def _oproj_call(x, mix, w_o, g_ffn, name):
    m = mix.shape[0]
    tm = min(TM, m)
    row = lambda i: (i, 0)
    if x.ndim == 3:
        assert m == tm
        x_spec = pl.BlockSpec(x.shape, lambda i: (0, 0, 0))
    else:
        x_spec = pl.BlockSpec((tm, D_MODEL), row)
    return pl.pallas_call(
        _oproj_kernel,
        grid=(m // tm,),
        in_specs=[x_spec,
                  pl.BlockSpec((tm, D_MODEL), row),
                  pl.BlockSpec((D_MODEL, D_MODEL), lambda i: (0, 0)),
                  pl.BlockSpec((1, D_MODEL), lambda i: (0, 0))],
        out_specs=[pl.BlockSpec((tm, D_MODEL), row),
                   pl.BlockSpec((tm, D_MODEL), row)],
        out_shape=[jax.ShapeDtypeStruct((m, D_MODEL), F32),
                   jax.ShapeDtypeStruct((m, D_MODEL), BF16)],
        compiler_params=_params("parallel"),
        name=name,
    )(x, mix, w_o, g_ffn.reshape(1, D_MODEL))


FFN_TM = 1024
FFN_SUB = 256


def _ffn_hidden(h, wa_ref, wg_ref, cw_ref, cb_ref, fx_ref, history):
    tm = h.shape[0]
    ps = []
    for c in range(wa_ref.shape[1] // FFN_SUB):
        cs = slice(c * FFN_SUB, (c + 1) * FFN_SUB)
        fa = _dot(h, wa_ref[:, cs])
        fg = _dot(h, wg_ref[:, cs])
        fx_ref[SUBLANE:, cs] = fa
        prev1, prev2 = history(cs, fx_ref[pl.ds(SUBLANE - 1, tm), cs], fx_ref[pl.ds(SUBLANE - 2, tm), cs])
        fc = cw_ref[2:3, cs] * fa + cw_ref[1:2, cs] * prev1 + cw_ref[0:1, cs] * prev2 + cb_ref[:, cs]
        ps.append((jax.nn.gelu(fc) * fg).astype(BF16))
    return jnp.concatenate(ps, axis=1)


def _ffn_prompt_kernel(h_ref, x1_hbm, wa_ref, wg_ref, wd_ref, cw_ref, cb_ref, g_ref, y_ref, tail_ref,
                       fx_ref, carry_ref, x1_ref, x1_sem, *, nt):
    tm = h_ref.shape[0]
    tf = wa_ref.shape[1]
    i = pl.program_id(0)
    j = pl.program_id(1)
    x1_copy = pltpu.make_async_copy(x1_hbm.at[pl.ds(i * tm, tm), :], x1_ref, x1_sem)

    @pl.when(j == 0)
    def _():
        x1_copy.start()
        y_ref[...] = jnp.zeros(y_ref.shape, F32)

    @pl.when(i % nt == 0)
    def _():
        fx_ref[0:SUBLANE, :] = jnp.zeros((SUBLANE, tf), F32)

    @pl.when(i % nt != 0)
    def _():
        fx_ref[0:SUBLANE, :] = carry_ref[j]

    p = _ffn_hidden(h_ref[...], wa_ref, wg_ref, cw_ref, cb_ref, fx_ref, lambda cs, p1, p2: (p1, p2))
    last = fx_ref[tm:tm + SUBLANE, :]
    carry_ref[j] = last
    tail_ref[0] = last
    y_ref[...] += _dot(p, wd_ref[...])

    @pl.when(j == pl.num_programs(1) - 1)
    def _():
        x1_copy.wait()

        def norm_rows(r, carry):
            rs = pl.ds(pl.multiple_of(r * CHUNK, CHUNK), CHUNK)
            y_ref[rs, :] = _rms(x1_ref[rs, :] + y_ref[rs, :], g_ref[...])
            return carry

        lax.fori_loop(0, tm // CHUNK, norm_rows, 0)


def _ffn_prompt_call(h2, x1, wa, wg, wd, cw, cb, g_final, nb):
    m = h2.shape[0]
    tm = FFN_TM
    nt = m // nb // tm
    nj = D_FF_PAD // TF
    row = lambda i, j: (i, 0)
    col = lambda i, j: (0, j)
    return pl.pallas_call(
        functools.partial(_ffn_prompt_kernel, nt=nt),
        grid=(m // tm, nj),
        in_specs=[pl.BlockSpec((tm, D_MODEL), row),
                  pl.BlockSpec(memory_space=pl.ANY),
                  pl.BlockSpec((D_MODEL, TF), col),
                  pl.BlockSpec((D_MODEL, TF), col),
                  pl.BlockSpec((TF, D_MODEL), lambda i, j: (j, 0)),
                  pl.BlockSpec((CONV_F, TF), col),
                  pl.BlockSpec((1, TF), col),
                  pl.BlockSpec((1, D_MODEL), lambda i, j: (0, 0))],
        out_specs=[pl.BlockSpec((tm, D_MODEL), row),
                   pl.BlockSpec((1, SUBLANE, TF), lambda i, j: (i, 0, j))],
        out_shape=[jax.ShapeDtypeStruct((m, D_MODEL), F32),
                   jax.ShapeDtypeStruct((m // tm, SUBLANE, D_FF_PAD), F32)],
        scratch_shapes=[pltpu.VMEM((tm + SUBLANE, TF), F32),
                        pltpu.VMEM((nj, SUBLANE, TF), F32),
                        pltpu.VMEM((tm, D_MODEL), F32),
                        pltpu.SemaphoreType.DMA(())],
        compiler_params=_params("arbitrary", "arbitrary"),
        name="ffn_prompt",
    )(h2, x1, wa, wg, wd, cw, cb, g_final.reshape(1, D_MODEL))


def _ffn_sample_kernel(h_ref, x1_ref, wa_ref, wg_ref, wd_ref, cw_ref, cb_ref, g_ref, hist_ref,
                       y_ref, st_ref, fx_ref, h1_ref, h2_ref, fa3_ref):
    nb, nl, tf = fa3_ref.shape
    j = pl.program_id(1)
    live = (j * tf + lax.broadcasted_iota(jnp.int32, (nb, tf), 1)) < D_FF
    older = jnp.where(live, hist_ref[:, 0, :], 0.0)
    newer = jnp.where(live, hist_ref[:, 1, :], 0.0)
    h1_ref[...] = jnp.zeros_like(h1_ref)
    h2_ref[...] = jnp.zeros_like(h2_ref)
    h1_ref[:, 0, :] = newer
    h2_ref[:, 0, :] = older
    h2_ref[:, 1, :] = newer
    fx_ref[0:SUBLANE, :] = jnp.zeros((SUBLANE, tf), F32)
    t = lax.broadcasted_iota(jnp.int32, (nb * nl, FFN_SUB), 0) % nl

    def history(cs, prev1, prev2):
        h1 = h1_ref[:, :, cs].reshape(nb * nl, FFN_SUB)
        h2 = h2_ref[:, :, cs].reshape(nb * nl, FFN_SUB)
        return jnp.where(t >= 1, prev1, h1), jnp.where(t >= 2, prev2, h2)

    p = _ffn_hidden(h_ref[...], wa_ref, wg_ref, cw_ref, cb_ref, fx_ref, history)
    fa3_ref[...] = fx_ref[SUBLANE:, :].reshape(nb, nl, tf)
    for r in range(CONV_F - 1):
        st_ref[:, r, :] = fa3_ref[:, nl - (CONV_F - 1) + r, :]

    @pl.when(j == 0)
    def _():
        y_ref[...] = x1_ref[...]

    y_ref[...] += _dot(p, wd_ref[...])

    @pl.when(j == pl.num_programs(1) - 1)
    def _():
        y_ref[...] = _rms(y_ref[...], g_ref[...])


def _ffn_sample_call(h2, x1, wa, wg, wd, cw, cb, g_final, state, l, nb):
    m = h2.shape[0]
    nl = m // nb
    nj = D_FF_PAD // TF
    row = lambda i, j: (i, 0)
    col = lambda i, j: (0, j)
    return pl.pallas_call(
        _ffn_sample_kernel,
        grid=(1, nj),
        in_specs=[pl.BlockSpec((m, D_MODEL), row, pipeline_mode=pl.Buffered(1)),
                  pl.BlockSpec((m, D_MODEL), row, pipeline_mode=pl.Buffered(1)),
                  pl.BlockSpec((D_MODEL, TF), col),
                  pl.BlockSpec((D_MODEL, TF), col),
                  pl.BlockSpec((TF, D_MODEL), lambda i, j: (j, 0)),
                  pl.BlockSpec((CONV_F, TF), col),
                  pl.BlockSpec((1, TF), col),
                  pl.BlockSpec((1, D_MODEL), lambda i, j: (0, 0)),
                  pl.BlockSpec((None, nb, CONV_F - 1, TF), lambda i, j: (l, 0, 0, j))],
        out_specs=[pl.BlockSpec((m, D_MODEL), row),
                   pl.BlockSpec((None, nb, CONV_F - 1, TF), lambda i, j: (0, 0, 0, j))],
        out_shape=[jax.ShapeDtypeStruct((m, D_MODEL), F32),
                   jax.ShapeDtypeStruct((1, nb, CONV_F - 1, D_FF), F32)],
        scratch_shapes=[pltpu.VMEM((m + SUBLANE, TF), F32),
                        pltpu.VMEM((nb, nl, TF), F32),
                        pltpu.VMEM((nb, nl, TF), F32),
                        pltpu.VMEM((nb, nl, TF), F32)],
        compiler_params=_params("arbitrary", "arbitrary"),
        name="ffn_sample",
    )(h2, x1, wa, wg, wd, cw, cb, g_final.reshape(1, D_MODEL), state)


def kernel(x_prompt, x_sample, mem_prompt, cache_mem_k, cache_mem_v, state_conv, state_ffn_conv, g_mix, w_in, ln_v_g, ln_v_b, w_s, b_s, w_pa, conv_w, conv_b, ln_b_g, ln_b_b, w_pb, g_mem, w_k, w_v, w_pc, w_o, g_ffn, w_up, ffn_conv_w, ffn_conv_b, w_down, g_final):
    depth = g_mix.shape[0]
    assert depth == 1
    l = 0
    nbp, lp, _ = x_prompt.shape
    nbs, ls, _ = x_sample.shape
    mp = nbp * lp
    pad_ff = D_FF_PAD - D_FF

    w_in_b = w_in[l].astype(BF16)
    w_pa_b = w_pa[l].astype(BF16)
    w_pb_b = w_pb[l].astype(BF16)
    w_pc_b = w_pc[l].astype(BF16)
    w_k_b = w_k[l].astype(BF16)
    w_v_b = w_v[l].astype(BF16)
    w_o_b = w_o[l].astype(BF16)
    w_up_a = jnp.pad(w_up[l][:, :D_FF], ((0, 0), (0, pad_ff))).astype(BF16)
    w_up_g = jnp.pad(w_up[l][:, D_FF:], ((0, 0), (0, pad_ff))).astype(BF16)
    w_down_b = jnp.pad(w_down[l], ((0, pad_ff), (0, 0))).astype(BF16)
    cw = jnp.pad(ffn_conv_w[l], ((0, 0), (0, pad_ff)))
    cb = jnp.pad(ffn_conv_b[l], (0, pad_ff)).reshape(1, D_FF_PAD)

    reps = CHUNK // ls
    ws_p = w_s[l]
    ws_s = jnp.tile(w_s[l][:, :ls, :ls], (1, reps, reps))
    bs_p = jnp.repeat(b_s[l].T, D_A // G_A, axis=1)
    bs_s = jnp.repeat(jnp.tile(b_s[l][:, :ls], (1, reps)).T, D_A // G_A, axis=1)

    xp = x_prompt.reshape(mp, D_MODEL)
    h, a, bact, q, ctail = _inproj_prompt_call(xp, nbp, g_mix[l], w_in_b, ln_v_g[l], ln_v_b[l], ws_p, bs_p,
                                               conv_w[l], conv_b[l], ln_b_g[l], ln_b_b[l])
    k5, v5, kb, vb = _memkv_call(mem_prompt, g_mem[l], w_k_b, w_v_b)
    cact = _attn_prompt_call(q, kb, vb)
    mix = _mix_call(h, a, bact, cact, w_pa_b, w_pb_b, w_pc_b, w_in_b, "mix_prompt")
    x1, h2 = _oproj_call(xp, mix, w_o_b, g_ffn[l], "oproj_prompt")
    yp, tail = _ffn_prompt_call(h2, x1, w_up_a, w_up_g, w_down_b, cw, cb, g_final, nbp)
    nt = tail.shape[0] // nbp
    ffn_p = tail.reshape(nbp, nt, SUBLANE, D_FF_PAD)[:, nt - 1, SUBLANE - (CONV_F - 1):, :D_FF]
    nt = ctail.shape[0] // nbp
    conv_p = ctail.reshape(nbp, nt, HALO, D_B)[:, nt - 1, HALO - (CONV_B - 1):]

    h, a, q, glu_s, vn_s = _inproj_sample_call(x_sample, g_mix[l], w_in_b, ln_v_g[l], ln_v_b[l], ws_s, bs_s)
    bact, conv_s = _convb_sample_call(state_conv, l, glu_s, conv_w[l], conv_b[l], ln_b_g[l], ln_b_b[l])
    cact = _attn_sample_call(q, cache_mem_k, cache_mem_v, l, ls)
    mix = _mix_call(h, a, bact, cact, w_pa_b, w_pb_b, w_pc_b, w_in_b, "mix_sample")
    x1, h2 = _oproj_call(x_sample, mix, w_o_b, g_ffn[l], "oproj_sample")
    ys, ffn_s = _ffn_sample_call(h2, x1, w_up_a, w_up_g, w_down_b, cw, cb, g_final, state_ffn_conv, l, nbs)

    return (yp.reshape(nbp, lp, D_MODEL), ys.reshape(nbs, ls, D_MODEL), k5, v5, conv_p[None], ffn_p[None],
            conv_s, ffn_s, vn_s)
```

```python
import functools

import jax
import jax.numpy as jnp
from jax import lax
from jax.experimental import pallas as pl
from jax.experimental.pallas import tpu as pltpu

F32 = jnp.float32
BF16 = jnp.bfloat16

D_MODEL = 2048
CHUNK = 128
D_A = D_MODEL // 2
G_A = 8
D_B = D_MODEL // 2
CONV_B = 31
N_MEM = 256
N_XHEADS = 4
XHEAD_DIM = D_MODEL // 8
D_C = N_XHEADS * XHEAD_DIM
D_FF = ((8 * D_MODEL // 3 + 127) // 128) * 128
CONV_F = 3
EPS = 1e-6

OFF_ZA = 0
OFF_ZB = 2 * D_A
OFF_Q = OFF_ZB + 2 * D_B
OFF_GATES = OFF_Q + D_C

LANE = 128
SUBLANE = 8
TM = 512
TF = 512
D_FF_PAD = ((D_FF + TF - 1) // TF) * TF
HALO = 32
VMEM_LIMIT = 60 * 1024 * 1024


def _params(*sem):
    return pltpu.CompilerParams(dimension_semantics=sem, vmem_limit_bytes=VMEM_LIMIT)


def _rms(x, g):
    return x * lax.rsqrt(jnp.mean(x * x, axis=-1, keepdims=True) + EPS) * g


def _layer_norm(x, g, b):
    mu = jnp.mean(x, axis=-1, keepdims=True)
    d = x - mu
    var = jnp.mean(d * d, axis=-1, keepdims=True)
    return d * lax.rsqrt(var + EPS) * g + b


def _dot(a, b):
    return jnp.dot(a, b, preferred_element_type=F32)


def _softmax(s):
    e = jnp.exp(s - jnp.max(s, axis=-1, keepdims=True))
    return e / jnp.sum(e, axis=-1, keepdims=True)


def _spatial_gate(u, vb, ws_ref, bs_ref, a_ref, sample):
    tm = vb.shape[0]
    r = lax.broadcasted_iota(jnp.int32, (CHUNK, CHUNK), 0)
    c = lax.broadcasted_iota(jnp.int32, (CHUNK, CHUNK), 1)
    mask = r >= c
    if sample:
        mask = mask & ((r >> 2) == (c >> 2))
    for g in range(G_A):
        cs = slice(g * LANE, (g + 1) * LANE)
        wm = jnp.where(mask, ws_ref[g], 0.0).astype(BF16)
        for ch in range(tm // CHUNK):
            rs = slice(ch * CHUNK, (ch + 1) * CHUNK)
            s = _dot(wm, vb[rs, cs]) + bs_ref[:, cs]
            a_ref[rs, cs] = (u[rs, cs] * s).astype(BF16)


IN_SUB = 256
CONV_ROWS = 128


def _conv_taps(xc_ref, sh_ref, w_ref, bias, r0, cs):
    lead = HALO - (CONV_B - 1)
    acc = jnp.broadcast_to(bias, (CONV_ROWS, bias.shape[1]))
    for s in range(SUBLANE):
        qs = [q for q in range(HALO // SUBLANE + 1) if lead <= SUBLANE * q + s < lead + CONV_B]
        n = CONV_ROWS + SUBLANE * qs[-1]
        if s:
            sh_ref[0:n, :] = xc_ref[pl.ds(r0 + s, n), cs]
        for q in qs:
            k = SUBLANE * q + s - lead
            if s:
                win = sh_ref[SUBLANE * q:SUBLANE * q + CONV_ROWS, :]
            else:
                win = xc_ref[r0 + SUBLANE * q:r0 + SUBLANE * q + CONV_ROWS, cs]
            acc = acc + w_ref[k:k + 1, cs] * win
    return acc


def _inproj_prompt_kernel(x_ref, gm_ref, wa_ref, wb_ref, wq_ref, lvg_ref, lvb_ref, ws_ref, bs_ref,
                          cw_ref, cb_ref, lbg_ref, lbb_ref, wpa_f, wpb_f, wpc_f, wo_f,
                          h_ref, a_ref, b_ref, q_ref, ctail_ref, wpa_o, wpb_o, wpc_o, wo_o,
                          xc_ref, cv_ref, sh_ref, u_ref, v_ref, *, nt):
    tm = x_ref.shape[0]
    t = pl.program_id(0) % nt
    for src, dst in ((wpa_f, wpa_o), (wpb_f, wpb_o), (wpc_f, wpc_o), (wo_f, wo_o)):
        dst[...] = src[...].astype(BF16)

    @pl.when(t == 0)
    def _():
        xc_ref[0:HALO, :] = jnp.zeros((HALO, D_B), F32)

    @pl.when(t != 0)
    def _():
        xc_ref[0:HALO, :] = xc_ref[tm:tm + HALO, :]

    h = _rms(x_ref[...], gm_ref[...]).astype(BF16)
    h_ref[...] = h

    for c in range(D_B // IN_SUB):
        cs = slice(c * IN_SUB, (c + 1) * IN_SUB)
        za = _dot(h, wb_ref[:, cs])
        zb = _dot(h, wb_ref[:, D_B + c * IN_SUB:D_B + (c + 1) * IN_SUB])
        xc_ref[HALO:, cs] = za * jax.nn.sigmoid(zb)
        for rb in range(tm // CONV_ROWS):
            r0 = rb * CONV_ROWS
            cv_ref[r0:r0 + CONV_ROWS, cs] = _conv_taps(xc_ref, sh_ref, cw_ref, cb_ref[:, cs], r0, cs)
    ctail_ref[0] = xc_ref[tm:tm + HALO, :]
    y = _layer_norm(cv_ref[...], lbg_ref[...], lbb_ref[...])
    b_ref[...] = (y * jax.nn.sigmoid(y)).astype(BF16)

    for c in range(D_A // IN_SUB):
        cs = slice(c * IN_SUB, (c + 1) * IN_SUB)
        u_ref[:, cs] = jax.nn.gelu(_dot(h, wa_ref[:, cs]))
        v_ref[:, cs] = jax.nn.gelu(_dot(h, wa_ref[:, D_A + c * IN_SUB:D_A + (c + 1) * IN_SUB]))
    q_ref[...] = _dot(h, wq_ref[...]).astype(BF16)
    vb = _layer_norm(v_ref[...], lvg_ref[...], lvb_ref[...]).astype(BF16)
    _spatial_gate(u_ref, vb, ws_ref, bs_ref, a_ref, False)


def _inproj_weight_specs(const_map):
    single = pl.Buffered(1)
    return [pl.BlockSpec((D_MODEL, 2 * D_A), lambda *i: (0, OFF_ZA // (2 * D_A)), pipeline_mode=single),
            pl.BlockSpec((D_MODEL, 2 * D_B), lambda *i: (0, OFF_ZB // (2 * D_B)), pipeline_mode=single),
            pl.BlockSpec((D_MODEL, D_C), lambda *i: (0, OFF_Q // D_C), pipeline_mode=single)]


def _inproj_prompt_call(x, nb, g_mix, w_in, ln_v_g, ln_v_b, ws, bs, conv_w, conv_b, ln_b_g, ln_b_b, casts):
    m = x.shape[0]
    tm = TM
    nt = m // nb // tm
    steps = m // tm
    slab = lambda w: pl.BlockSpec((w.shape[0] // steps, w.shape[1]), lambda i: (i, 0))
    const = lambda i: (0, 0)
    row = lambda i: (i, 0)
    vec = lambda n: pl.BlockSpec((1, n), const)
    return pl.pallas_call(
        functools.partial(_inproj_prompt_kernel, nt=nt),
        grid=(m // tm,),
        in_specs=[pl.BlockSpec((tm, D_MODEL), row), vec(D_MODEL)] + _inproj_weight_specs(const) + [
                  vec(D_A), vec(D_A),
                  pl.BlockSpec((G_A, CHUNK, CHUNK), lambda i: (0, 0, 0)),
                  pl.BlockSpec((CHUNK, D_A), const),
                  pl.BlockSpec((CONV_B, D_B), const),
                  vec(D_B), vec(D_B), vec(D_B)] + [slab(w) for w in casts],
        out_specs=[pl.BlockSpec((tm, D_MODEL), row),
                   pl.BlockSpec((tm, D_A), row),
                   pl.BlockSpec((tm, D_B), row),
                   pl.BlockSpec((tm, D_C), row),
                   pl.BlockSpec((1, HALO, D_B), lambda i: (i, 0, 0))] + [slab(w) for w in casts],
        out_shape=[jax.ShapeDtypeStruct((m, D_MODEL), BF16),
                   jax.ShapeDtypeStruct((m, D_A), BF16),
                   jax.ShapeDtypeStruct((m, D_B), BF16),
                   jax.ShapeDtypeStruct((m, D_C), BF16),
                   jax.ShapeDtypeStruct((m // tm, HALO, D_B), F32)]
                  + [jax.ShapeDtypeStruct(w.shape, BF16) for w in casts],
        scratch_shapes=[pltpu.VMEM((tm + HALO, D_B), F32),
                        pltpu.VMEM((tm, D_B), F32),
                        pltpu.VMEM((CONV_ROWS + HALO, IN_SUB), F32),
                        pltpu.VMEM((tm, D_A), F32),
                        pltpu.VMEM((tm, D_A), F32)],
        compiler_params=_params("arbitrary"),
        name="inproj_prompt",
    )(x, g_mix.reshape(1, D_MODEL), w_in, w_in, w_in, ln_v_g.reshape(1, D_A), ln_v_b.reshape(1, D_A), ws, bs,
      conv_w, conv_b.reshape(1, D_B), ln_b_g.reshape(1, D_B), ln_b_b.reshape(1, D_B), *casts)


def _inproj_sample_kernel(x_ref, gm_ref, wa_ref, wb_ref, wq_ref, lvg_ref, lvb_ref, ws_ref, bs_ref,
                          h_ref, a_ref, q_ref, glu_ref, vn_ref):
    nb, nl, _ = x_ref.shape
    m = nb * nl
    h = _rms(x_ref[...].reshape(m, D_MODEL), gm_ref[...]).astype(BF16)
    h_ref[...] = h
    zg = jax.nn.gelu(_dot(h, wa_ref[...]))
    vn = _layer_norm(zg[:, D_A:], lvg_ref[...], lvb_ref[...])
    vn_ref[...] = vn.reshape(nb, nl, D_A)
    _spatial_gate(zg[:, :D_A], vn.astype(BF16), ws_ref, bs_ref, a_ref, True)
    z = _dot(h, wb_ref[...])
    glu_ref[...] = (z[:, :D_B] * jax.nn.sigmoid(z[:, D_B:])).reshape(nb, nl, D_B)
    q_ref[...] = _dot(h, wq_ref[...])


def _inproj_sample_call(x, g_mix, w_in, ln_v_g, ln_v_b, ws, bs):
    nb, nl, _ = x.shape
    m = nb * nl
    bt = CHUNK // nl
    const = lambda i: (0, 0)
    vec = lambda n: pl.BlockSpec((1, n), const)
    flat = lambda n: pl.BlockSpec((bt * nl, n), lambda i: (i, 0))
    return pl.pallas_call(
        _inproj_sample_kernel,
        grid=(nb // bt,),
        in_specs=[pl.BlockSpec((bt, nl, D_MODEL), lambda i: (i, 0, 0)), vec(D_MODEL)]
                 + _inproj_weight_specs(const) + [
                  vec(D_A), vec(D_A),
                  pl.BlockSpec((G_A, CHUNK, CHUNK), lambda i: (0, 0, 0)),
                  pl.BlockSpec((CHUNK, D_A), const)],
        out_specs=[flat(D_MODEL), flat(D_A), flat(D_C),
                   pl.BlockSpec((bt, nl, D_B), lambda i: (i, 0, 0)),
                   pl.BlockSpec((None, bt, nl, D_A), lambda i: (0, i, 0, 0))],
        out_shape=[jax.ShapeDtypeStruct((m, D_MODEL), BF16),
                   jax.ShapeDtypeStruct((m, D_A), BF16),
                   jax.ShapeDtypeStruct((m, D_C), F32),
                   jax.ShapeDtypeStruct((nb, nl, D_B), F32),
                   jax.ShapeDtypeStruct((1, nb, nl, D_A), F32)],
        compiler_params=_params("arbitrary"),
        name="inproj_sample",
    )(x, g_mix.reshape(1, D_MODEL), w_in, w_in, w_in, ln_v_g.reshape(1, D_A), ln_v_b.reshape(1, D_A), ws, bs)


def _convb_sample_kernel(hist_ref, x_ref, w_ref, cb_ref, lng_ref, lnb_ref, o_ref, nh_ref):
    bt, nl, _ = x_ref.shape
    nh = hist_ref.shape[0]
    new_rows = [x_ref[:, t, :] for t in range(nl)]
    window = lambda j: hist_ref[j] if j < nh else new_rows[j - nh]
    for t in range(nl):
        acc = jnp.broadcast_to(cb_ref[...], (bt, D_B))
        for k in range(CONV_B):
            acc = acc + w_ref[k:k + 1, :] * window(t + k)
        y = _layer_norm(acc, lng_ref[...], lnb_ref[...])
        o_ref[:, t, :] = y * jax.nn.sigmoid(y)
    for j in range(nh):
        nh_ref[j] = window(j + nl)


def _convb_sample_call(hist, l, x, conv_w, conv_b, ln_g, ln_b):
    nb, nl, _ = x.shape
    nh = hist.shape[1]
    bt = 16
    const = lambda i: (0, 0)
    state_spec = pl.BlockSpec((None, nh, bt, D_B), lambda i: (l, 0, i, 0))
    new_spec = pl.BlockSpec((bt, nl, D_B), lambda i: (i, 0, 0))
    return pl.pallas_call(
        _convb_sample_kernel,
        grid=(nb // bt,),
        in_specs=[state_spec, new_spec,
                  pl.BlockSpec((CONV_B, D_B), const),
                  pl.BlockSpec((1, D_B), const),
                  pl.BlockSpec((1, D_B), const),
                  pl.BlockSpec((1, D_B), const)],
        out_specs=[new_spec, pl.BlockSpec((None, nh, bt, D_B), lambda i: (0, 0, i, 0))],
        out_shape=[jax.ShapeDtypeStruct((nb, nl, D_B), F32),
                   jax.ShapeDtypeStruct((1, nh, nb, D_B), F32)],
        compiler_params=_params("parallel"),
        name="convb_sample",
    )(hist, x, conv_w, conv_b.reshape(1, D_B), ln_g.reshape(1, D_B), ln_b.reshape(1, D_B))


def _memkv_kernel(m_ref, g_ref, wk_ref, wv_ref, k5_ref, v5_ref, kb_ref, vb_ref):
    mn = _rms(m_ref[...], g_ref[...]).astype(BF16)
    k = _dot(mn, wk_ref[...])
    v = _dot(mn, wv_ref[...])
    k5_ref[...] = k.reshape(N_MEM, N_XHEADS, XHEAD_DIM)
    v5_ref[...] = v.reshape(N_MEM, N_XHEADS, XHEAD_DIM)
    kb_ref[...] = k.astype(BF16)
    vb_ref[...] = v.astype(BF16)


def _memkv_call(mem, g_mem, w_k, w_v):
    nb = mem.shape[0]
    const = lambda i: (0, 0)
    cache_spec = pl.BlockSpec((None, None, N_MEM, N_XHEADS, XHEAD_DIM), lambda i: (0, i, 0, 0, 0))
    seq_spec = pl.BlockSpec((None, N_MEM, D_C), lambda i: (i, 0, 0))
    cache_shape = jax.ShapeDtypeStruct((1, nb, N_MEM, N_XHEADS, XHEAD_DIM), F32)
    return pl.pallas_call(
        _memkv_kernel,
        grid=(nb,),
        in_specs=[pl.BlockSpec((None, N_MEM, D_MODEL), lambda i: (i, 0, 0)),
                  pl.BlockSpec((1, D_MODEL), const),
                  pl.BlockSpec((D_MODEL, D_C), const),
                  pl.BlockSpec((D_MODEL, D_C), const)],
        out_specs=[cache_spec, cache_spec, seq_spec, seq_spec],
        out_shape=[cache_shape, cache_shape,
                   jax.ShapeDtypeStruct((nb, N_MEM, D_C), BF16),
                   jax.ShapeDtypeStruct((nb, N_MEM, D_C), BF16)],
        compiler_params=_params("parallel"),
        name="memory_kv",
    )(mem, g_mem.reshape(1, D_MODEL), w_k, w_v)


def _attn_prompt_kernel(q_ref, k_ref, v_ref, o_ref):
    for h in range(N_XHEADS):
        cs = slice(h * XHEAD_DIM, (h + 1) * XHEAD_DIM)
        s = lax.dot_general(q_ref[:, cs], k_ref[:, cs], (((1,), (1,)), ((), ())), preferred_element_type=F32)
        p = _softmax(s * (XHEAD_DIM ** -0.5)).astype(BF16)
        o_ref[:, cs] = _dot(p, v_ref[:, cs]).astype(BF16)


def _attn_prompt_call(q, k, v):
    m = q.shape[0]
    nb = k.shape[0]
    tm = TM
    nt = m // nb // tm
    return pl.pallas_call(
        _attn_prompt_kernel,
        grid=(nb, nt),
        in_specs=[pl.BlockSpec((tm, D_C), lambda b, t: (b * nt + t, 0)),
                  pl.BlockSpec((None, N_MEM, D_C), lambda b, t: (b, 0, 0)),
                  pl.BlockSpec((None, N_MEM, D_C), lambda b, t: (b, 0, 0))],
        out_specs=pl.BlockSpec((tm, D_C), lambda b, t: (b * nt + t, 0)),
        out_shape=jax.ShapeDtypeStruct((m, D_C), BF16),
        compiler_params=_params("parallel", "parallel"),
        name="attn_prompt",
    )(q, k, v)


def _attn_sample_kernel(q_ref, k_ref, v_ref, o_ref):
    bt = k_ref.shape[0]
    nl = q_ref.shape[0] // bt
    rows = lax.broadcasted_iota(jnp.int32, (N_XHEADS * nl, N_MEM * N_XHEADS), 0)
    cols = lax.broadcasted_iota(jnp.int32, (N_XHEADS * nl, N_MEM * N_XHEADS), 1)
    same_head = (cols % N_XHEADS) == (rows // nl)
    outs = []
    for b in range(bt):
        q = q_ref[b * nl:(b + 1) * nl, :]
        qs = jnp.concatenate([q[:, h * XHEAD_DIM:(h + 1) * XHEAD_DIM] for h in range(N_XHEADS)], axis=0)
        k2 = k_ref[b].reshape(N_MEM * N_XHEADS, XHEAD_DIM).astype(BF16)
        v2 = v_ref[b].reshape(N_MEM * N_XHEADS, XHEAD_DIM).astype(BF16)
        s = lax.dot_general(qs.astype(BF16), k2, (((1,), (1,)), ((), ())), preferred_element_type=F32)
        p = _softmax(jnp.where(same_head, s * (XHEAD_DIM ** -0.5), -jnp.inf)).astype(BF16)
        o = _dot(p, v2)
        outs.append(jnp.concatenate([o[h * nl:(h + 1) * nl, :] for h in range(N_XHEADS)], axis=1))
    o_ref[...] = jnp.concatenate(outs, axis=0).astype(BF16)


def _attn_sample_call(q, k, v, l, nl):
    m = q.shape[0]
    bt = 4
    cache_spec = pl.BlockSpec((None, bt, N_MEM, N_XHEADS, XHEAD_DIM), lambda i: (l, i, 0, 0, 0))
    return pl.pallas_call(
        _attn_sample_kernel,
        grid=(m // (bt * nl),),
        in_specs=[pl.BlockSpec((bt * nl, D_C), lambda i: (i, 0)), cache_spec, cache_spec],
        out_specs=pl.BlockSpec((bt * nl, D_C), lambda i: (i, 0)),
        out_shape=jax.ShapeDtypeStruct((m, D_C), BF16),
        compiler_params=_params("parallel"),
        name="attn_sample",
    )(q, k, v)


def _cast_ffn_weights(wu_f, wd_f, wa_o, wg_o, wd_o, step):
    wu = wu_f[...]
    zpad = jnp.zeros((wu.shape[0], D_FF_PAD - D_FF), BF16)
    wa_o[:, :D_FF] = wu[:, :D_FF].astype(BF16)
    wa_o[:, D_FF:] = zpad
    wg_o[:, :D_FF] = wu[:, D_FF:].astype(BF16)
    wg_o[:, D_FF:] = zpad
    live_blocks = D_FF // wd_f.shape[0]

    @pl.when(step < live_blocks)
    def _():
        wd_o[...] = wd_f[...].astype(BF16)

    @pl.when(step >= live_blocks)
    def _():
        wd_o[...] = jnp.zeros(wd_o.shape, BF16)


def _mix_kernel(*refs, cast_ffn):
    h_ref, a_ref, b_ref, c_ref, wpa_ref, wpb_ref, wpc_ref, wga_ref, wgb_ref, wgc_ref = refs[:10]
    if cast_ffn:
        wu_f, wd_f, o_ref, wa_o, wg_o, wd_o = refs[10:]
        _cast_ffn_weights(wu_f, wd_f, wa_o, wg_o, wd_o, pl.program_id(0) * pl.num_programs(1) + pl.program_id(1))
    else:
        (o_ref,) = refs[10:]
    h = h_ref[...]
    b = b_ref[...].reshape(h.shape[0], D_B).astype(BF16)
    mix = jax.nn.sigmoid(_dot(h, wga_ref[...])) * _dot(a_ref[...], wpa_ref[...])
    mix = mix + jax.nn.sigmoid(_dot(h, wgb_ref[...])) * _dot(b, wpb_ref[...])
    mix = mix + jax.nn.sigmoid(_dot(h, wgc_ref[...])) * _dot(c_ref[...], wpc_ref[...])
    o_ref[...] = mix.astype(BF16)


def _mix_call(h, a, b, c, w_pa, w_pb, w_pc, w_in, name, ffn_weights=None):
    m = h.shape[0]
    tm = min(TM, m)
    tn = 512
    nj = D_MODEL // tn
    gate_blk = OFF_GATES // tn
    per_gate = D_MODEL // tn
    row = lambda i, j: (i, 0)
    col = lambda i, j: (0, j)
    if b.ndim == 3:
        assert m == tm
        b_spec = pl.BlockSpec(b.shape, lambda i, j: (0, 0, 0))
    else:
        b_spec = pl.BlockSpec((tm, D_B), row)
    in_specs = [pl.BlockSpec((tm, D_MODEL), row),
                pl.BlockSpec((tm, D_A), row),
                b_spec,
                pl.BlockSpec((tm, D_C), row),
                pl.BlockSpec((D_A, tn), col),
                pl.BlockSpec((D_B, tn), col),
                pl.BlockSpec((D_C, tn), col),
                pl.BlockSpec((D_MODEL, tn), lambda i, j: (0, gate_blk + j)),
                pl.BlockSpec((D_MODEL, tn), lambda i, j: (0, gate_blk + per_gate + j)),
                pl.BlockSpec((D_MODEL, tn), lambda i, j: (0, gate_blk + 2 * per_gate + j))]
    out_specs = [pl.BlockSpec((tm, tn), lambda i, j: (i, j))]
    out_shape = [jax.ShapeDtypeStruct((m, D_MODEL), BF16)]
    args = [h, a, b, c, w_pa, w_pb, w_pc, w_in, w_in, w_in]
    if ffn_weights is not None:
        w_up, w_down = ffn_weights
        steps = (m // tm) * nj
        up_rows = D_MODEL // steps
        down_rows = CHUNK
        down_blocks = D_FF_PAD // down_rows
        assert D_MODEL % steps == 0 and D_FF % down_rows == 0 and down_blocks <= steps
        step = lambda i, j: i * nj + j
        in_specs += [pl.BlockSpec((up_rows, 2 * D_FF), lambda i, j: (step(i, j), 0)),
                     pl.BlockSpec((down_rows, D_MODEL),
                                  lambda i, j: (jnp.minimum(step(i, j), D_FF // down_rows - 1), 0))]
        out_specs += [pl.BlockSpec((up_rows, D_FF_PAD), lambda i, j: (step(i, j), 0)),
                      pl.BlockSpec((up_rows, D_FF_PAD), lambda i, j: (step(i, j), 0)),
                      pl.BlockSpec((down_rows, D_MODEL),
                                   lambda i, j: (jnp.minimum(step(i, j), down_blocks - 1), 0))]
        out_shape += [jax.ShapeDtypeStruct((D_MODEL, D_FF_PAD), BF16),
                      jax.ShapeDtypeStruct((D_MODEL, D_FF_PAD), BF16),
                      jax.ShapeDtypeStruct((D_FF_PAD, D_MODEL), BF16)]
        args += [w_up, w_down]
    return pl.pallas_call(
        functools.partial(_mix_kernel, cast_ffn=ffn_weights is not None),
        grid=(m // tm, nj),
        in_specs=in_specs,
        out_specs=out_specs,
        out_shape=out_shape,
        compiler_params=_params("arbitrary", "arbitrary"),
        name=name,
    )(*args)


def _oproj_kernel(x_ref, mix_ref, wo_ref, g_ref, x1_ref, h2_ref):
    x1 = x_ref[...].reshape(mix_ref.shape) + _dot(mix_ref[...], wo_ref[...])
    x1_ref[...] = x1
    h2_ref[...] = _rms(x1, g_ref[...]).astype(BF16)


def _oproj_call(x, mix, w_o, g_ffn, name):
    m = mix.shape[0]
    tm = min(TM, m)
    row = lambda i: (i, 0)
    if x.ndim == 3:
        assert m == tm
        x_spec = pl.BlockSpec(x.shape, lambda i: (0, 0, 0))
    else:
        x_spec = pl.BlockSpec((tm, D_MODEL), row)
    return pl.pallas_call(
        _oproj_kernel,
        grid=(m // tm,),
        in_specs=[x_spec,
                  pl.BlockSpec((tm, D_MODEL), row),
                  pl.BlockSpec((D_MODEL, D_MODEL), lambda i: (0, 0)),
                  pl.BlockSpec((1, D_MODEL), lambda i: (0, 0))],
        out_specs=[pl.BlockSpec((tm, D_MODEL), row),
                   pl.BlockSpec((tm, D_MODEL), row)],
        out_shape=[jax.ShapeDtypeStruct((m, D_MODEL), F32),
                   jax.ShapeDtypeStruct((m, D_MODEL), BF16)],
        compiler_params=_params("parallel"),
        name=name,
    )(x, mix, w_o, g_ffn.reshape(1, D_MODEL))


FFN_TM = 1024
FFN_SUB = 256


def _ffn_hidden(h, wa_ref, wg_ref, cw_ref, cb_ref, fx_ref, history):
    tm = h.shape[0]
    ps = []
    for c in range(wa_ref.shape[1] // FFN_SUB):
        cs = slice(c * FFN_SUB, (c + 1) * FFN_SUB)
        fa = _dot(h, wa_ref[:, cs])
        fg = _dot(h, wg_ref[:, cs])
        fx_ref[SUBLANE:, cs] = fa
        prev1, prev2 = history(cs, fx_ref[pl.ds(SUBLANE - 1, tm), cs], fx_ref[pl.ds(SUBLANE - 2, tm), cs])
        fc = cw_ref[2:3, cs] * fa + cw_ref[1:2, cs] * prev1 + cw_ref[0:1, cs] * prev2 + cb_ref[:, cs]
        ps.append((jax.nn.gelu(fc) * fg).astype(BF16))
    return jnp.concatenate(ps, axis=1)


def _ffn_prompt_kernel(h_ref, x1_hbm, wa_ref, wg_ref, wd_ref, cw_ref, cb_ref, g_ref, y_ref, tail_ref,
                       fx_ref, carry_ref, x1_ref, x1_sem, *, nt):
    tm = h_ref.shape[0]
    tf = wa_ref.shape[1]
    i = pl.program_id(0)
    j = pl.program_id(1)
    x1_copy = pltpu.make_async_copy(x1_hbm.at[pl.ds(i * tm, tm), :], x1_ref, x1_sem)

    @pl.when(j == 0)
    def _():
        x1_copy.start()
        y_ref[...] = jnp.zeros(y_ref.shape, F32)

    @pl.when(i % nt == 0)
    def _():
        fx_ref[0:SUBLANE, :] = jnp.zeros((SUBLANE, tf), F32)

    @pl.when(i % nt != 0)
    def _():
        fx_ref[0:SUBLANE, :] = carry_ref[j]

    p = _ffn_hidden(h_ref[...], wa_ref, wg_ref, cw_ref, cb_ref, fx_ref, lambda cs, p1, p2: (p1, p2))
    last = fx_ref[tm:tm + SUBLANE, :]
    carry_ref[j] = last
    tail_ref[0] = last
    y_ref[...] += _dot(p, wd_ref[...])

    @pl.when(j == pl.num_programs(1) - 1)
    def _():
        x1_copy.wait()

        def norm_rows(r, carry):
            rs = pl.ds(pl.multiple_of(r * CHUNK, CHUNK), CHUNK)
            y_ref[rs, :] = _rms(x1_ref[rs, :] + y_ref[rs, :], g_ref[...])
            return carry

        lax.fori_loop(0, tm // CHUNK, norm_rows, 0)


def _ffn_prompt_call(h2, x1, wa, wg, wd, cw, cb, g_final, nb):
    m = h2.shape[0]
    tm = FFN_TM
    nt = m // nb // tm
    nj = D_FF_PAD // TF
    row = lambda i, j: (i, 0)
    col = lambda i, j: (0, j)
    return pl.pallas_call(
        functools.partial(_ffn_prompt_kernel, nt=nt),
        grid=(m // tm, nj),
        in_specs=[pl.BlockSpec((tm, D_MODEL), row),
                  pl.BlockSpec(memory_space=pl.ANY),
                  pl.BlockSpec((D_MODEL, TF), col),
                  pl.BlockSpec((D_MODEL, TF), col),
                  pl.BlockSpec((TF, D_MODEL), lambda i, j: (j, 0)),
                  pl.BlockSpec((CONV_F, TF), col),
                  pl.BlockSpec((1, TF), col),
                  pl.BlockSpec((1, D_MODEL), lambda i, j: (0, 0))],
        out_specs=[pl.BlockSpec((tm, D_MODEL), row),
                   pl.BlockSpec((1, SUBLANE, TF), lambda i, j: (i, 0, j))],
        out_shape=[jax.ShapeDtypeStruct((m, D_MODEL), F32),
                   jax.ShapeDtypeStruct((m // tm, SUBLANE, D_FF_PAD), F32)],
        scratch_shapes=[pltpu.VMEM((tm + SUBLANE, TF), F32),
                        pltpu.VMEM((nj, SUBLANE, TF), F32),
                        pltpu.VMEM((tm, D_MODEL), F32),
                        pltpu.SemaphoreType.DMA(())],
        compiler_params=_params("arbitrary", "arbitrary"),
        name="ffn_prompt",
    )(h2, x1, wa, wg, wd, cw, cb, g_final.reshape(1, D_MODEL))


def _ffn_sample_kernel(h_ref, x1_ref, wa_ref, wg_ref, wd_ref, cw_ref, cb_ref, g_ref, hist_ref,
                       y_ref, st_ref, fx_ref, h1_ref, h2_ref, fa3_ref):
    nb, nl, tf = fa3_ref.shape
    j = pl.program_id(1)
    live = (j * tf + lax.broadcasted_iota(jnp.int32, (nb, tf), 1)) < D_FF
    older = jnp.where(live, hist_ref[:, 0, :], 0.0)
    newer = jnp.where(live, hist_ref[:, 1, :], 0.0)
    h1_ref[...] = jnp.zeros_like(h1_ref)
    h2_ref[...] = jnp.zeros_like(h2_ref)
    h1_ref[:, 0, :] = newer
    h2_ref[:, 0, :] = older
    h2_ref[:, 1, :] = newer
    fx_ref[0:SUBLANE, :] = jnp.zeros((SUBLANE, tf), F32)
    t = lax.broadcasted_iota(jnp.int32, (nb * nl, FFN_SUB), 0) % nl

    def history(cs, prev1, prev2):
        h1 = h1_ref[:, :, cs].reshape(nb * nl, FFN_SUB)
        h2 = h2_ref[:, :, cs].reshape(nb * nl, FFN_SUB)
        return jnp.where(t >= 1, prev1, h1), jnp.where(t >= 2, prev2, h2)

    p = _ffn_hidden(h_ref[...], wa_ref, wg_ref, cw_ref, cb_ref, fx_ref, history)
    fa3_ref[...] = fx_ref[SUBLANE:, :].reshape(nb, nl, tf)
    for r in range(CONV_F - 1):
        st_ref[:, r, :] = fa3_ref[:, nl - (CONV_F - 1) + r, :]

    @pl.when(j == 0)
    def _():
        y_ref[...] = x1_ref[...]

    y_ref[...] += _dot(p, wd_ref[...])

    @pl.when(j == pl.num_programs(1) - 1)
    def _():
        y_ref[...] = _rms(y_ref[...], g_ref[...])


def _ffn_sample_call(h2, x1, wa, wg, wd, cw, cb, g_final, state, l, nb):
    m = h2.shape[0]
    nl = m // nb
    nj = D_FF_PAD // TF
    row = lambda i, j: (i, 0)
    col = lambda i, j: (0, j)
    return pl.pallas_call(
        _ffn_sample_kernel,
        grid=(1, nj),
        in_specs=[pl.BlockSpec((m, D_MODEL), row, pipeline_mode=pl.Buffered(1)),
                  pl.BlockSpec((m, D_MODEL), row, pipeline_mode=pl.Buffered(1)),
                  pl.BlockSpec((D_MODEL, TF), col),
                  pl.BlockSpec((D_MODEL, TF), col),
                  pl.BlockSpec((TF, D_MODEL), lambda i, j: (j, 0)),
                  pl.BlockSpec((CONV_F, TF), col),
                  pl.BlockSpec((1, TF), col),
                  pl.BlockSpec((1, D_MODEL), lambda i, j: (0, 0)),
                  pl.BlockSpec((None, nb, CONV_F - 1, TF), lambda i, j: (l, 0, 0, j))],
        out_specs=[pl.BlockSpec((m, D_MODEL), row),
                   pl.BlockSpec((None, nb, CONV_F - 1, TF), lambda i, j: (0, 0, 0, j))],
        out_shape=[jax.ShapeDtypeStruct((m, D_MODEL), F32),
                   jax.ShapeDtypeStruct((1, nb, CONV_F - 1, D_FF), F32)],
        scratch_shapes=[pltpu.VMEM((m + SUBLANE, TF), F32),
                        pltpu.VMEM((nb, nl, TF), F32),
                        pltpu.VMEM((nb, nl, TF), F32),
                        pltpu.VMEM((nb, nl, TF), F32)],
        compiler_params=_params("arbitrary", "arbitrary"),
        name="ffn_sample",
    )(h2, x1, wa, wg, wd, cw, cb, g_final.reshape(1, D_MODEL), state)


def kernel(x_prompt, x_sample, mem_prompt, cache_mem_k, cache_mem_v, state_conv, state_ffn_conv, g_mix, w_in, ln_v_g, ln_v_b, w_s, b_s, w_pa, conv_w, conv_b, ln_b_g, ln_b_b, w_pb, g_mem, w_k, w_v, w_pc, w_o, g_ffn, w_up, ffn_conv_w, ffn_conv_b, w_down, g_final):
    depth = g_mix.shape[0]
    assert depth == 1
    l = 0
    nbp, lp, _ = x_prompt.shape
    nbs, ls, _ = x_sample.shape
    mp = nbp * lp
    pad_ff = D_FF_PAD - D_FF

    w_in_b = w_in[l].astype(BF16)
    w_k_b = w_k[l].astype(BF16)
    w_v_b = w_v[l].astype(BF16)
    cw = jnp.pad(ffn_conv_w[l], ((0, 0), (0, pad_ff)))
    cb = jnp.pad(ffn_conv_b[l], (0, pad_ff)).reshape(1, D_FF_PAD)

    reps = CHUNK // ls
    ws_p = w_s[l]
    ws_s = jnp.tile(w_s[l][:, :ls, :ls], (1, reps, reps))
    bs_p = jnp.repeat(b_s[l].T, D_A // G_A, axis=1)
    bs_s = jnp.repeat(jnp.tile(b_s[l][:, :ls], (1, reps)).T, D_A // G_A, axis=1)

    xp = x_prompt.reshape(mp, D_MODEL)
    h, a, bact, q, ctail, w_pa_b, w_pb_b, w_pc_b, w_o_b = _inproj_prompt_call(
        xp, nbp, g_mix[l], w_in_b, ln_v_g[l], ln_v_b[l], ws_p, bs_p, conv_w[l], conv_b[l], ln_b_g[l], ln_b_b[l],
        (w_pa[l], w_pb[l], w_pc[l], w_o[l]))
    k5, v5, kb, vb = _memkv_call(mem_prompt, g_mem[l], w_k_b, w_v_b)
    cact = _attn_prompt_call(q, kb, vb)
    mix, w_up_a, w_up_g, w_down_b = _mix_call(h, a, bact, cact, w_pa_b, w_pb_b, w_pc_b, w_in_b, "mix_prompt",
                                              (w_up[l], w_down[l]))
    x1, h2 = _oproj_call(xp, mix, w_o_b, g_ffn[l], "oproj_prompt")
    yp, tail = _ffn_prompt_call(h2, x1, w_up_a, w_up_g, w_down_b, cw, cb, g_final, nbp)
    nt = tail.shape[0] // nbp
    ffn_p = tail.reshape(nbp, nt, SUBLANE, D_FF_PAD)[:, nt - 1, SUBLANE - (CONV_F - 1):, :D_FF]
    nt = ctail.shape[0] // nbp
    conv_p = ctail.reshape(nbp, nt, HALO, D_B)[:, nt - 1, HALO - (CONV_B - 1):]

    h, a, q, glu_s, vn_s = _inproj_sample_call(x_sample, g_mix[l], w_in_b, ln_v_g[l], ln_v_b[l], ws_s, bs_s)
    bact, conv_s = _convb_sample_call(jnp.swapaxes(state_conv, 1, 2), l, glu_s, conv_w[l], conv_b[l],
                                      ln_b_g[l], ln_b_b[l])
    conv_s = jnp.swapaxes(conv_s, 1, 2)
    cact = _attn_sample_call(q, cache_mem_k, cache_mem_v, l, ls)
    (mix,) = _mix_call(h, a, bact, cact, w_pa_b, w_pb_b, w_pc_b, w_in_b, "mix_sample")
    x1, h2 = _oproj_call(x_sample, mix, w_o_b, g_ffn[l], "oproj_sample")
    ys, ffn_s = _ffn_sample_call(h2, x1, w_up_a, w_up_g, w_down_b, cw, cb, g_final, state_ffn_conv, l, nbs)

    return (yp.reshape(nbp, lp, D_MODEL), ys.reshape(nbs, ls, D_MODEL), k5, v5, conv_p[None], ffn_p[None],
            conv_s, ffn_s, vn_s)
```

```python
import functools

import jax
import jax.numpy as jnp
from jax import lax
from jax.experimental import pallas as pl
from jax.experimental.pallas import tpu as pltpu

F32 = jnp.float32
BF16 = jnp.bfloat16

D_MODEL = 2048
CHUNK = 128
D_A = D_MODEL // 2
G_A = 8
D_B = D_MODEL // 2
CONV_B = 31
N_MEM = 256
N_XHEADS = 4
XHEAD_DIM = D_MODEL // 8
D_C = N_XHEADS * XHEAD_DIM
D_FF = ((8 * D_MODEL // 3 + 127) // 128) * 128
CONV_F = 3
EPS = 1e-6

OFF_ZA = 0
OFF_ZB = 2 * D_A
OFF_Q = OFF_ZB + 2 * D_B
OFF_GATES = OFF_Q + D_C

LANE = 128
SUBLANE = 8
TM = 512
TF = 512
D_FF_PAD = ((D_FF + TF - 1) // TF) * TF
HALO = 32
VMEM_LIMIT = 60 * 1024 * 1024


def _params(*sem):
    return pltpu.CompilerParams(dimension_semantics=sem, vmem_limit_bytes=VMEM_LIMIT)


def _rms(x, g):
    return x * lax.rsqrt(jnp.mean(x * x, axis=-1, keepdims=True) + EPS) * g


def _layer_norm(x, g, b):
    mu = jnp.mean(x, axis=-1, keepdims=True)
    d = x - mu
    var = jnp.mean(d * d, axis=-1, keepdims=True)
    return d * lax.rsqrt(var + EPS) * g + b


def _dot(a, b):
    return jnp.dot(a, b, preferred_element_type=F32)


def _softmax(s):
    e = jnp.exp(s - jnp.max(s, axis=-1, keepdims=True))
    return e / jnp.sum(e, axis=-1, keepdims=True)


def _spatial_gate(u, vb, ws_ref, bs_ref, a_ref, sample):
    tm = vb.shape[0]
    r = lax.broadcasted_iota(jnp.int32, (CHUNK, CHUNK), 0)
    c = lax.broadcasted_iota(jnp.int32, (CHUNK, CHUNK), 1)
    mask = r >= c
    if sample:
        mask = mask & ((r >> 2) == (c >> 2))
    for g in range(G_A):
        cs = slice(g * LANE, (g + 1) * LANE)
        wm = jnp.where(mask, ws_ref[g], 0.0).astype(BF16)
        for ch in range(tm // CHUNK):
            rs = slice(ch * CHUNK, (ch + 1) * CHUNK)
            s = _dot(wm, vb[rs, cs]) + bs_ref[:, cs]
            a_ref[rs, cs] = (u[rs, cs] * s).astype(BF16)


IN_SUB = 256
CONV_ROWS = 128


def _conv_taps(xc_ref, sh_ref, w_ref, bias, r0, cs):
    lead = HALO - (CONV_B - 1)
    acc = jnp.broadcast_to(bias, (CONV_ROWS, bias.shape[1]))
    for s in range(SUBLANE):
        qs = [q for q in range(HALO // SUBLANE + 1) if lead <= SUBLANE * q + s < lead + CONV_B]
        n = CONV_ROWS + SUBLANE * qs[-1]
        if s:
            sh_ref[0:n, :] = xc_ref[pl.ds(r0 + s, n), cs]
        for q in qs:
            k = SUBLANE * q + s - lead
            if s:
                win = sh_ref[SUBLANE * q:SUBLANE * q + CONV_ROWS, :]
            else:
                win = xc_ref[r0 + SUBLANE * q:r0 + SUBLANE * q + CONV_ROWS, cs]
            acc = acc + w_ref[k:k + 1, cs] * win
    return acc


def _inproj_prompt_kernel(x_ref, gm_ref, wa_ref, wb_ref, wq_ref, lvg_ref, lvb_ref, ws_ref, bs_ref,
                          cw_ref, cb_ref, lbg_ref, lbb_ref, wpa_f, wpb_f, wpc_f, wo_f,
                          h_ref, a_ref, b_ref, q_ref, ctail_ref, wpa_o, wpb_o, wpc_o, wo_o,
                          xc_ref, cv_ref, sh_ref, u_ref, v_ref, *, nt):
    tm = x_ref.shape[0]
    t = pl.program_id(0) % nt
    for src, dst in ((wpa_f, wpa_o), (wpb_f, wpb_o), (wpc_f, wpc_o), (wo_f, wo_o)):
        dst[...] = src[...].astype(BF16)

    @pl.when(t == 0)
    def _():
        xc_ref[0:HALO, :] = jnp.zeros((HALO, D_B), F32)

    @pl.when(t != 0)
    def _():
        xc_ref[0:HALO, :] = xc_ref[tm:tm + HALO, :]

    h = _rms(x_ref[...], gm_ref[...]).astype(BF16)
    h_ref[...] = h

    for c in range(D_B // IN_SUB):
        cs = slice(c * IN_SUB, (c + 1) * IN_SUB)
        za = _dot(h, wb_ref[:, cs])
        zb = _dot(h, wb_ref[:, D_B + c * IN_SUB:D_B + (c + 1) * IN_SUB])
        xc_ref[HALO:, cs] = za * jax.nn.sigmoid(zb)
        for rb in range(tm // CONV_ROWS):
            r0 = rb * CONV_ROWS
            cv_ref[r0:r0 + CONV_ROWS, cs] = _conv_taps(xc_ref, sh_ref, cw_ref, cb_ref[:, cs], r0, cs)
    ctail_ref[0] = xc_ref[tm:tm + HALO, :]
    y = _layer_norm(cv_ref[...], lbg_ref[...], lbb_ref[...])
    b_ref[...] = (y * jax.nn.sigmoid(y)).astype(BF16)

    for c in range(D_A // IN_SUB):
        cs = slice(c * IN_SUB, (c + 1) * IN_SUB)
        u_ref[:, cs] = jax.nn.gelu(_dot(h, wa_ref[:, cs]))
        v_ref[:, cs] = jax.nn.gelu(_dot(h, wa_ref[:, D_A + c * IN_SUB:D_A + (c + 1) * IN_SUB]))
    q_ref[...] = _dot(h, wq_ref[...]).astype(BF16)
    vb = _layer_norm(v_ref[...], lvg_ref[...], lvb_ref[...]).astype(BF16)
    _spatial_gate(u_ref, vb, ws_ref, bs_ref, a_ref, False)


def _inproj_weight_specs(const_map):
    single = pl.Buffered(1)
    return [pl.BlockSpec((D_MODEL, 2 * D_A), lambda *i: (0, OFF_ZA // (2 * D_A)), pipeline_mode=single),
            pl.BlockSpec((D_MODEL, 2 * D_B), lambda *i: (0, OFF_ZB // (2 * D_B)), pipeline_mode=single),
            pl.BlockSpec((D_MODEL, D_C), lambda *i: (0, OFF_Q // D_C), pipeline_mode=single)]


def _inproj_prompt_call(x, nb, g_mix, w_in, ln_v_g, ln_v_b, ws, bs, conv_w, conv_b, ln_b_g, ln_b_b, casts):
    m = x.shape[0]
    tm = TM
    nt = m // nb // tm
    steps = m // tm
    slab = lambda w: pl.BlockSpec((w.shape[0] // steps, w.shape[1]), lambda i: (i, 0))
    const = lambda i: (0, 0)
    row = lambda i: (i, 0)
    vec = lambda n: pl.BlockSpec((1, n), const)
    return pl.pallas_call(
        functools.partial(_inproj_prompt_kernel, nt=nt),
        grid=(m // tm,),
        in_specs=[pl.BlockSpec((tm, D_MODEL), row), vec(D_MODEL)] + _inproj_weight_specs(const) + [
                  vec(D_A), vec(D_A),
                  pl.BlockSpec((G_A, CHUNK, CHUNK), lambda i: (0, 0, 0)),
                  pl.BlockSpec((CHUNK, D_A), const),
                  pl.BlockSpec((CONV_B, D_B), const),
                  vec(D_B), vec(D_B), vec(D_B)] + [slab(w) for w in casts],
        out_specs=[pl.BlockSpec((tm, D_MODEL), row),
                   pl.BlockSpec((tm, D_A), row),
                   pl.BlockSpec((tm, D_B), row),
                   pl.BlockSpec((tm, D_C), row),
                   pl.BlockSpec((1, HALO, D_B), lambda i: (i, 0, 0))] + [slab(w) for w in casts],
        out_shape=[jax.ShapeDtypeStruct((m, D_MODEL), BF16),
                   jax.ShapeDtypeStruct((m, D_A), BF16),
                   jax.ShapeDtypeStruct((m, D_B), BF16),
                   jax.ShapeDtypeStruct((m, D_C), BF16),
                   jax.ShapeDtypeStruct((m // tm, HALO, D_B), F32)]
                  + [jax.ShapeDtypeStruct(w.shape, BF16) for w in casts],
        scratch_shapes=[pltpu.VMEM((tm + HALO, D_B), F32),
                        pltpu.VMEM((tm, D_B), F32),
                        pltpu.VMEM((CONV_ROWS + HALO, IN_SUB), F32),
                        pltpu.VMEM((tm, D_A), F32),
                        pltpu.VMEM((tm, D_A), F32)],
        compiler_params=_params("arbitrary"),
        name="inproj_prompt",
    )(x, g_mix.reshape(1, D_MODEL), w_in, w_in, w_in, ln_v_g.reshape(1, D_A), ln_v_b.reshape(1, D_A), ws, bs,
      conv_w, conv_b.reshape(1, D_B), ln_b_g.reshape(1, D_B), ln_b_b.reshape(1, D_B), *casts)


def _inproj_sample_kernel(x_ref, gm_ref, wa_ref, wb_ref, wq_ref, lvg_ref, lvb_ref, ws_ref, bs_ref,
                          h_ref, a_ref, q_ref, glu_ref, vn_ref):
    nb, nl, _ = x_ref.shape
    m = nb * nl
    h = _rms(x_ref[...].reshape(m, D_MODEL), gm_ref[...]).astype(BF16)
    h_ref[...] = h
    zg = jax.nn.gelu(_dot(h, wa_ref[...]))
    vn = _layer_norm(zg[:, D_A:], lvg_ref[...], lvb_ref[...])
    vn_ref[...] = vn.reshape(nb, nl, D_A)
    _spatial_gate(zg[:, :D_A], vn.astype(BF16), ws_ref, bs_ref, a_ref, True)
    z = _dot(h, wb_ref[...])
    glu_ref[...] = (z[:, :D_B] * jax.nn.sigmoid(z[:, D_B:])).reshape(nb, nl, D_B)
    q_ref[...] = _dot(h, wq_ref[...])


def _inproj_sample_call(x, g_mix, w_in, ln_v_g, ln_v_b, ws, bs):
    nb, nl, _ = x.shape
    m = nb * nl
    bt = CHUNK // nl
    const = lambda i: (0, 0)
    vec = lambda n: pl.BlockSpec((1, n), const)
    flat = lambda n: pl.BlockSpec((bt * nl, n), lambda i: (i, 0))
    return pl.pallas_call(
        _inproj_sample_kernel,
        grid=(nb // bt,),
        in_specs=[pl.BlockSpec((bt, nl, D_MODEL), lambda i: (i, 0, 0)), vec(D_MODEL)]
                 + _inproj_weight_specs(const) + [
                  vec(D_A), vec(D_A),
                  pl.BlockSpec((G_A, CHUNK, CHUNK), lambda i: (0, 0, 0)),
                  pl.BlockSpec((CHUNK, D_A), const)],
        out_specs=[flat(D_MODEL), flat(D_A), flat(D_C),
                   pl.BlockSpec((bt, nl, D_B), lambda i: (i, 0, 0)),
                   pl.BlockSpec((None, bt, nl, D_A), lambda i: (0, i, 0, 0))],
        out_shape=[jax.ShapeDtypeStruct((m, D_MODEL), BF16),
                   jax.ShapeDtypeStruct((m, D_A), BF16),
                   jax.ShapeDtypeStruct((m, D_C), F32),
                   jax.ShapeDtypeStruct((nb, nl, D_B), F32),
                   jax.ShapeDtypeStruct((1, nb, nl, D_A), F32)],
        compiler_params=_params("arbitrary"),
        name="inproj_sample",
    )(x, g_mix.reshape(1, D_MODEL), w_in, w_in, w_in, ln_v_g.reshape(1, D_A), ln_v_b.reshape(1, D_A), ws, bs)


def _convb_sample_kernel(hist_ref, x_ref, w_ref, cb_ref, lng_ref, lnb_ref, o_ref, nh_ref):
    bt, nl, _ = x_ref.shape
    nh = hist_ref.shape[0]
    new_rows = [x_ref[:, t, :] for t in range(nl)]
    window = lambda j: hist_ref[j] if j < nh else new_rows[j - nh]
    for t in range(nl):
        acc = jnp.broadcast_to(cb_ref[...], (bt, D_B))
        for k in range(CONV_B):
            acc = acc + w_ref[k:k + 1, :] * window(t + k)
        y = _layer_norm(acc, lng_ref[...], lnb_ref[...])
        o_ref[:, t, :] = y * jax.nn.sigmoid(y)
    for j in range(nh):
        nh_ref[j] = window(j + nl)


def _convb_sample_call(hist, l, x, conv_w, conv_b, ln_g, ln_b):
    nb, nl, _ = x.shape
    nh = hist.shape[1]
    bt = 16
    const = lambda i: (0, 0)
    state_spec = pl.BlockSpec((None, nh, bt, D_B), lambda i: (l, 0, i, 0))
    new_spec = pl.BlockSpec((bt, nl, D_B), lambda i: (i, 0, 0))
    return pl.pallas_call(
        _convb_sample_kernel,
        grid=(nb // bt,),
        in_specs=[state_spec, new_spec,
                  pl.BlockSpec((CONV_B, D_B), const),
                  pl.BlockSpec((1, D_B), const),
                  pl.BlockSpec((1, D_B), const),
                  pl.BlockSpec((1, D_B), const)],
        out_specs=[new_spec, pl.BlockSpec((None, nh, bt, D_B), lambda i: (0, 0, i, 0))],
        out_shape=[jax.ShapeDtypeStruct((nb, nl, D_B), F32),
                   jax.ShapeDtypeStruct((1, nh, nb, D_B), F32)],
        compiler_params=_params("parallel"),
        name="convb_sample",
    )(hist, x, conv_w, conv_b.reshape(1, D_B), ln_g.reshape(1, D_B), ln_b.reshape(1, D_B))


def _memkv_kernel(m_ref, g_ref, wk_ref, wv_ref, k5_ref, v5_ref, kb_ref, vb_ref):
    mn = _rms(m_ref[...], g_ref[...]).astype(BF16)
    k = _dot(mn, wk_ref[...])
    v = _dot(mn, wv_ref[...])
    k5_ref[...] = k.reshape(N_MEM, N_XHEADS, XHEAD_DIM)
    v5_ref[...] = v.reshape(N_MEM, N_XHEADS, XHEAD_DIM)
    kb_ref[...] = k.astype(BF16)
    vb_ref[...] = v.astype(BF16)


def _memkv_call(mem, g_mem, w_k, w_v):
    nb = mem.shape[0]
    const = lambda i: (0, 0)
    cache_spec = pl.BlockSpec((None, None, N_MEM, N_XHEADS, XHEAD_DIM), lambda i: (0, i, 0, 0, 0))
    seq_spec = pl.BlockSpec((None, N_MEM, D_C), lambda i: (i, 0, 0))
    cache_shape = jax.ShapeDtypeStruct((1, nb, N_MEM, N_XHEADS, XHEAD_DIM), F32)
    return pl.pallas_call(
        _memkv_kernel,
        grid=(nb,),
        in_specs=[pl.BlockSpec((None, N_MEM, D_MODEL), lambda i: (i, 0, 0)),
                  pl.BlockSpec((1, D_MODEL), const),
                  pl.BlockSpec((D_MODEL, D_C), const),
                  pl.BlockSpec((D_MODEL, D_C), const)],
        out_specs=[cache_spec, cache_spec, seq_spec, seq_spec],
        out_shape=[cache_shape, cache_shape,
                   jax.ShapeDtypeStruct((nb, N_MEM, D_C), BF16),
                   jax.ShapeDtypeStruct((nb, N_MEM, D_C), BF16)],
        compiler_params=_params("parallel"),
        name="memory_kv",
    )(mem, g_mem.reshape(1, D_MODEL), w_k, w_v)


GATE_COLS = 1024


def _attn_prompt_kernel(q_ref, k_ref, v_ref, *refs):
    gate_refs, o_ref, gates_o = refs[:-2], refs[-2], refs[-1]
    for n, g_ref in enumerate(gate_refs):
        gates_o[:, n * GATE_COLS:(n + 1) * GATE_COLS] = g_ref[...].astype(BF16)
    for h in range(N_XHEADS):
        cs = slice(h * XHEAD_DIM, (h + 1) * XHEAD_DIM)
        s = lax.dot_general(q_ref[:, cs], k_ref[:, cs], (((1,), (1,)), ((), ())), preferred_element_type=F32)
        p = _softmax(s * (XHEAD_DIM ** -0.5)).astype(BF16)
        o_ref[:, cs] = _dot(p, v_ref[:, cs]).astype(BF16)


def _attn_prompt_call(q, k, v, w_in):
    m = q.shape[0]
    nb = k.shape[0]
    tm = TM
    nt = m // nb // tm
    n_gate = 3 * D_MODEL
    slab = D_MODEL // (nb * nt)
    assert OFF_GATES % GATE_COLS == 0 and n_gate % GATE_COLS == 0 and D_MODEL % (nb * nt) == 0
    gate_specs = [pl.BlockSpec((slab, GATE_COLS), lambda b, t, c=c: (b * nt + t, OFF_GATES // GATE_COLS + c))
                  for c in range(n_gate // GATE_COLS)]
    return pl.pallas_call(
        _attn_prompt_kernel,
        grid=(nb, nt),
        in_specs=[pl.BlockSpec((tm, D_C), lambda b, t: (b * nt + t, 0)),
                  pl.BlockSpec((None, N_MEM, D_C), lambda b, t: (b, 0, 0)),
                  pl.BlockSpec((None, N_MEM, D_C), lambda b, t: (b, 0, 0))] + gate_specs,
        out_specs=[pl.BlockSpec((tm, D_C), lambda b, t: (b * nt + t, 0)),
                   pl.BlockSpec((slab, n_gate), lambda b, t: (b * nt + t, 0))],
        out_shape=[jax.ShapeDtypeStruct((m, D_C), BF16),
                   jax.ShapeDtypeStruct((D_MODEL, n_gate), BF16)],
        compiler_params=_params("arbitrary", "arbitrary"),
        name="attn_prompt",
    )(q, k, v, *([w_in] * len(gate_specs)))


def _attn_sample_kernel(q_ref, k_ref, v_ref, o_ref):
    bt = k_ref.shape[0]
    nl = q_ref.shape[0] // bt
    rows = lax.broadcasted_iota(jnp.int32, (N_XHEADS * nl, N_MEM * N_XHEADS), 0)
    cols = lax.broadcasted_iota(jnp.int32, (N_XHEADS * nl, N_MEM * N_XHEADS), 1)
    same_head = (cols % N_XHEADS) == (rows // nl)
    outs = []
    for b in range(bt):
        q = q_ref[b * nl:(b + 1) * nl, :]
        qs = jnp.concatenate([q[:, h * XHEAD_DIM:(h + 1) * XHEAD_DIM] for h in range(N_XHEADS)], axis=0)
        k2 = k_ref[b].reshape(N_MEM * N_XHEADS, XHEAD_DIM).astype(BF16)
        v2 = v_ref[b].reshape(N_MEM * N_XHEADS, XHEAD_DIM).astype(BF16)
        s = lax.dot_general(qs.astype(BF16), k2, (((1,), (1,)), ((), ())), preferred_element_type=F32)
        p = _softmax(jnp.where(same_head, s * (XHEAD_DIM ** -0.5), -jnp.inf)).astype(BF16)
        o = _dot(p, v2)
        outs.append(jnp.concatenate([o[h * nl:(h + 1) * nl, :] for h in range(N_XHEADS)], axis=1))
    o_ref[...] = jnp.concatenate(outs, axis=0).astype(BF16)


def _attn_sample_call(q, k, v, l, nl):
    m = q.shape[0]
    bt = 8
    cache_spec = pl.BlockSpec((None, bt, N_MEM, N_XHEADS, XHEAD_DIM), lambda i: (l, i, 0, 0, 0))
    return pl.pallas_call(
        _attn_sample_kernel,
        grid=(m // (bt * nl),),
        in_specs=[pl.BlockSpec((bt * nl, D_C), lambda i: (i, 0)), cache_spec, cache_spec],
        out_specs=pl.BlockSpec((bt * nl, D_C), lambda i: (i, 0)),
        out_shape=jax.ShapeDtypeStruct((m, D_C), BF16),
        compiler_params=_params("parallel"),
        name="attn_sample",
    )(q, k, v)


def _cast_ffn_weights(wu_f, wd_f, wa_o, wg_o, wd_o, step):
    wu = wu_f[...]
    zpad = jnp.zeros((wu.shape[0], D_FF_PAD - D_FF), BF16)
    wa_o[:, :D_FF] = wu[:, :D_FF].astype(BF16)
    wa_o[:, D_FF:] = zpad
    wg_o[:, :D_FF] = wu[:, D_FF:].astype(BF16)
    wg_o[:, D_FF:] = zpad
    live_blocks = D_FF // wd_f.shape[0]

    @pl.when(step < live_blocks)
    def _():
        wd_o[...] = wd_f[...].astype(BF16)

    @pl.when(step >= live_blocks)
    def _():
        wd_o[...] = jnp.zeros(wd_o.shape, BF16)


def _mix_kernel(*refs, cast_ffn):
    h_ref, a_ref, b_ref, c_ref, wpa_ref, wpb_ref, wpc_ref, wga_ref, wgb_ref, wgc_ref = refs[:10]
    if cast_ffn:
        wu_f, wd_f, o_ref, wa_o, wg_o, wd_o = refs[10:]
        _cast_ffn_weights(wu_f, wd_f, wa_o, wg_o, wd_o, pl.program_id(0) * pl.num_programs(1) + pl.program_id(1))
    else:
        (o_ref,) = refs[10:]
    h = h_ref[...]
    b = b_ref[...].reshape(h.shape[0], D_B).astype(BF16)
    mix = jax.nn.sigmoid(_dot(h, wga_ref[...])) * _dot(a_ref[...], wpa_ref[...])
    mix = mix + jax.nn.sigmoid(_dot(h, wgb_ref[...])) * _dot(b, wpb_ref[...])
    mix = mix + jax.nn.sigmoid(_dot(h, wgc_ref[...])) * _dot(c_ref[...], wpc_ref[...])
    o_ref[...] = mix.astype(BF16)


def _mix_call(h, a, b, c, w_pa, w_pb, w_pc, w_gates, name, ffn_weights=None):
    m = h.shape[0]
    tm = min(TM, m)
    tn = 512
    nj = D_MODEL // tn
    per_gate = D_MODEL // tn
    row = lambda i, j: (i, 0)
    col = lambda i, j: (0, j)
    if b.ndim == 3:
        assert m == tm
        b_spec = pl.BlockSpec(b.shape, lambda i, j: (0, 0, 0))
    else:
        b_spec = pl.BlockSpec((tm, D_B), row)
    in_specs = [pl.BlockSpec((tm, D_MODEL), row),
                pl.BlockSpec((tm, D_A), row),
                b_spec,
                pl.BlockSpec((tm, D_C), row),
                pl.BlockSpec((D_A, tn), col),
                pl.BlockSpec((D_B, tn), col),
                pl.BlockSpec((D_C, tn), col),
                pl.BlockSpec((D_MODEL, tn), lambda i, j: (0, j)),
                pl.BlockSpec((D_MODEL, tn), lambda i, j: (0, per_gate + j)),
                pl.BlockSpec((D_MODEL, tn), lambda i, j: (0, 2 * per_gate + j))]
    out_specs = [pl.BlockSpec((tm, tn), lambda i, j: (i, j))]
    out_shape = [jax.ShapeDtypeStruct((m, D_MODEL), BF16)]
    args = [h, a, b, c, w_pa, w_pb, w_pc, w_gates, w_gates, w_gates]
    if ffn_weights is not None:
        w_up, w_down = ffn_weights
        steps = (m // tm) * nj
        up_rows = D_MODEL // steps
        down_rows = CHUNK
        down_blocks = D_FF_PAD // down_rows
        assert D_MODEL % steps == 0 and D_FF % down_rows == 0 and down_blocks <= steps
        step = lambda i, j: i * nj + j
        in_specs += [pl.BlockSpec((up_rows, 2 * D_FF), lambda i, j: (step(i, j), 0)),
                     pl.BlockSpec((down_rows, D_MODEL),
                                  lambda i, j: (jnp.minimum(step(i, j), D_FF // down_rows - 1), 0))]
        out_specs += [pl.BlockSpec((up_rows, D_FF_PAD), lambda i, j: (step(i, j), 0)),
                      pl.BlockSpec((up_rows, D_FF_PAD), lambda i, j: (step(i, j), 0)),
                      pl.BlockSpec((down_rows, D_MODEL),
                                   lambda i, j: (jnp.minimum(step(i, j), down_blocks - 1), 0))]
        out_shape += [jax.ShapeDtypeStruct((D_MODEL, D_FF_PAD), BF16),
                      jax.ShapeDtypeStruct((D_MODEL, D_FF_PAD), BF16),
                      jax.ShapeDtypeStruct((D_FF_PAD, D_MODEL), BF16)]
        args += [w_up, w_down]
    return pl.pallas_call(
        functools.partial(_mix_kernel, cast_ffn=ffn_weights is not None),
        grid=(m // tm, nj),
        in_specs=in_specs,
        out_specs=out_specs,
        out_shape=out_shape,
        compiler_params=_params("arbitrary", "arbitrary"),
        name=name,
    )(*args)


def _oproj_kernel(x_ref, mix_ref, wo_ref, g_ref, x1_ref, h2_ref):
    x1 = x_ref[...].reshape(mix_ref.shape) + _dot(mix_ref[...], wo_ref[...])
    x1_ref[...] = x1
    h2_ref[...] = _rms(x1, g_ref[...]).astype(BF16)


def _oproj_call(x, mix, w_o, g_ffn, name):
    m = mix.shape[0]
    tm = min(TM, m)
    row = lambda i: (i, 0)
    if x.ndim == 3:
        assert m == tm
        x_spec = pl.BlockSpec(x.shape, lambda i: (0, 0, 0))
    else:
        x_spec = pl.BlockSpec((tm, D_MODEL), row)
    return pl.pallas_call(
        _oproj_kernel,
        grid=(m // tm,),
        in_specs=[x_spec,
                  pl.BlockSpec((tm, D_MODEL), row),
                  pl.BlockSpec((D_MODEL, D_MODEL), lambda i: (0, 0)),
                  pl.BlockSpec((1, D_MODEL), lambda i: (0, 0))],
        out_specs=[pl.BlockSpec((tm, D_MODEL), row),
                   pl.BlockSpec((tm, D_MODEL), row)],
        out_shape=[jax.ShapeDtypeStruct((m, D_MODEL), F32),
                   jax.ShapeDtypeStruct((m, D_MODEL), BF16)],
        compiler_params=_params("parallel"),
        name=name,
    )(x, mix, w_o, g_ffn.reshape(1, D_MODEL))


FFN_TM = 1024
FFN_SUB = 256


def _ffn_hidden(h, wa_ref, wg_ref, cw_ref, cb_ref, fx_ref, history):
    tm = h.shape[0]
    ps = []
    for c in range(wa_ref.shape[1] // FFN_SUB):
        cs = slice(c * FFN_SUB, (c + 1) * FFN_SUB)
        fa = _dot(h, wa_ref[:, cs])
        fg = _dot(h, wg_ref[:, cs])
        fx_ref[SUBLANE:, cs] = fa
        prev1, prev2 = history(cs, fx_ref[pl.ds(SUBLANE - 1, tm), cs], fx_ref[pl.ds(SUBLANE - 2, tm), cs])
        fc = cw_ref[2:3, cs] * fa + cw_ref[1:2, cs] * prev1 + cw_ref[0:1, cs] * prev2 + cb_ref[:, cs]
        ps.append((jax.nn.gelu(fc) * fg).astype(BF16))
    return jnp.concatenate(ps, axis=1)


def _ffn_prompt_kernel(h_ref, x1_hbm, wa_ref, wg_ref, wd_ref, cw_ref, cb_ref, g_ref, y_ref, tail_ref,
                       fx_ref, carry_ref, x1_ref, x1_sem, *, nt):
    tm = h_ref.shape[0]
    tf = wa_ref.shape[1]
    i = pl.program_id(0)
    j = pl.program_id(1)
    x1_copy = pltpu.make_async_copy(x1_hbm.at[pl.ds(i * tm, tm), :], x1_ref, x1_sem)

    @pl.when(j == 0)
    def _():
        x1_copy.start()
        y_ref[...] = jnp.zeros(y_ref.shape, F32)

    @pl.when(i % nt == 0)
    def _():
        fx_ref[0:SUBLANE, :] = jnp.zeros((SUBLANE, tf), F32)

    @pl.when(i % nt != 0)
    def _():
        fx_ref[0:SUBLANE, :] = carry_ref[j]

    p = _ffn_hidden(h_ref[...], wa_ref, wg_ref, cw_ref, cb_ref, fx_ref, lambda cs, p1, p2: (p1, p2))
    last = fx_ref[tm:tm + SUBLANE, :]
    carry_ref[j] = last
    tail_ref[0] = last
    y_ref[...] += _dot(p, wd_ref[...])

    @pl.when(j == pl.num_programs(1) - 1)
    def _():
        x1_copy.wait()

        def norm_rows(r, carry):
            rs = pl.ds(pl.multiple_of(r * CHUNK, CHUNK), CHUNK)
            y_ref[rs, :] = _rms(x1_ref[rs, :] + y_ref[rs, :], g_ref[...])
            return carry

        lax.fori_loop(0, tm // CHUNK, norm_rows, 0)


def _ffn_prompt_call(h2, x1, wa, wg, wd, cw, cb, g_final, nb):
    m = h2.shape[0]
    tm = FFN_TM
    nt = m // nb // tm
    nj = D_FF_PAD // TF
    row = lambda i, j: (i, 0)
    col = lambda i, j: (0, j)
    return pl.pallas_call(
        functools.partial(_ffn_prompt_kernel, nt=nt),
        grid=(m // tm, nj),
        in_specs=[pl.BlockSpec((tm, D_MODEL), row),
                  pl.BlockSpec(memory_space=pl.ANY),
                  pl.BlockSpec((D_MODEL, TF), col),
                  pl.BlockSpec((D_MODEL, TF), col),
                  pl.BlockSpec((TF, D_MODEL), lambda i, j: (j, 0)),
                  pl.BlockSpec((CONV_F, TF), col),
                  pl.BlockSpec((1, TF), col),
                  pl.BlockSpec((1, D_MODEL), lambda i, j: (0, 0))],
        out_specs=[pl.BlockSpec((tm, D_MODEL), row),
                   pl.BlockSpec((1, SUBLANE, TF), lambda i, j: (i, 0, j))],
        out_shape=[jax.ShapeDtypeStruct((m, D_MODEL), F32),
                   jax.ShapeDtypeStruct((m // tm, SUBLANE, D_FF_PAD), F32)],
        scratch_shapes=[pltpu.VMEM((tm + SUBLANE, TF), F32),
                        pltpu.VMEM((nj, SUBLANE, TF), F32),
                        pltpu.VMEM((tm, D_MODEL), F32),
                        pltpu.SemaphoreType.DMA(())],
        compiler_params=_params("arbitrary", "arbitrary"),
        name="ffn_prompt",
    )(h2, x1, wa, wg, wd, cw, cb, g_final.reshape(1, D_MODEL))


def _ffn_sample_kernel(h_ref, x1_ref, wa_ref, wg_ref, wd_ref, cw_ref, cb_ref, g_ref, hist_ref,
                       y_ref, st_ref, fx_ref, h1_ref, h2_ref, fa3_ref):
    nb, nl, tf = fa3_ref.shape
    j = pl.program_id(1)
    live = (j * tf + lax.broadcasted_iota(jnp.int32, (nb, tf), 1)) < D_FF
    older = jnp.where(live, hist_ref[:, 0, :], 0.0)
    newer = jnp.where(live, hist_ref[:, 1, :], 0.0)
    h1_ref[...] = jnp.zeros_like(h1_ref)
    h2_ref[...] = jnp.zeros_like(h2_ref)
    h1_ref[:, 0, :] = newer
    h2_ref[:, 0, :] = older
    h2_ref[:, 1, :] = newer
    fx_ref[0:SUBLANE, :] = jnp.zeros((SUBLANE, tf), F32)
    t = lax.broadcasted_iota(jnp.int32, (nb * nl, FFN_SUB), 0) % nl

    def history(cs, prev1, prev2):
        h1 = h1_ref[:, :, cs].reshape(nb * nl, FFN_SUB)
        h2 = h2_ref[:, :, cs].reshape(nb * nl, FFN_SUB)
        return jnp.where(t >= 1, prev1, h1), jnp.where(t >= 2, prev2, h2)

    p = _ffn_hidden(h_ref[...], wa_ref, wg_ref, cw_ref, cb_ref, fx_ref, history)
    fa3_ref[...] = fx_ref[SUBLANE:, :].reshape(nb, nl, tf)
    for r in range(CONV_F - 1):
        st_ref[:, r, :] = fa3_ref[:, nl - (CONV_F - 1) + r, :]

    @pl.when(j == 0)
    def _():
        y_ref[...] = x1_ref[...]

    y_ref[...] += _dot(p, wd_ref[...])

    @pl.when(j == pl.num_programs(1) - 1)
    def _():
        y_ref[...] = _rms(y_ref[...], g_ref[...])


def _ffn_sample_call(h2, x1, wa, wg, wd, cw, cb, g_final, state, l, nb):
    m = h2.shape[0]
    nl = m // nb
    nj = D_FF_PAD // TF
    row = lambda i, j: (i, 0)
    col = lambda i, j: (0, j)
    return pl.pallas_call(
        _ffn_sample_kernel,
        grid=(1, nj),
        in_specs=[pl.BlockSpec((m, D_MODEL), row, pipeline_mode=pl.Buffered(1)),
                  pl.BlockSpec((m, D_MODEL), row, pipeline_mode=pl.Buffered(1)),
                  pl.BlockSpec((D_MODEL, TF), col),
                  pl.BlockSpec((D_MODEL, TF), col),
                  pl.BlockSpec((TF, D_MODEL), lambda i, j: (j, 0)),
                  pl.BlockSpec((CONV_F, TF), col),
                  pl.BlockSpec((1, TF), col),
                  pl.BlockSpec((1, D_MODEL), lambda i, j: (0, 0)),
                  pl.BlockSpec((None, nb, CONV_F - 1, TF), lambda i, j: (l, 0, 0, j))],
        out_specs=[pl.BlockSpec((m, D_MODEL), row),
                   pl.BlockSpec((None, nb, CONV_F - 1, TF), lambda i, j: (0, 0, 0, j))],
        out_shape=[jax.ShapeDtypeStruct((m, D_MODEL), F32),
                   jax.ShapeDtypeStruct((1, nb, CONV_F - 1, D_FF), F32)],
        scratch_shapes=[pltpu.VMEM((m + SUBLANE, TF), F32),
                        pltpu.VMEM((nb, nl, TF), F32),
                        pltpu.VMEM((nb, nl, TF), F32),
                        pltpu.VMEM((nb, nl, TF), F32)],
        compiler_params=_params("arbitrary", "arbitrary"),
        name="ffn_sample",
    )(h2, x1, wa, wg, wd, cw, cb, g_final.reshape(1, D_MODEL), state)


def kernel(x_prompt, x_sample, mem_prompt, cache_mem_k, cache_mem_v, state_conv, state_ffn_conv, g_mix, w_in, ln_v_g, ln_v_b, w_s, b_s, w_pa, conv_w, conv_b, ln_b_g, ln_b_b, w_pb, g_mem, w_k, w_v, w_pc, w_o, g_ffn, w_up, ffn_conv_w, ffn_conv_b, w_down, g_final):
    depth = g_mix.shape[0]
    assert depth == 1
    l = 0
    nbp, lp, _ = x_prompt.shape
    nbs, ls, _ = x_sample.shape
    mp = nbp * lp
    pad_ff = D_FF_PAD - D_FF

    w_in_b = w_in[l][:, :OFF_GATES].astype(BF16)
    w_k_b = w_k[l].astype(BF16)
    w_v_b = w_v[l].astype(BF16)
    cw = jnp.pad(ffn_conv_w[l], ((0, 0), (0, pad_ff)))
    cb = jnp.pad(ffn_conv_b[l], (0, pad_ff)).reshape(1, D_FF_PAD)

    reps = CHUNK // ls
    ws_p = w_s[l]
    ws_s = jnp.tile(w_s[l][:, :ls, :ls], (1, reps, reps))
    bs_p = jnp.repeat(b_s[l].T, D_A // G_A, axis=1)
    bs_s = jnp.repeat(jnp.tile(b_s[l][:, :ls], (1, reps)).T, D_A // G_A, axis=1)

    xp = x_prompt.reshape(mp, D_MODEL)
    h, a, bact, q, ctail, w_pa_b, w_pb_b, w_pc_b, w_o_b = _inproj_prompt_call(
        xp, nbp, g_mix[l], w_in_b, ln_v_g[l], ln_v_b[l], ws_p, bs_p, conv_w[l], conv_b[l], ln_b_g[l], ln_b_b[l],
        (w_pa[l], w_pb[l], w_pc[l], w_o[l]))
    k5, v5, kb, vb = _memkv_call(mem_prompt, g_mem[l], w_k_b, w_v_b)
    cact, w_gates = _attn_prompt_call(q, kb, vb, w_in[l])
    mix, w_up_a, w_up_g, w_down_b = _mix_call(h, a, bact, cact, w_pa_b, w_pb_b, w_pc_b, w_gates, "mix_prompt",
                                              (w_up[l], w_down[l]))
    x1, h2 = _oproj_call(xp, mix, w_o_b, g_ffn[l], "oproj_prompt")
    yp, tail = _ffn_prompt_call(h2, x1, w_up_a, w_up_g, w_down_b, cw, cb, g_final, nbp)
    nt = tail.shape[0] // nbp
    ffn_p = tail.reshape(nbp, nt, SUBLANE, D_FF_PAD)[:, nt - 1, SUBLANE - (CONV_F - 1):, :D_FF]
    nt = ctail.shape[0] // nbp
    conv_p = ctail.reshape(nbp, nt, HALO, D_B)[:, nt - 1, HALO - (CONV_B - 1):]

    h, a, q, glu_s, vn_s = _inproj_sample_call(x_sample, g_mix[l], w_in_b, ln_v_g[l], ln_v_b[l], ws_s, bs_s)
    bact, conv_s = _convb_sample_call(jnp.swapaxes(state_conv, 1, 2), l, glu_s, conv_w[l], conv_b[l],
                                      ln_b_g[l], ln_b_b[l])
    conv_s = jnp.swapaxes(conv_s, 1, 2)
    cact = _attn_sample_call(q, cache_mem_k, cache_mem_v, l, ls)
    (mix,) = _mix_call(h, a, bact, cact, w_pa_b, w_pb_b, w_pc_b, w_gates, "mix_sample")
    x1, h2 = _oproj_call(x_sample, mix, w_o_b, g_ffn[l], "oproj_sample")
    ys, ffn_s = _ffn_sample_call(h2, x1, w_up_a, w_up_g, w_down_b, cw, cb, g_final, state_ffn_conv, l, nbs)

    return (yp.reshape(nbp, lp, D_MODEL), ys.reshape(nbs, ls, D_MODEL), k5, v5, conv_p[None], ffn_p[None],
            conv_s, ffn_s, vn_s)
```

```python
import functools

import jax
import jax.numpy as jnp
from jax import lax
from jax.experimental import pallas as pl
from jax.experimental.pallas import tpu as pltpu

F32 = jnp.float32
BF16 = jnp.bfloat16

D_MODEL = 2048
CHUNK = 128
D_A = D_MODEL // 2
G_A = 8
D_B = D_MODEL // 2
CONV_B = 31
N_MEM = 256
N_XHEADS = 4
XHEAD_DIM = D_MODEL // 8
D_C = N_XHEADS * XHEAD_DIM
D_FF = ((8 * D_MODEL // 3 + 127) // 128) * 128
CONV_F = 3
EPS = 1e-6

OFF_ZA = 0
OFF_ZB = 2 * D_A
OFF_Q = OFF_ZB + 2 * D_B
OFF_GATES = OFF_Q + D_C

LANE = 128
SUBLANE = 8
TM = 512
TF = 512
D_FF_PAD = ((D_FF + TF - 1) // TF) * TF
HALO = 32
VMEM_LIMIT = 60 * 1024 * 1024


def _params(*sem):
    return pltpu.CompilerParams(dimension_semantics=sem, vmem_limit_bytes=VMEM_LIMIT)


def _rms(x, g):
    return x * lax.rsqrt(jnp.mean(x * x, axis=-1, keepdims=True) + EPS) * g


def _layer_norm(x, g, b):
    mu = jnp.mean(x, axis=-1, keepdims=True)
    d = x - mu
    var = jnp.mean(d * d, axis=-1, keepdims=True)
    return d * lax.rsqrt(var + EPS) * g + b


def _dot(a, b):
    return jnp.dot(a, b, preferred_element_type=F32)


def _softmax(s):
    e = jnp.exp(s - jnp.max(s, axis=-1, keepdims=True))
    return e / jnp.sum(e, axis=-1, keepdims=True)


def _spatial_gate(u, vb, ws_ref, bs_ref, a_ref, sample):
    tm = vb.shape[0]
    r = lax.broadcasted_iota(jnp.int32, (CHUNK, CHUNK), 0)
    c = lax.broadcasted_iota(jnp.int32, (CHUNK, CHUNK), 1)
    mask = r >= c
    if sample:
        mask = mask & ((r >> 2) == (c >> 2))
    for g in range(G_A):
        cs = slice(g * LANE, (g + 1) * LANE)
        wm = jnp.where(mask, ws_ref[g], 0.0).astype(BF16)
        for ch in range(tm // CHUNK):
            rs = slice(ch * CHUNK, (ch + 1) * CHUNK)
            s = _dot(wm, vb[rs, cs]) + bs_ref[:, cs]
            a_ref[rs, cs] = (u[rs, cs] * s).astype(BF16)


IN_SUB = 256
CONV_ROWS = 128


def _conv_taps(xc_ref, sh_ref, w_ref, bias, r0, cs):
    lead = HALO - (CONV_B - 1)
    acc = jnp.broadcast_to(bias, (CONV_ROWS, bias.shape[1]))
    for s in range(SUBLANE):
        qs = [q for q in range(HALO // SUBLANE + 1) if lead <= SUBLANE * q + s < lead + CONV_B]
        n = CONV_ROWS + SUBLANE * qs[-1]
        if s:
            sh_ref[0:n, :] = xc_ref[pl.ds(r0 + s, n), cs]
        for q in qs:
            k = SUBLANE * q + s - lead
            if s:
                win = sh_ref[SUBLANE * q:SUBLANE * q + CONV_ROWS, :]
            else:
                win = xc_ref[r0 + SUBLANE * q:r0 + SUBLANE * q + CONV_ROWS, cs]
            acc = acc + w_ref[k:k + 1, cs] * win
    return acc


def _inproj_prompt_kernel(x_ref, gm_ref, wa_ref, wb_ref, wq_ref, lvg_ref, lvb_ref, ws_ref, bs_ref,
                          cw_ref, cb_ref, lbg_ref, lbb_ref, wpa_f, wpb_f, wpc_f, wo_f,
                          h_ref, a_ref, b_ref, q_ref, ctail_ref, wpa_o, wpb_o, wpc_o, wo_o,
                          xc_ref, cv_ref, sh_ref, u_ref, v_ref, *, nt):
    tm = x_ref.shape[0]
    t = pl.program_id(0) % nt
    for src, dst in ((wpa_f, wpa_o), (wpb_f, wpb_o), (wpc_f, wpc_o), (wo_f, wo_o)):
        dst[...] = src[...].astype(BF16)

    @pl.when(t == 0)
    def _():
        xc_ref[0:HALO, :] = jnp.zeros((HALO, D_B), F32)

    @pl.when(t != 0)
    def _():
        xc_ref[0:HALO, :] = xc_ref[tm:tm + HALO, :]

    h = _rms(x_ref[...], gm_ref[...]).astype(BF16)
    h_ref[...] = h

    for c in range(D_B // IN_SUB):
        cs = slice(c * IN_SUB, (c + 1) * IN_SUB)
        za = _dot(h, wb_ref[:, cs])
        zb = _dot(h, wb_ref[:, D_B + c * IN_SUB:D_B + (c + 1) * IN_SUB])
        xc_ref[HALO:, cs] = za * jax.nn.sigmoid(zb)
        for rb in range(tm // CONV_ROWS):
            r0 = rb * CONV_ROWS
            cv_ref[r0:r0 + CONV_ROWS, cs] = _conv_taps(xc_ref, sh_ref, cw_ref, cb_ref[:, cs], r0, cs)
    ctail_ref[0] = xc_ref[tm:tm + HALO, :]
    y = _layer_norm(cv_ref[...], lbg_ref[...], lbb_ref[...])
    b_ref[...] = (y * jax.nn.sigmoid(y)).astype(BF16)

    for c in range(D_A // IN_SUB):
        cs = slice(c * IN_SUB, (c + 1) * IN_SUB)
        u_ref[:, cs] = jax.nn.gelu(_dot(h, wa_ref[:, cs]))
        v_ref[:, cs] = jax.nn.gelu(_dot(h, wa_ref[:, D_A + c * IN_SUB:D_A + (c + 1) * IN_SUB]))
    q_ref[...] = _dot(h, wq_ref[...]).astype(BF16)
    vb = _layer_norm(v_ref[...], lvg_ref[...], lvb_ref[...]).astype(BF16)
    _spatial_gate(u_ref, vb, ws_ref, bs_ref, a_ref, False)


def _inproj_weight_specs(const_map):
    single = pl.Buffered(1)
    return [pl.BlockSpec((D_MODEL, 2 * D_A), lambda *i: (0, OFF_ZA // (2 * D_A)), pipeline_mode=single),
            pl.BlockSpec((D_MODEL, 2 * D_B), lambda *i: (0, OFF_ZB // (2 * D_B)), pipeline_mode=single),
            pl.BlockSpec((D_MODEL, D_C), lambda *i: (0, OFF_Q // D_C), pipeline_mode=single)]


def _inproj_prompt_call(x, nb, g_mix, w_in, ln_v_g, ln_v_b, ws, bs, conv_w, conv_b, ln_b_g, ln_b_b, casts):
    m = x.shape[0]
    tm = TM
    nt = m // nb // tm
    steps = m // tm
    slab = lambda w: pl.BlockSpec((w.shape[0] // steps, w.shape[1]), lambda i: (i, 0))
    const = lambda i: (0, 0)
    row = lambda i: (i, 0)
    vec = lambda n: pl.BlockSpec((1, n), const)
    return pl.pallas_call(
        functools.partial(_inproj_prompt_kernel, nt=nt),
        grid=(m // tm,),
        in_specs=[pl.BlockSpec((tm, D_MODEL), row), vec(D_MODEL)] + _inproj_weight_specs(const) + [
                  vec(D_A), vec(D_A),
                  pl.BlockSpec((G_A, CHUNK, CHUNK), lambda i: (0, 0, 0)),
                  pl.BlockSpec((CHUNK, D_A), const),
                  pl.BlockSpec((CONV_B, D_B), const),
                  vec(D_B), vec(D_B), vec(D_B)] + [slab(w) for w in casts],
        out_specs=[pl.BlockSpec((tm, D_MODEL), row),
                   pl.BlockSpec((tm, D_A), row),
                   pl.BlockSpec((tm, D_B), row),
                   pl.BlockSpec((tm, D_C), row),
                   pl.BlockSpec((1, HALO, D_B), lambda i: (i, 0, 0))] + [slab(w) for w in casts],
        out_shape=[jax.ShapeDtypeStruct((m, D_MODEL), BF16),
                   jax.ShapeDtypeStruct((m, D_A), BF16),
                   jax.ShapeDtypeStruct((m, D_B), BF16),
                   jax.ShapeDtypeStruct((m, D_C), BF16),
                   jax.ShapeDtypeStruct((m // tm, HALO, D_B), F32)]
                  + [jax.ShapeDtypeStruct(w.shape, BF16) for w in casts],
        scratch_shapes=[pltpu.VMEM((tm + HALO, D_B), F32),
                        pltpu.VMEM((tm, D_B), F32),
                        pltpu.VMEM((CONV_ROWS + HALO, IN_SUB), F32),
                        pltpu.VMEM((tm, D_A), F32),
                        pltpu.VMEM((tm, D_A), F32)],
        compiler_params=_params("arbitrary"),
        name="inproj_prompt",
    )(x, g_mix.reshape(1, D_MODEL), w_in, w_in, w_in, ln_v_g.reshape(1, D_A), ln_v_b.reshape(1, D_A), ws, bs,
      conv_w, conv_b.reshape(1, D_B), ln_b_g.reshape(1, D_B), ln_b_b.reshape(1, D_B), *casts)


def _inproj_sample_kernel(x_ref, gm_ref, wa_ref, wb_ref, wq_ref, lvg_ref, lvb_ref, ws_ref, bs_ref,
                          h_ref, a_ref, q_ref, glu_ref, vn_ref):
    nb, nl, _ = x_ref.shape
    m = nb * nl
    h = _rms(x_ref[...].reshape(m, D_MODEL), gm_ref[...]).astype(BF16)
    h_ref[...] = h
    zg = jax.nn.gelu(_dot(h, wa_ref[...]))
    vn = _layer_norm(zg[:, D_A:], lvg_ref[...], lvb_ref[...])
    vn_ref[...] = vn.reshape(nb, nl, D_A)
    _spatial_gate(zg[:, :D_A], vn.astype(BF16), ws_ref, bs_ref, a_ref, True)
    z = _dot(h, wb_ref[...])
    glu_ref[...] = (z[:, :D_B] * jax.nn.sigmoid(z[:, D_B:])).reshape(nb, nl, D_B)
    q_ref[...] = _dot(h, wq_ref[...])


def _inproj_sample_call(x, g_mix, w_in, ln_v_g, ln_v_b, ws, bs):
    nb, nl, _ = x.shape
    m = nb * nl
    bt = CHUNK // nl
    const = lambda i: (0, 0)
    vec = lambda n: pl.BlockSpec((1, n), const)
    flat = lambda n: pl.BlockSpec((bt * nl, n), lambda i: (i, 0))
    return pl.pallas_call(
        _inproj_sample_kernel,
        grid=(nb // bt,),
        in_specs=[pl.BlockSpec((bt, nl, D_MODEL), lambda i: (i, 0, 0)), vec(D_MODEL)]
                 + _inproj_weight_specs(const) + [
                  vec(D_A), vec(D_A),
                  pl.BlockSpec((G_A, CHUNK, CHUNK), lambda i: (0, 0, 0)),
                  pl.BlockSpec((CHUNK, D_A), const)],
        out_specs=[flat(D_MODEL), flat(D_A), flat(D_C),
                   pl.BlockSpec((bt, nl, D_B), lambda i: (i, 0, 0)),
                   pl.BlockSpec((None, bt, nl, D_A), lambda i: (0, i, 0, 0))],
        out_shape=[jax.ShapeDtypeStruct((m, D_MODEL), BF16),
                   jax.ShapeDtypeStruct((m, D_A), BF16),
                   jax.ShapeDtypeStruct((m, D_C), F32),
                   jax.ShapeDtypeStruct((nb, nl, D_B), F32),
                   jax.ShapeDtypeStruct((1, nb, nl, D_A), F32)],
        compiler_params=_params("arbitrary"),
        name="inproj_sample",
    )(x, g_mix.reshape(1, D_MODEL), w_in, w_in, w_in, ln_v_g.reshape(1, D_A), ln_v_b.reshape(1, D_A), ws, bs)


def _convb_sample_kernel(hist_ref, x_ref, w_ref, cb_ref, lng_ref, lnb_ref, o_ref, nh_ref):
    bt, nl, _ = x_ref.shape
    nh = hist_ref.shape[0]
    new_rows = [x_ref[:, t, :] for t in range(nl)]
    window = lambda j: hist_ref[j] if j < nh else new_rows[j - nh]
    for t in range(nl):
        acc = jnp.broadcast_to(cb_ref[...], (bt, D_B))
        for k in range(CONV_B):
            acc = acc + w_ref[k:k + 1, :] * window(t + k)
        y = _layer_norm(acc, lng_ref[...], lnb_ref[...])
        o_ref[:, t, :] = y * jax.nn.sigmoid(y)
    for j in range(nh):
        nh_ref[j] = window(j + nl)


def _convb_sample_call(hist, l, x, conv_w, conv_b, ln_g, ln_b):
    nb, nl, _ = x.shape
    nh = hist.shape[1]
    bt = 16
    const = lambda i: (0, 0)
    state_spec = pl.BlockSpec((None, nh, bt, D_B), lambda i: (l, 0, i, 0))
    new_spec = pl.BlockSpec((bt, nl, D_B), lambda i: (i, 0, 0))
    return pl.pallas_call(
        _convb_sample_kernel,
        grid=(nb // bt,),
        in_specs=[state_spec, new_spec,
                  pl.BlockSpec((CONV_B, D_B), const),
                  pl.BlockSpec((1, D_B), const),
                  pl.BlockSpec((1, D_B), const),
                  pl.BlockSpec((1, D_B), const)],
        out_specs=[new_spec, pl.BlockSpec((None, nh, bt, D_B), lambda i: (0, 0, i, 0))],
        out_shape=[jax.ShapeDtypeStruct((nb, nl, D_B), F32),
                   jax.ShapeDtypeStruct((1, nh, nb, D_B), F32)],
        compiler_params=_params("parallel"),
        name="convb_sample",
    )(hist, x, conv_w, conv_b.reshape(1, D_B), ln_g.reshape(1, D_B), ln_b.reshape(1, D_B))


def _memkv_kernel(m_ref, g_ref, wk_ref, wv_ref, k5_ref, v5_ref, kb_ref, vb_ref):
    mn = _rms(m_ref[...], g_ref[...]).astype(BF16)
    k = _dot(mn, wk_ref[...].astype(BF16))
    v = _dot(mn, wv_ref[...].astype(BF16))
    k5_ref[...] = k.reshape(N_MEM, N_XHEADS, XHEAD_DIM)
    v5_ref[...] = v.reshape(N_MEM, N_XHEADS, XHEAD_DIM)
    kb_ref[...] = k.astype(BF16)
    vb_ref[...] = v.astype(BF16)


def _memkv_call(mem, g_mem, w_k, w_v):
    nb = mem.shape[0]
    const = lambda i: (0, 0)
    cache_spec = pl.BlockSpec((None, None, N_MEM, N_XHEADS, XHEAD_DIM), lambda i: (0, i, 0, 0, 0))
    seq_spec = pl.BlockSpec((None, N_MEM, D_C), lambda i: (i, 0, 0))
    cache_shape = jax.ShapeDtypeStruct((1, nb, N_MEM, N_XHEADS, XHEAD_DIM), F32)
    return pl.pallas_call(
        _memkv_kernel,
        grid=(nb,),
        in_specs=[pl.BlockSpec((None, N_MEM, D_MODEL), lambda i: (i, 0, 0)),
                  pl.BlockSpec((1, D_MODEL), const),
                  pl.BlockSpec((D_MODEL, D_C), const),
                  pl.BlockSpec((D_MODEL, D_C), const)],
        out_specs=[cache_spec, cache_spec, seq_spec, seq_spec],
        out_shape=[cache_shape, cache_shape,
                   jax.ShapeDtypeStruct((nb, N_MEM, D_C), BF16),
                   jax.ShapeDtypeStruct((nb, N_MEM, D_C), BF16)],
        compiler_params=_params("parallel"),
        name="memory_kv",
    )(mem, g_mem.reshape(1, D_MODEL), w_k, w_v)


GATE_COLS = 1024


def _attn_prompt_kernel(q_ref, k_ref, v_ref, *refs):
    gate_refs, o_ref, gates_o = refs[:-2], refs[-2], refs[-1]
    for n, g_ref in enumerate(gate_refs):
        gates_o[:, n * GATE_COLS:(n + 1) * GATE_COLS] = g_ref[...].astype(BF16)
    for h in range(N_XHEADS):
        cs = slice(h * XHEAD_DIM, (h + 1) * XHEAD_DIM)
        s = lax.dot_general(q_ref[:, cs], k_ref[:, cs], (((1,), (1,)), ((), ())), preferred_element_type=F32)
        p = _softmax(s * (XHEAD_DIM ** -0.5)).astype(BF16)
        o_ref[:, cs] = _dot(p, v_ref[:, cs]).astype(BF16)


def _attn_prompt_call(q, k, v, w_in):
    m = q.shape[0]
    nb = k.shape[0]
    tm = TM
    nt = m // nb // tm
    n_gate = 3 * D_MODEL
    slab = D_MODEL // (nb * nt)
    assert OFF_GATES % GATE_COLS == 0 and n_gate % GATE_COLS == 0 and D_MODEL % (nb * nt) == 0
    gate_specs = [pl.BlockSpec((slab, GATE_COLS), lambda b, t, c=c: (b * nt + t, OFF_GATES // GATE_COLS + c))
                  for c in range(n_gate // GATE_COLS)]
    return pl.pallas_call(
        _attn_prompt_kernel,
        grid=(nb, nt),
        in_specs=[pl.BlockSpec((tm, D_C), lambda b, t: (b * nt + t, 0)),
                  pl.BlockSpec((None, N_MEM, D_C), lambda b, t: (b, 0, 0)),
                  pl.BlockSpec((None, N_MEM, D_C), lambda b, t: (b, 0, 0))] + gate_specs,
        out_specs=[pl.BlockSpec((tm, D_C), lambda b, t: (b * nt + t, 0)),
                   pl.BlockSpec((slab, n_gate), lambda b, t: (b * nt + t, 0))],
        out_shape=[jax.ShapeDtypeStruct((m, D_C), BF16),
                   jax.ShapeDtypeStruct((D_MODEL, n_gate), BF16)],
        compiler_params=_params("arbitrary", "arbitrary"),
        name="attn_prompt",
    )(q, k, v, *([w_in] * len(gate_specs)))


def _attn_sample_kernel(q_ref, k_ref, v_ref, o_ref):
    bt = k_ref.shape[0]
    nl = q_ref.shape[0] // bt
    rows = lax.broadcasted_iota(jnp.int32, (N_XHEADS * nl, N_MEM * N_XHEADS), 0)
    cols = lax.broadcasted_iota(jnp.int32, (N_XHEADS * nl, N_MEM * N_XHEADS), 1)
    same_head = (cols % N_XHEADS) == (rows // nl)
    outs = []
    for b in range(bt):
        q = q_ref[b * nl:(b + 1) * nl, :]
        qs = jnp.concatenate([q[:, h * XHEAD_DIM:(h + 1) * XHEAD_DIM] for h in range(N_XHEADS)], axis=0)
        k2 = k_ref[b].reshape(N_MEM * N_XHEADS, XHEAD_DIM).astype(BF16)
        v2 = v_ref[b].reshape(N_MEM * N_XHEADS, XHEAD_DIM).astype(BF16)
        s = lax.dot_general(qs.astype(BF16), k2, (((1,), (1,)), ((), ())), preferred_element_type=F32)
        p = _softmax(jnp.where(same_head, s * (XHEAD_DIM ** -0.5), -jnp.inf)).astype(BF16)
        o = _dot(p, v2)
        outs.append(jnp.concatenate([o[h * nl:(h + 1) * nl, :] for h in range(N_XHEADS)], axis=1))
    o_ref[...] = jnp.concatenate(outs, axis=0).astype(BF16)


def _attn_sample_call(q, k, v, l, nl):
    m = q.shape[0]
    bt = 8
    cache_spec = pl.BlockSpec((None, bt, N_MEM, N_XHEADS, XHEAD_DIM), lambda i: (l, i, 0, 0, 0))
    return pl.pallas_call(
        _attn_sample_kernel,
        grid=(m // (bt * nl),),
        in_specs=[pl.BlockSpec((bt * nl, D_C), lambda i: (i, 0)), cache_spec, cache_spec],
        out_specs=pl.BlockSpec((bt * nl, D_C), lambda i: (i, 0)),
        out_shape=jax.ShapeDtypeStruct((m, D_C), BF16),
        compiler_params=_params("parallel"),
        name="attn_sample",
    )(q, k, v)


def _cast_ffn_weights(wu_f, wd_f, wa_o, wg_o, wd_o, step):
    wu = wu_f[...]
    zpad = jnp.zeros((wu.shape[0], D_FF_PAD - D_FF), BF16)
    wa_o[:, :D_FF] = wu[:, :D_FF].astype(BF16)
    wa_o[:, D_FF:] = zpad
    wg_o[:, :D_FF] = wu[:, D_FF:].astype(BF16)
    wg_o[:, D_FF:] = zpad
    live_blocks = D_FF // wd_f.shape[0]

    @pl.when(step < live_blocks)
    def _():
        wd_o[...] = wd_f[...].astype(BF16)

    @pl.when(step >= live_blocks)
    def _():
        wd_o[...] = jnp.zeros(wd_o.shape, BF16)


def _mix_kernel(*refs, cast_ffn):
    h_ref, a_ref, b_ref, c_ref, wpa_ref, wpb_ref, wpc_ref, wga_ref, wgb_ref, wgc_ref = refs[:10]
    if cast_ffn:
        wu_f, wd_f, o_ref, wa_o, wg_o, wd_o = refs[10:]
        _cast_ffn_weights(wu_f, wd_f, wa_o, wg_o, wd_o, pl.program_id(0) * pl.num_programs(1) + pl.program_id(1))
    else:
        (o_ref,) = refs[10:]
    h = h_ref[...]
    b = b_ref[...].reshape(h.shape[0], D_B).astype(BF16)
    mix = jax.nn.sigmoid(_dot(h, wga_ref[...])) * _dot(a_ref[...], wpa_ref[...])
    mix = mix + jax.nn.sigmoid(_dot(h, wgb_ref[...])) * _dot(b, wpb_ref[...])
    mix = mix + jax.nn.sigmoid(_dot(h, wgc_ref[...])) * _dot(c_ref[...], wpc_ref[...])
    o_ref[...] = mix.astype(BF16)


def _mix_call(h, a, b, c, w_pa, w_pb, w_pc, w_gates, name, ffn_weights=None):
    m = h.shape[0]
    tm = min(TM, m)
    tn = 512
    nj = D_MODEL // tn
    per_gate = D_MODEL // tn
    row = lambda i, j: (i, 0)
    col = lambda i, j: (0, j)
    if b.ndim == 3:
        assert m == tm
        b_spec = pl.BlockSpec(b.shape, lambda i, j: (0, 0, 0))
    else:
        b_spec = pl.BlockSpec((tm, D_B), row)
    in_specs = [pl.BlockSpec((tm, D_MODEL), row),
                pl.BlockSpec((tm, D_A), row),
                b_spec,
                pl.BlockSpec((tm, D_C), row),
                pl.BlockSpec((D_A, tn), col),
                pl.BlockSpec((D_B, tn), col),
                pl.BlockSpec((D_C, tn), col),
                pl.BlockSpec((D_MODEL, tn), lambda i, j: (0, j)),
                pl.BlockSpec((D_MODEL, tn), lambda i, j: (0, per_gate + j)),
                pl.BlockSpec((D_MODEL, tn), lambda i, j: (0, 2 * per_gate + j))]
    out_specs = [pl.BlockSpec((tm, tn), lambda i, j: (i, j))]
    out_shape = [jax.ShapeDtypeStruct((m, D_MODEL), BF16)]
    args = [h, a, b, c, w_pa, w_pb, w_pc, w_gates, w_gates, w_gates]
    if ffn_weights is not None:
        w_up, w_down = ffn_weights
        steps = (m // tm) * nj
        up_rows = D_MODEL // steps
        down_rows = CHUNK
        down_blocks = D_FF_PAD // down_rows
        assert D_MODEL % steps == 0 and D_FF % down_rows == 0 and down_blocks <= steps
        step = lambda i, j: i * nj + j
        in_specs += [pl.BlockSpec((up_rows, 2 * D_FF), lambda i, j: (step(i, j), 0)),
                     pl.BlockSpec((down_rows, D_MODEL),
                                  lambda i, j: (jnp.minimum(step(i, j), D_FF // down_rows - 1), 0))]
        out_specs += [pl.BlockSpec((up_rows, D_FF_PAD), lambda i, j: (step(i, j), 0)),
                      pl.BlockSpec((up_rows, D_FF_PAD), lambda i, j: (step(i, j), 0)),
                      pl.BlockSpec((down_rows, D_MODEL),
                                   lambda i, j: (jnp.minimum(step(i, j), down_blocks - 1), 0))]
        out_shape += [jax.ShapeDtypeStruct((D_MODEL, D_FF_PAD), BF16),
                      jax.ShapeDtypeStruct((D_MODEL, D_FF_PAD), BF16),
                      jax.ShapeDtypeStruct((D_FF_PAD, D_MODEL), BF16)]
        args += [w_up, w_down]
    return pl.pallas_call(
        functools.partial(_mix_kernel, cast_ffn=ffn_weights is not None),
        grid=(m // tm, nj),
        in_specs=in_specs,
        out_specs=out_specs,
        out_shape=out_shape,
        compiler_params=_params("arbitrary", "arbitrary"),
        name=name,
    )(*args)


def _oproj_kernel(x_ref, mix_ref, wo_ref, g_ref, x1_ref, h2_ref):
    x1 = x_ref[...] + _dot(mix_ref[...], wo_ref[...])
    x1_ref[...] = x1
    h2_ref[...] = _rms(x1, g_ref[...]).astype(BF16)


def _oproj_call(x, mix, w_o, g_ffn):
    m = mix.shape[0]
    tm = TM
    row = lambda i: (i, 0)
    return pl.pallas_call(
        _oproj_kernel,
        grid=(m // tm,),
        in_specs=[pl.BlockSpec((tm, D_MODEL), row),
                  pl.BlockSpec((tm, D_MODEL), row),
                  pl.BlockSpec((D_MODEL, D_MODEL), lambda i: (0, 0)),
                  pl.BlockSpec((1, D_MODEL), lambda i: (0, 0))],
        out_specs=[pl.BlockSpec((tm, D_MODEL), row),
                   pl.BlockSpec((tm, D_MODEL), row)],
        out_shape=[jax.ShapeDtypeStruct((m, D_MODEL), F32),
                   jax.ShapeDtypeStruct((m, D_MODEL), BF16)],
        compiler_params=_params("parallel"),
        name="oproj_prompt",
    )(x, mix, w_o, g_ffn.reshape(1, D_MODEL))


def _oproj_sample_kernel(x_ref, mix_ref, wo_ref, g_ref, x1_ref, h2_ref, x3_ref):
    nb, nl, _ = x_ref.shape
    x3_ref[...] = x_ref[...] + _dot(mix_ref[...], wo_ref[...]).reshape(nb, nl, D_MODEL)
    x1 = jnp.concatenate([x3_ref[:, t, :] for t in range(nl)], axis=0)
    x1_ref[...] = x1
    h2_ref[...] = _rms(x1, g_ref[...]).astype(BF16)


def _oproj_sample_call(x, mix, w_o, g_ffn):
    nb, nl, _ = x.shape
    m = nb * nl
    const = lambda i: (0, 0)
    single = pl.Buffered(1)
    return pl.pallas_call(
        _oproj_sample_kernel,
        grid=(1,),
        in_specs=[pl.BlockSpec((nb, nl, D_MODEL), lambda i: (0, 0, 0), pipeline_mode=single),
                  pl.BlockSpec((m, D_MODEL), const, pipeline_mode=single),
                  pl.BlockSpec((D_MODEL, D_MODEL), const, pipeline_mode=single),
                  pl.BlockSpec((1, D_MODEL), const)],
        out_specs=[pl.BlockSpec((m, D_MODEL), const),
                   pl.BlockSpec((m, D_MODEL), const)],
        out_shape=[jax.ShapeDtypeStruct((m, D_MODEL), F32),
                   jax.ShapeDtypeStruct((m, D_MODEL), BF16)],
        scratch_shapes=[pltpu.VMEM((nb, nl, D_MODEL), F32)],
        compiler_params=_params("arbitrary"),
        name="oproj_sample",
    )(x, mix, w_o, g_ffn.reshape(1, D_MODEL))


FFN_TM = 1024
FFN_SUB = 256


def _ffn_hidden(h, wa_ref, wg_ref, cw_ref, cb_ref, fx_ref, base, lag):
    tm = h.shape[0]
    ps = []
    for c in range(wa_ref.shape[1] // FFN_SUB):
        cs = slice(c * FFN_SUB, (c + 1) * FFN_SUB)
        fa = _dot(h, wa_ref[:, cs])
        fg = _dot(h, wg_ref[:, cs])
        fx_ref[base:base + tm, cs] = fa
        prev1 = fx_ref[pl.ds(base - lag, tm), cs]
        prev2 = fx_ref[pl.ds(base - 2 * lag, tm), cs]
        fc = cw_ref[2:3, cs] * fa + cw_ref[1:2, cs] * prev1 + cw_ref[0:1, cs] * prev2 + cb_ref[:, cs]
        ps.append((jax.nn.gelu(fc) * fg).astype(BF16))
    return jnp.concatenate(ps, axis=1)


def _ffn_prompt_kernel(h_ref, x1_hbm, wa_ref, wg_ref, wd_ref, cw_ref, cb_ref, g_ref, y_ref, tail_ref,
                       fx_ref, carry_ref, x1_ref, x1_sem, *, nt):
    tm = h_ref.shape[0]
    tf = wa_ref.shape[1]
    i = pl.program_id(0)
    j = pl.program_id(1)
    x1_copy = pltpu.make_async_copy(x1_hbm.at[pl.ds(i * tm, tm), :], x1_ref, x1_sem)

    @pl.when(j == 0)
    def _():
        x1_copy.start()
        y_ref[...] = jnp.zeros(y_ref.shape, F32)

    @pl.when(i % nt == 0)
    def _():
        fx_ref[0:SUBLANE, :] = jnp.zeros((SUBLANE, tf), F32)

    @pl.when(i % nt != 0)
    def _():
        fx_ref[0:SUBLANE, :] = carry_ref[j]

    p = _ffn_hidden(h_ref[...], wa_ref, wg_ref, cw_ref, cb_ref, fx_ref, SUBLANE, 1)
    last = fx_ref[tm:tm + SUBLANE, :]
    carry_ref[j] = last
    tail_ref[0] = last
    y_ref[...] += _dot(p, wd_ref[...])

    @pl.when(j == pl.num_programs(1) - 1)
    def _():
        x1_copy.wait()

        def norm_rows(r, carry):
            rs = pl.ds(pl.multiple_of(r * CHUNK, CHUNK), CHUNK)
            y_ref[rs, :] = _rms(x1_ref[rs, :] + y_ref[rs, :], g_ref[...])
            return carry

        lax.fori_loop(0, tm // CHUNK, norm_rows, 0)


def _ffn_prompt_call(h2, x1, wa, wg, wd, cw, cb, g_final, nb):
    m = h2.shape[0]
    tm = FFN_TM
    nt = m // nb // tm
    nj = D_FF_PAD // TF
    row = lambda i, j: (i, 0)
    col = lambda i, j: (0, j)
    return pl.pallas_call(
        functools.partial(_ffn_prompt_kernel, nt=nt),
        grid=(m // tm, nj),
        in_specs=[pl.BlockSpec((tm, D_MODEL), row),
                  pl.BlockSpec(memory_space=pl.ANY),
                  pl.BlockSpec((D_MODEL, TF), col),
                  pl.BlockSpec((D_MODEL, TF), col),
                  pl.BlockSpec((TF, D_MODEL), lambda i, j: (j, 0)),
                  pl.BlockSpec((CONV_F, TF), col),
                  pl.BlockSpec((1, TF), col),
                  pl.BlockSpec((1, D_MODEL), lambda i, j: (0, 0))],
        out_specs=[pl.BlockSpec((tm, D_MODEL), row),
                   pl.BlockSpec((1, SUBLANE, TF), lambda i, j: (i, 0, j))],
        out_shape=[jax.ShapeDtypeStruct((m, D_MODEL), F32),
                   jax.ShapeDtypeStruct((m // tm, SUBLANE, D_FF_PAD), F32)],
        scratch_shapes=[pltpu.VMEM((tm + SUBLANE, TF), F32),
                        pltpu.VMEM((nj, SUBLANE, TF), F32),
                        pltpu.VMEM((tm, D_MODEL), F32),
                        pltpu.SemaphoreType.DMA(())],
        compiler_params=_params("arbitrary", "arbitrary"),
        name="ffn_prompt",
    )(h2, x1, wa, wg, wd, cw, cb, g_final.reshape(1, D_MODEL))


def _ffn_sample_kernel(h_ref, x1_ref, wa_ref, wg_ref, wd_ref, cw_ref, cb_ref, g_ref, hist_ref,
                       y_ref, st_ref, fx_ref):
    nb = hist_ref.shape[0]
    tm = h_ref.shape[0]
    tf = wa_ref.shape[1]
    j = pl.program_id(1)
    live = (j * tf + lax.broadcasted_iota(jnp.int32, (nb, tf), 1)) < D_FF
    for r in range(CONV_F - 1):
        fx_ref[r * nb:(r + 1) * nb, :] = jnp.where(live, hist_ref[:, r, :], 0.0)
    base = (CONV_F - 1) * nb
    p = _ffn_hidden(h_ref[...], wa_ref, wg_ref, cw_ref, cb_ref, fx_ref, base, nb)
    for r in range(CONV_F - 1):
        st_ref[:, r, :] = fx_ref[tm + r * nb:tm + (r + 1) * nb, :]

    @pl.when(j == 0)
    def _():
        y_ref[...] = x1_ref[...]

    y_ref[...] += _dot(p, wd_ref[...])

    @pl.when(j == pl.num_programs(1) - 1)
    def _():
        y_ref[...] = _rms(y_ref[...], g_ref[...])


def _ffn_sample_call(h2, x1, wa, wg, wd, cw, cb, g_final, state, l, nb):
    m = h2.shape[0]
    nj = D_FF_PAD // TF
    row = lambda i, j: (i, 0)
    col = lambda i, j: (0, j)
    return pl.pallas_call(
        _ffn_sample_kernel,
        grid=(1, nj),
        in_specs=[pl.BlockSpec((m, D_MODEL), row, pipeline_mode=pl.Buffered(1)),
                  pl.BlockSpec((m, D_MODEL), row, pipeline_mode=pl.Buffered(1)),
                  pl.BlockSpec((D_MODEL, TF), col),
                  pl.BlockSpec((D_MODEL, TF), col),
                  pl.BlockSpec((TF, D_MODEL), lambda i, j: (j, 0)),
                  pl.BlockSpec((CONV_F, TF), col),
                  pl.BlockSpec((1, TF), col),
                  pl.BlockSpec((1, D_MODEL), lambda i, j: (0, 0)),
                  pl.BlockSpec((None, nb, CONV_F - 1, TF), lambda i, j: (l, 0, 0, j))],
        out_specs=[pl.BlockSpec((m, D_MODEL), row),
                   pl.BlockSpec((None, nb, CONV_F - 1, TF), lambda i, j: (0, 0, 0, j))],
        out_shape=[jax.ShapeDtypeStruct((m, D_MODEL), F32),
                   jax.ShapeDtypeStruct((1, nb, CONV_F - 1, D_FF), F32)],
        scratch_shapes=[pltpu.VMEM(((CONV_F - 1) * nb + m, TF), F32)],
        compiler_params=_params("arbitrary", "arbitrary"),
        name="ffn_sample",
    )(h2, x1, wa, wg, wd, cw, cb, g_final.reshape(1, D_MODEL), state)


def kernel(x_prompt, x_sample, mem_prompt, cache_mem_k, cache_mem_v, state_conv, state_ffn_conv, g_mix, w_in, ln_v_g, ln_v_b, w_s, b_s, w_pa, conv_w, conv_b, ln_b_g, ln_b_b, w_pb, g_mem, w_k, w_v, w_pc, w_o, g_ffn, w_up, ffn_conv_w, ffn_conv_b, w_down, g_final):
    depth = g_mix.shape[0]
    assert depth == 1
    l = 0
    nbp, lp, _ = x_prompt.shape
    nbs, ls, _ = x_sample.shape
    mp = nbp * lp
    pad_ff = D_FF_PAD - D_FF

    w_in_b = w_in[l][:, :OFF_GATES].astype(BF16)
    cw = jnp.pad(ffn_conv_w[l], ((0, 0), (0, pad_ff)))
    cb = jnp.pad(ffn_conv_b[l], (0, pad_ff)).reshape(1, D_FF_PAD)

    reps = CHUNK // ls
    ws_p = w_s[l]
    ws_s = jnp.tile(w_s[l][:, :ls, :ls], (1, reps, reps))
    bs_p = jnp.repeat(b_s[l].T, D_A // G_A, axis=1)
    bs_s = jnp.repeat(jnp.tile(b_s[l][:, :ls], (1, reps)).T, D_A // G_A, axis=1)

    xp = x_prompt.reshape(mp, D_MODEL)
    h, a, bact, q, ctail, w_pa_b, w_pb_b, w_pc_b, w_o_b = _inproj_prompt_call(
        xp, nbp, g_mix[l], w_in_b, ln_v_g[l], ln_v_b[l], ws_p, bs_p, conv_w[l], conv_b[l], ln_b_g[l], ln_b_b[l],
        (w_pa[l], w_pb[l], w_pc[l], w_o[l]))
    k5, v5, kb, vb = _memkv_call(mem_prompt, g_mem[l], w_k[l], w_v[l])
    cact, w_gates = _attn_prompt_call(q, kb, vb, w_in[l])
    mix, w_up_a, w_up_g, w_down_b = _mix_call(h, a, bact, cact, w_pa_b, w_pb_b, w_pc_b, w_gates, "mix_prompt",
                                              (w_up[l], w_down[l]))
    x1, h2 = _oproj_call(xp, mix, w_o_b, g_ffn[l])
    yp, tail = _ffn_prompt_call(h2, x1, w_up_a, w_up_g, w_down_b, cw, cb, g_final, nbp)
    nt = tail.shape[0] // nbp
    ffn_p = tail.reshape(nbp, nt, SUBLANE, D_FF_PAD)[:, nt - 1, SUBLANE - (CONV_F - 1):, :D_FF]
    nt = ctail.shape[0] // nbp
    conv_p = ctail.reshape(nbp, nt, HALO, D_B)[:, nt - 1, HALO - (CONV_B - 1):]

    h, a, q, glu_s, vn_s = _inproj_sample_call(x_sample, g_mix[l], w_in_b, ln_v_g[l], ln_v_b[l], ws_s, bs_s)
    bact, conv_s = _convb_sample_call(jnp.swapaxes(state_conv, 1, 2), l, glu_s, conv_w[l], conv_b[l],
                                      ln_b_g[l], ln_b_b[l])
    conv_s = jnp.swapaxes(conv_s, 1, 2)
    cact = _attn_sample_call(q, cache_mem_k, cache_mem_v, l, ls)
    (mix,) = _mix_call(h, a, bact, cact, w_pa_b, w_pb_b, w_pc_b, w_gates, "mix_sample")
    x1, h2 = _oproj_sample_call(x_sample, mix, w_o_b, g_ffn[l])
    ys, ffn_s = _ffn_sample_call(h2, x1, w_up_a, w_up_g, w_down_b, cw, cb, g_final, state_ffn_conv, l, nbs)

    ys = jnp.swapaxes(ys.reshape(ls, nbs, D_MODEL), 0, 1)
    return (yp.reshape(nbp, lp, D_MODEL), ys, k5, v5, conv_p[None], ffn_p[None],
            conv_s, ffn_s, vn_s)
```

```python
import functools

import jax
import jax.numpy as jnp
from jax import lax
from jax.experimental import pallas as pl
from jax.experimental.pallas import tpu as pltpu

F32 = jnp.float32
BF16 = jnp.bfloat16

D_MODEL = 2048
CHUNK = 128
D_A = D_MODEL // 2
G_A = 8
D_B = D_MODEL // 2
CONV_B = 31
N_MEM = 256
N_XHEADS = 4
XHEAD_DIM = D_MODEL // 8
D_C = N_XHEADS * XHEAD_DIM
D_FF = ((8 * D_MODEL // 3 + 127) // 128) * 128
CONV_F = 3
EPS = 1e-6

OFF_ZA = 0
OFF_ZB = 2 * D_A
OFF_Q = OFF_ZB + 2 * D_B
OFF_GATES = OFF_Q + D_C

LANE = 128
SUBLANE = 8
TM = 512
TF = 512
D_FF_PAD = ((D_FF + TF - 1) // TF) * TF
HALO = 32
VMEM_LIMIT = 60 * 1024 * 1024


def _params(*sem):
    return pltpu.CompilerParams(dimension_semantics=sem, vmem_limit_bytes=VMEM_LIMIT)


def _rms(x, g):
    return x * lax.rsqrt(jnp.mean(x * x, axis=-1, keepdims=True) + EPS) * g


def _layer_norm(x, g, b):
    mu = jnp.mean(x, axis=-1, keepdims=True)
    d = x - mu
    var = jnp.mean(d * d, axis=-1, keepdims=True)
    return d * lax.rsqrt(var + EPS) * g + b


def _dot(a, b):
    return jnp.dot(a, b, preferred_element_type=F32)


def _softmax(s):
    e = jnp.exp(s - jnp.max(s, axis=-1, keepdims=True))
    return e / jnp.sum(e, axis=-1, keepdims=True)


def _spatial_gate(u, vb, ws_ref, bs_ref, a_ref, sample):
    tm = vb.shape[0]
    r = lax.broadcasted_iota(jnp.int32, (CHUNK, CHUNK), 0)
    c = lax.broadcasted_iota(jnp.int32, (CHUNK, CHUNK), 1)
    mask = r >= c
    if sample:
        nl = sample
        mask = mask & ((r // nl) == (c // nl))
        pick_rows = jnp.where(c == r % nl, 1.0, 0.0).astype(BF16)
        pick_cols = jnp.where(r == c % nl, 1.0, 0.0).astype(BF16)
    for g in range(G_A):
        cs = slice(g * LANE, (g + 1) * LANE)
        w = ws_ref[g]
        if sample:
            w = _dot(_dot(pick_rows, w.astype(BF16)).astype(BF16), pick_cols)
        wm = jnp.where(mask, w, 0.0).astype(BF16)
        for ch in range(tm // CHUNK):
            rs = slice(ch * CHUNK, (ch + 1) * CHUNK)
            s = _dot(wm, vb[rs, cs]) + bs_ref[:, cs]
            a_ref[rs, cs] = (u[rs, cs] * s).astype(BF16)


IN_SUB = 256
CONV_ROWS = 128


def _conv_taps(xc_ref, sh_ref, w_ref, bias, r0, cs):
    lead = HALO - (CONV_B - 1)
    acc = jnp.broadcast_to(bias, (CONV_ROWS, bias.shape[1]))
    for s in range(SUBLANE):
        qs = [q for q in range(HALO // SUBLANE + 1) if lead <= SUBLANE * q + s < lead + CONV_B]
        n = CONV_ROWS + SUBLANE * qs[-1]
        if s:
            sh_ref[0:n, :] = xc_ref[pl.ds(r0 + s, n), cs]
        for q in qs:
            k = SUBLANE * q + s - lead
            if s:
                win = sh_ref[SUBLANE * q:SUBLANE * q + CONV_ROWS, :]
            else:
                win = xc_ref[r0 + SUBLANE * q:r0 + SUBLANE * q + CONV_ROWS, cs]
            acc = acc + w_ref[k:k + 1, cs] * win
    return acc


def _inproj_prompt_kernel(x_ref, gm_ref, wa_ref, wb_ref, wq_ref, lvg_ref, lvb_ref, ws_ref, bs_ref,
                          cw_ref, cb_ref, lbg_ref, lbb_ref, wpa_f, wpb_f, wpc_f, wo_f,
                          h_ref, a_ref, b_ref, q_ref, ctail_ref, wpa_o, wpb_o, wpc_o, wo_o,
                          xc_ref, cv_ref, sh_ref, u_ref, v_ref, *, nt):
    tm = x_ref.shape[0]
    t = pl.program_id(0) % nt
    for src, dst in ((wpa_f, wpa_o), (wpb_f, wpb_o), (wpc_f, wpc_o), (wo_f, wo_o)):
        dst[...] = src[...].astype(BF16)

    @pl.when(t == 0)
    def _():
        xc_ref[0:HALO, :] = jnp.zeros((HALO, D_B), F32)

    @pl.when(t != 0)
    def _():
        xc_ref[0:HALO, :] = xc_ref[tm:tm + HALO, :]

    h = _rms(x_ref[...], gm_ref[...]).astype(BF16)
    h_ref[...] = h

    for c in range(D_B // IN_SUB):
        cs = slice(c * IN_SUB, (c + 1) * IN_SUB)
        za = _dot(h, wb_ref[:, cs])
        zb = _dot(h, wb_ref[:, D_B + c * IN_SUB:D_B + (c + 1) * IN_SUB])
        xc_ref[HALO:, cs] = za * jax.nn.sigmoid(zb)
        for rb in range(tm // CONV_ROWS):
            r0 = rb * CONV_ROWS
            cv_ref[r0:r0 + CONV_ROWS, cs] = _conv_taps(xc_ref, sh_ref, cw_ref, cb_ref[:, cs], r0, cs)
    ctail_ref[0] = xc_ref[tm:tm + HALO, :]
    y = _layer_norm(cv_ref[...], lbg_ref[...], lbb_ref[...])
    b_ref[...] = (y * jax.nn.sigmoid(y)).astype(BF16)

    for c in range(D_A // IN_SUB):
        cs = slice(c * IN_SUB, (c + 1) * IN_SUB)
        u_ref[:, cs] = jax.nn.gelu(_dot(h, wa_ref[:, cs]))
        v_ref[:, cs] = jax.nn.gelu(_dot(h, wa_ref[:, D_A + c * IN_SUB:D_A + (c + 1) * IN_SUB]))
    q_ref[...] = _dot(h, wq_ref[...]).astype(BF16)
    vb = _layer_norm(v_ref[...], lvg_ref[...], lvb_ref[...]).astype(BF16)
    _spatial_gate(u_ref, vb, ws_ref, bs_ref, a_ref, 0)


def _inproj_weight_specs(const_map):
    single = pl.Buffered(1)
    return [pl.BlockSpec((D_MODEL, 2 * D_A), lambda *i: (0, OFF_ZA // (2 * D_A)), pipeline_mode=single),
            pl.BlockSpec((D_MODEL, 2 * D_B), lambda *i: (0, OFF_ZB // (2 * D_B)), pipeline_mode=single),
            pl.BlockSpec((D_MODEL, D_C), lambda *i: (0, OFF_Q // D_C), pipeline_mode=single)]


def _inproj_prompt_call(x, nb, g_mix, w_in, ln_v_g, ln_v_b, ws, bs, conv_w, conv_b, ln_b_g, ln_b_b, casts):
    m = x.shape[0]
    tm = TM
    nt = m // nb // tm
    steps = m // tm
    slab = lambda w: pl.BlockSpec((w.shape[0] // steps, w.shape[1]), lambda i: (i, 0))
    const = lambda i: (0, 0)
    row = lambda i: (i, 0)
    vec = lambda n: pl.BlockSpec((1, n), const)
    return pl.pallas_call(
        functools.partial(_inproj_prompt_kernel, nt=nt),
        grid=(m // tm,),
        in_specs=[pl.BlockSpec((tm, D_MODEL), row), vec(D_MODEL)] + _inproj_weight_specs(const) + [
                  vec(D_A), vec(D_A),
                  pl.BlockSpec((G_A, CHUNK, CHUNK), lambda i: (0, 0, 0)),
                  pl.BlockSpec((CHUNK, D_A), const),
                  pl.BlockSpec((CONV_B, D_B), const),
                  vec(D_B), vec(D_B), vec(D_B)] + [slab(w) for w in casts],
        out_specs=[pl.BlockSpec((tm, D_MODEL), row),
                   pl.BlockSpec((tm, D_A), row),
                   pl.BlockSpec((tm, D_B), row),
                   pl.BlockSpec((tm, D_C), row),
                   pl.BlockSpec((1, HALO, D_B), lambda i: (i, 0, 0))] + [slab(w) for w in casts],
        out_shape=[jax.ShapeDtypeStruct((m, D_MODEL), BF16),
                   jax.ShapeDtypeStruct((m, D_A), BF16),
                   jax.ShapeDtypeStruct((m, D_B), BF16),
                   jax.ShapeDtypeStruct((m, D_C), BF16),
                   jax.ShapeDtypeStruct((m // tm, HALO, D_B), F32)]
                  + [jax.ShapeDtypeStruct(w.shape, BF16) for w in casts],
        scratch_shapes=[pltpu.VMEM((tm + HALO, D_B), F32),
                        pltpu.VMEM((tm, D_B), F32),
                        pltpu.VMEM((CONV_ROWS + HALO, IN_SUB), F32),
                        pltpu.VMEM((tm, D_A), F32),
                        pltpu.VMEM((tm, D_A), F32)],
        compiler_params=_params("arbitrary"),
        name="inproj_prompt",
    )(x, g_mix.reshape(1, D_MODEL), w_in, w_in, w_in, ln_v_g.reshape(1, D_A), ln_v_b.reshape(1, D_A), ws, bs,
      conv_w, conv_b.reshape(1, D_B), ln_b_g.reshape(1, D_B), ln_b_b.reshape(1, D_B), *casts)


def _inproj_sample_kernel(x_ref, gm_ref, wa_ref, wb_ref, wq_ref, lvg_ref, lvb_ref, ws_ref, bs_ref,
                          h_ref, a_ref, q_ref, glu_ref, vn_ref):
    nb, nl, _ = x_ref.shape
    m = nb * nl
    h = _rms(x_ref[...].reshape(m, D_MODEL), gm_ref[...]).astype(BF16)
    h_ref[...] = h
    zg = jax.nn.gelu(_dot(h, wa_ref[...]))
    vn = _layer_norm(zg[:, D_A:], lvg_ref[...], lvb_ref[...])
    vn_ref[...] = vn.reshape(nb, nl, D_A)
    _spatial_gate(zg[:, :D_A], vn.astype(BF16), ws_ref, bs_ref, a_ref, nl)
    z = _dot(h, wb_ref[...])
    glu_ref[...] = (z[:, :D_B] * jax.nn.sigmoid(z[:, D_B:])).reshape(nb, nl, D_B)
    q_ref[...] = _dot(h, wq_ref[...])


def _inproj_sample_call(x, g_mix, w_in, ln_v_g, ln_v_b, ws, bs):
    nb, nl, _ = x.shape
    m = nb * nl
    bt = CHUNK // nl
    const = lambda i: (0, 0)
    vec = lambda n: pl.BlockSpec((1, n), const)
    flat = lambda n: pl.BlockSpec((bt * nl, n), lambda i: (i, 0))
    return pl.pallas_call(
        _inproj_sample_kernel,
        grid=(nb // bt,),
        in_specs=[pl.BlockSpec((bt, nl, D_MODEL), lambda i: (i, 0, 0)), vec(D_MODEL)]
                 + _inproj_weight_specs(const) + [
                  vec(D_A), vec(D_A),
                  pl.BlockSpec((G_A, CHUNK, CHUNK), lambda i: (0, 0, 0)),
                  pl.BlockSpec((CHUNK, D_A), const)],
        out_specs=[flat(D_MODEL), flat(D_A), flat(D_C),
                   pl.BlockSpec((bt, nl, D_B), lambda i: (i, 0, 0)),
                   pl.BlockSpec((None, bt, nl, D_A), lambda i: (0, i, 0, 0))],
        out_shape=[jax.ShapeDtypeStruct((m, D_MODEL), BF16),
                   jax.ShapeDtypeStruct((m, D_A), BF16),
                   jax.ShapeDtypeStruct((m, D_C), F32),
                   jax.ShapeDtypeStruct((nb, nl, D_B), F32),
                   jax.ShapeDtypeStruct((1, nb, nl, D_A), F32)],
        compiler_params=_params("arbitrary"),
        name="inproj_sample",
    )(x, g_mix.reshape(1, D_MODEL), w_in, w_in, w_in, ln_v_g.reshape(1, D_A), ln_v_b.reshape(1, D_A), ws, bs)


def _convb_sample_kernel(hist_ref, x_ref, w_ref, cb_ref, lng_ref, lnb_ref, o_ref, nh_ref):
    bt, nl, _ = x_ref.shape
    nh = hist_ref.shape[0]
    new_rows = [x_ref[:, t, :] for t in range(nl)]
    window = lambda j: hist_ref[j] if j < nh else new_rows[j - nh]
    for t in range(nl):
        acc = jnp.broadcast_to(cb_ref[...], (bt, D_B))
        for k in range(CONV_B):
            acc = acc + w_ref[k:k + 1, :] * window(t + k)
        y = _layer_norm(acc, lng_ref[...], lnb_ref[...])
        o_ref[:, t, :] = y * jax.nn.sigmoid(y)
    for j in range(nh):
        nh_ref[j] = window(j + nl)


def _convb_sample_call(hist, l, x, conv_w, conv_b, ln_g, ln_b):
    nb, nl, _ = x.shape
    nh = hist.shape[1]
    bt = 16
    const = lambda i: (0, 0)
    state_spec = pl.BlockSpec((None, nh, bt, D_B), lambda i: (l, 0, i, 0))
    new_spec = pl.BlockSpec((bt, nl, D_B), lambda i: (i, 0, 0))
    return pl.pallas_call(
        _convb_sample_kernel,
        grid=(nb // bt,),
        in_specs=[state_spec, new_spec,
                  pl.BlockSpec((CONV_B, D_B), const),
                  pl.BlockSpec((1, D_B), const),
                  pl.BlockSpec((1, D_B), const),
                  pl.BlockSpec((1, D_B), const)],
        out_specs=[new_spec, pl.BlockSpec((None, nh, bt, D_B), lambda i: (0, 0, i, 0))],
        out_shape=[jax.ShapeDtypeStruct((nb, nl, D_B), F32),
                   jax.ShapeDtypeStruct((1, nh, nb, D_B), F32)],
        compiler_params=_params("parallel"),
        name="convb_sample",
    )(hist, x, conv_w, conv_b.reshape(1, D_B), ln_g.reshape(1, D_B), ln_b.reshape(1, D_B))


def _memkv_kernel(m_ref, g_ref, wk_ref, wv_ref, k5_ref, v5_ref, kb_ref, vb_ref):
    mn = _rms(m_ref[...], g_ref[...]).astype(BF16)
    k = _dot(mn, wk_ref[...].astype(BF16))
    v = _dot(mn, wv_ref[...].astype(BF16))
    k5_ref[...] = k.reshape(N_MEM, N_XHEADS, XHEAD_DIM)
    v5_ref[...] = v.reshape(N_MEM, N_XHEADS, XHEAD_DIM)
    kb_ref[...] = k.astype(BF16)
    vb_ref[...] = v.astype(BF16)


def _memkv_call(mem, g_mem, w_k, w_v):
    nb = mem.shape[0]
    const = lambda i: (0, 0)
    cache_spec = pl.BlockSpec((None, None, N_MEM, N_XHEADS, XHEAD_DIM), lambda i: (0, i, 0, 0, 0))
    seq_spec = pl.BlockSpec((None, N_MEM, D_C), lambda i: (i, 0, 0))
    cache_shape = jax.ShapeDtypeStruct((1, nb, N_MEM, N_XHEADS, XHEAD_DIM), F32)
    return pl.pallas_call(
        _memkv_kernel,
        grid=(nb,),
        in_specs=[pl.BlockSpec((None, N_MEM, D_MODEL), lambda i: (i, 0, 0)),
                  pl.BlockSpec((1, D_MODEL), const),
                  pl.BlockSpec((D_MODEL, D_C), const),
                  pl.BlockSpec((D_MODEL, D_C), const)],
        out_specs=[cache_spec, cache_spec, seq_spec, seq_spec],
        out_shape=[cache_shape, cache_shape,
                   jax.ShapeDtypeStruct((nb, N_MEM, D_C), BF16),
                   jax.ShapeDtypeStruct((nb, N_MEM, D_C), BF16)],
        compiler_params=_params("parallel"),
        name="memory_kv",
    )(mem, g_mem.reshape(1, D_MODEL), w_k, w_v)


GATE_COLS = 1024


def _attn_prompt_kernel(q_ref, k_ref, v_ref, *refs):
    gate_refs, o_ref, gates_o = refs[:-2], refs[-2], refs[-1]
    for n, g_ref in enumerate(gate_refs):
        gates_o[:, n * GATE_COLS:(n + 1) * GATE_COLS] = g_ref[...].astype(BF16)
    for h in range(N_XHEADS):
        cs = slice(h * XHEAD_DIM, (h + 1) * XHEAD_DIM)
        s = lax.dot_general(q_ref[:, cs], k_ref[:, cs], (((1,), (1,)), ((), ())), preferred_element_type=F32)
        p = _softmax(s * (XHEAD_DIM ** -0.5)).astype(BF16)
        o_ref[:, cs] = _dot(p, v_ref[:, cs]).astype(BF16)


def _attn_prompt_call(q, k, v, w_in):
    m = q.shape[0]
    nb = k.shape[0]
    tm = TM
    nt = m // nb // tm
    n_gate = 3 * D_MODEL
    slab = D_MODEL // (nb * nt)
    assert OFF_GATES % GATE_COLS == 0 and n_gate % GATE_COLS == 0 and D_MODEL % (nb * nt) == 0
    gate_specs = [pl.BlockSpec((slab, GATE_COLS), lambda b, t, c=c: (b * nt + t, OFF_GATES // GATE_COLS + c))
                  for c in range(n_gate // GATE_COLS)]
    return pl.pallas_call(
        _attn_prompt_kernel,
        grid=(nb, nt),
        in_specs=[pl.BlockSpec((tm, D_C), lambda b, t: (b * nt + t, 0)),
                  pl.BlockSpec((None, N_MEM, D_C), lambda b, t: (b, 0, 0)),
                  pl.BlockSpec((None, N_MEM, D_C), lambda b, t: (b, 0, 0))] + gate_specs,
        out_specs=[pl.BlockSpec((tm, D_C), lambda b, t: (b * nt + t, 0)),
                   pl.BlockSpec((slab, n_gate), lambda b, t: (b * nt + t, 0))],
        out_shape=[jax.ShapeDtypeStruct((m, D_C), BF16),
                   jax.ShapeDtypeStruct((D_MODEL, n_gate), BF16)],
        compiler_params=_params("arbitrary", "arbitrary"),
        name="attn_prompt",
    )(q, k, v, *([w_in] * len(gate_specs)))


def _attn_sample_kernel(q_ref, k_ref, v_ref, o_ref):
    bt = k_ref.shape[0]
    nl = q_ref.shape[0] // bt
    rows = lax.broadcasted_iota(jnp.int32, (N_XHEADS * nl, N_MEM * N_XHEADS), 0)
    cols = lax.broadcasted_iota(jnp.int32, (N_XHEADS * nl, N_MEM * N_XHEADS), 1)
    same_head = (cols % N_XHEADS) == (rows // nl)
    outs = []
    for b in range(bt):
        q = q_ref[b * nl:(b + 1) * nl, :]
        qs = jnp.concatenate([q[:, h * XHEAD_DIM:(h + 1) * XHEAD_DIM] for h in range(N_XHEADS)], axis=0)
        k2 = k_ref[b].reshape(N_MEM * N_XHEADS, XHEAD_DIM).astype(BF16)
        v2 = v_ref[b].reshape(N_MEM * N_XHEADS, XHEAD_DIM).astype(BF16)
        s = lax.dot_general(qs.astype(BF16), k2, (((1,), (1,)), ((), ())), preferred_element_type=F32)
        p = _softmax(jnp.where(same_head, s * (XHEAD_DIM ** -0.5), -jnp.inf)).astype(BF16)
        o = _dot(p, v2)
        outs.append(jnp.concatenate([o[h * nl:(h + 1) * nl, :] for h in range(N_XHEADS)], axis=1))
    o_ref[...] = jnp.concatenate(outs, axis=0).astype(BF16)


def _attn_sample_call(q, k, v, l, nl):
    m = q.shape[0]
    bt = 8
    cache_spec = pl.BlockSpec((None, bt, N_MEM, N_XHEADS, XHEAD_DIM), lambda i: (l, i, 0, 0, 0))
    return pl.pallas_call(
        _attn_sample_kernel,
        grid=(m // (bt * nl),),
        in_specs=[pl.BlockSpec((bt * nl, D_C), lambda i: (i, 0)), cache_spec, cache_spec],
        out_specs=pl.BlockSpec((bt * nl, D_C), lambda i: (i, 0)),
        out_shape=jax.ShapeDtypeStruct((m, D_C), BF16),
        compiler_params=_params("parallel"),
        name="attn_sample",
    )(q, k, v)


def _cast_ffn_weights(wu_f, wd_f, wa_o, wg_o, wd_o, step):
    wu = wu_f[...]
    zpad = jnp.zeros((wu.shape[0], D_FF_PAD - D_FF), BF16)
    wa_o[:, :D_FF] = wu[:, :D_FF].astype(BF16)
    wa_o[:, D_FF:] = zpad
    wg_o[:, :D_FF] = wu[:, D_FF:].astype(BF16)
    wg_o[:, D_FF:] = zpad
    live_blocks = D_FF // wd_f.shape[0]

    @pl.when(step < live_blocks)
    def _():
        wd_o[...] = wd_f[...].astype(BF16)

    @pl.when(step >= live_blocks)
    def _():
        wd_o[...] = jnp.zeros(wd_o.shape, BF16)


def _mix_kernel(*refs, cast_ffn):
    h_ref, a_ref, b_ref, c_ref, wpa_ref, wpb_ref, wpc_ref, wga_ref, wgb_ref, wgc_ref = refs[:10]
    if cast_ffn:
        wu_f, wd_f, o_ref, wa_o, wg_o, wd_o = refs[10:]
        _cast_ffn_weights(wu_f, wd_f, wa_o, wg_o, wd_o, pl.program_id(0) * pl.num_programs(1) + pl.program_id(1))
    else:
        (o_ref,) = refs[10:]
    h = h_ref[...]
    b = b_ref[...].reshape(h.shape[0], D_B).astype(BF16)
    mix = jax.nn.sigmoid(_dot(h, wga_ref[...])) * _dot(a_ref[...], wpa_ref[...])
    mix = mix + jax.nn.sigmoid(_dot(h, wgb_ref[...])) * _dot(b, wpb_ref[...])
    mix = mix + jax.nn.sigmoid(_dot(h, wgc_ref[...])) * _dot(c_ref[...], wpc_ref[...])
    o_ref[...] = mix.astype(BF16)


def _mix_call(h, a, b, c, w_pa, w_pb, w_pc, w_gates, name, ffn_weights=None):
    m = h.shape[0]
    tm = min(TM, m)
    tn = 512
    nj = D_MODEL // tn
    per_gate = D_MODEL // tn
    row = lambda i, j: (i, 0)
    col = lambda i, j: (0, j)
    if b.ndim == 3:
        assert m == tm
        b_spec = pl.BlockSpec(b.shape, lambda i, j: (0, 0, 0))
    else:
        b_spec = pl.BlockSpec((tm, D_B), row)
    in_specs = [pl.BlockSpec((tm, D_MODEL), row),
                pl.BlockSpec((tm, D_A), row),
                b_spec,
                pl.BlockSpec((tm, D_C), row),
                pl.BlockSpec((D_A, tn), col),
                pl.BlockSpec((D_B, tn), col),
                pl.BlockSpec((D_C, tn), col),
                pl.BlockSpec((D_MODEL, tn), lambda i, j: (0, j)),
                pl.BlockSpec((D_MODEL, tn), lambda i, j: (0, per_gate + j)),
                pl.BlockSpec((D_MODEL, tn), lambda i, j: (0, 2 * per_gate + j))]
    out_specs = [pl.BlockSpec((tm, tn), lambda i, j: (i, j))]
    out_shape = [jax.ShapeDtypeStruct((m, D_MODEL), BF16)]
    args = [h, a, b, c, w_pa, w_pb, w_pc, w_gates, w_gates, w_gates]
    if ffn_weights is not None:
        w_up, w_down = ffn_weights
        steps = (m // tm) * nj
        up_rows = D_MODEL // steps
        down_rows = CHUNK
        down_blocks = D_FF_PAD // down_rows
        assert D_MODEL % steps == 0 and D_FF % down_rows == 0 and down_blocks <= steps
        step = lambda i, j: i * nj + j
        in_specs += [pl.BlockSpec((up_rows, 2 * D_FF), lambda i, j: (step(i, j), 0)),
                     pl.BlockSpec((down_rows, D_MODEL),
                                  lambda i, j: (jnp.minimum(step(i, j), D_FF // down_rows - 1), 0))]
        out_specs += [pl.BlockSpec((up_rows, D_FF_PAD), lambda i, j: (step(i, j), 0)),
                      pl.BlockSpec((up_rows, D_FF_PAD), lambda i, j: (step(i, j), 0)),
                      pl.BlockSpec((down_rows, D_MODEL),
                                   lambda i, j: (jnp.minimum(step(i, j), down_blocks - 1), 0))]
        out_shape += [jax.ShapeDtypeStruct((D_MODEL, D_FF_PAD), BF16),
                      jax.ShapeDtypeStruct((D_MODEL, D_FF_PAD), BF16),
                      jax.ShapeDtypeStruct((D_FF_PAD, D_MODEL), BF16)]
        args += [w_up, w_down]
    return pl.pallas_call(
        functools.partial(_mix_kernel, cast_ffn=ffn_weights is not None),
        grid=(m // tm, nj),
        in_specs=in_specs,
        out_specs=out_specs,
        out_shape=out_shape,
        compiler_params=_params("arbitrary", "arbitrary"),
        name=name,
    )(*args)


def _oproj_kernel(x_ref, mix_ref, wo_ref, g_ref, x1_ref, h2_ref):
    x1 = x_ref[...] + _dot(mix_ref[...], wo_ref[...])
    x1_ref[...] = x1
    h2_ref[...] = _rms(x1, g_ref[...]).astype(BF16)


def _oproj_call(x, mix, w_o, g_ffn):
    m = mix.shape[0]
    tm = TM
    row = lambda i: (i, 0)
    return pl.pallas_call(
        _oproj_kernel,
        grid=(m // tm,),
        in_specs=[pl.BlockSpec((tm, D_MODEL), row),
                  pl.BlockSpec((tm, D_MODEL), row),
                  pl.BlockSpec((D_MODEL, D_MODEL), lambda i: (0, 0)),
                  pl.BlockSpec((1, D_MODEL), lambda i: (0, 0))],
        out_specs=[pl.BlockSpec((tm, D_MODEL), row),
                   pl.BlockSpec((tm, D_MODEL), row)],
        out_shape=[jax.ShapeDtypeStruct((m, D_MODEL), F32),
                   jax.ShapeDtypeStruct((m, D_MODEL), BF16)],
        compiler_params=_params("parallel"),
        name="oproj_prompt",
    )(x, mix, w_o, g_ffn.reshape(1, D_MODEL))


def _oproj_sample_kernel(x_ref, mix_ref, wo_ref, g_ref, x1_ref, h2_ref, x3_ref):
    nb, nl, _ = x_ref.shape
    x3_ref[...] = x_ref[...] + _dot(mix_ref[...], wo_ref[...]).reshape(nb, nl, D_MODEL)
    x1 = jnp.concatenate([x3_ref[:, t, :] for t in range(nl)], axis=0)
    x1_ref[...] = x1
    h2_ref[...] = _rms(x1, g_ref[...]).astype(BF16)


def _oproj_sample_call(x, mix, w_o, g_ffn):
    nb, nl, _ = x.shape
    m = nb * nl
    const = lambda i: (0, 0)
    single = pl.Buffered(1)
    return pl.pallas_call(
        _oproj_sample_kernel,
        grid=(1,),
        in_specs=[pl.BlockSpec((nb, nl, D_MODEL), lambda i: (0, 0, 0), pipeline_mode=single),
                  pl.BlockSpec((m, D_MODEL), const, pipeline_mode=single),
                  pl.BlockSpec((D_MODEL, D_MODEL), const, pipeline_mode=single),
                  pl.BlockSpec((1, D_MODEL), const)],
        out_specs=[pl.BlockSpec((m, D_MODEL), const),
                   pl.BlockSpec((m, D_MODEL), const)],
        out_shape=[jax.ShapeDtypeStruct((m, D_MODEL), F32),
                   jax.ShapeDtypeStruct((m, D_MODEL), BF16)],
        scratch_shapes=[pltpu.VMEM((nb, nl, D_MODEL), F32)],
        compiler_params=_params("arbitrary"),
        name="oproj_sample",
    )(x, mix, w_o, g_ffn.reshape(1, D_MODEL))


FFN_TM = 1024
FFN_SUB = 256


def _ffn_hidden(h, wa_ref, wg_ref, cw_ref, cb_ref, fx_ref, base, lag):
    tm = h.shape[0]
    ps = []
    for c in range(wa_ref.shape[1] // FFN_SUB):
        cs = slice(c * FFN_SUB, (c + 1) * FFN_SUB)
        fa = _dot(h, wa_ref[:, cs])
        fg = _dot(h, wg_ref[:, cs])
        fx_ref[base:base + tm, cs] = fa
        prev1 = fx_ref[pl.ds(base - lag, tm), cs]
        prev2 = fx_ref[pl.ds(base - 2 * lag, tm), cs]
        fc = cw_ref[2:3, cs] * fa + cw_ref[1:2, cs] * prev1 + cw_ref[0:1, cs] * prev2 + cb_ref[:, cs]
        ps.append((jax.nn.gelu(fc) * fg).astype(BF16))
    return jnp.concatenate(ps, axis=1)


def _ffn_prompt_kernel(h_ref, x1_hbm, wa_ref, wg_ref, wd_ref, cw_ref, cb_ref, g_ref, y_ref, tail_ref,
                       fx_ref, carry_ref, x1_ref, x1_sem, *, nt):
    tm = h_ref.shape[0]
    tf = wa_ref.shape[1]
    i = pl.program_id(0)
    j = pl.program_id(1)
    x1_copy = pltpu.make_async_copy(x1_hbm.at[pl.ds(i * tm, tm), :], x1_ref, x1_sem)

    @pl.when(j == 0)
    def _():
        x1_copy.start()
        y_ref[...] = jnp.zeros(y_ref.shape, F32)

    @pl.when(i % nt == 0)
    def _():
        fx_ref[0:SUBLANE, :] = jnp.zeros((SUBLANE, tf), F32)

    @pl.when(i % nt != 0)
    def _():
        fx_ref[0:SUBLANE, :] = carry_ref[j]

    p = _ffn_hidden(h_ref[...], wa_ref, wg_ref, cw_ref, cb_ref, fx_ref, SUBLANE, 1)
    last = fx_ref[tm:tm + SUBLANE, :]
    carry_ref[j] = last
    tail_ref[0] = last
    y_ref[...] += _dot(p, wd_ref[...])

    @pl.when(j == pl.num_programs(1) - 1)
    def _():
        x1_copy.wait()

        def norm_rows(r, carry):
            rs = pl.ds(pl.multiple_of(r * CHUNK, CHUNK), CHUNK)
            y_ref[rs, :] = _rms(x1_ref[rs, :] + y_ref[rs, :], g_ref[...])
            return carry

        lax.fori_loop(0, tm // CHUNK, norm_rows, 0)


def _ffn_prompt_call(h2, x1, wa, wg, wd, cw, cb, g_final, nb):
    m = h2.shape[0]
    tm = FFN_TM
    nt = m // nb // tm
    nj = D_FF_PAD // TF
    row = lambda i, j: (i, 0)
    col = lambda i, j: (0, j)
    return pl.pallas_call(
        functools.partial(_ffn_prompt_kernel, nt=nt),
        grid=(m // tm, nj),
        in_specs=[pl.BlockSpec((tm, D_MODEL), row),
                  pl.BlockSpec(memory_space=pl.ANY),
                  pl.BlockSpec((D_MODEL, TF), col),
                  pl.BlockSpec((D_MODEL, TF), col),
                  pl.BlockSpec((TF, D_MODEL), lambda i, j: (j, 0)),
                  pl.BlockSpec((CONV_F, TF), col),
                  pl.BlockSpec((1, TF), col),
                  pl.BlockSpec((1, D_MODEL), lambda i, j: (0, 0))],
        out_specs=[pl.BlockSpec((tm, D_MODEL), row),
                   pl.BlockSpec((1, SUBLANE, TF), lambda i, j: (i, 0, j))],
        out_shape=[jax.ShapeDtypeStruct((m, D_MODEL), F32),
                   jax.ShapeDtypeStruct((m // tm, SUBLANE, D_FF_PAD), F32)],
        scratch_shapes=[pltpu.VMEM((tm + SUBLANE, TF), F32),
                        pltpu.VMEM((nj, SUBLANE, TF), F32),
                        pltpu.VMEM((tm, D_MODEL), F32),
                        pltpu.SemaphoreType.DMA(())],
        compiler_params=_params("arbitrary", "arbitrary"),
        name="ffn_prompt",
    )(h2, x1, wa, wg, wd, cw, cb, g_final.reshape(1, D_MODEL))


def _ffn_sample_kernel(h_ref, x1_ref, wa_ref, wg_ref, wd_ref, cw_ref, cb_ref, g_ref, hist_ref,
                       y_ref, st_ref, fx_ref, acc_ref):
    nb = hist_ref.shape[0]
    tm = h_ref.shape[0]
    tf = wa_ref.shape[1]
    j = pl.program_id(1)
    live = (j * tf + lax.broadcasted_iota(jnp.int32, (nb, tf), 1)) < D_FF
    for r in range(CONV_F - 1):
        fx_ref[r * nb:(r + 1) * nb, :] = jnp.where(live, hist_ref[:, r, :], 0.0)
    base = (CONV_F - 1) * nb
    p = _ffn_hidden(h_ref[...], wa_ref, wg_ref, cw_ref, cb_ref, fx_ref, base, nb)
    for r in range(CONV_F - 1):
        st_ref[:, r, :] = fx_ref[tm + r * nb:tm + (r + 1) * nb, :]

    @pl.when(j == 0)
    def _():
        acc_ref[...] = x1_ref[...]

    acc_ref[...] += _dot(p, wd_ref[...])

    @pl.when(j == pl.num_programs(1) - 1)
    def _():
        for t in range(tm // nb):
            y_ref[:, t, :] = _rms(acc_ref[t * nb:(t + 1) * nb, :], g_ref[...])


def _ffn_sample_call(h2, x1, wa, wg, wd, cw, cb, g_final, state, l, nb):
    m = h2.shape[0]
    nj = D_FF_PAD // TF
    row = lambda i, j: (i, 0)
    col = lambda i, j: (0, j)
    return pl.pallas_call(
        _ffn_sample_kernel,
        grid=(1, nj),
        in_specs=[pl.BlockSpec((m, D_MODEL), row, pipeline_mode=pl.Buffered(1)),
                  pl.BlockSpec((m, D_MODEL), row, pipeline_mode=pl.Buffered(1)),
                  pl.BlockSpec((D_MODEL, TF), col),
                  pl.BlockSpec((D_MODEL, TF), col),
                  pl.BlockSpec((TF, D_MODEL), lambda i, j: (j, 0)),
                  pl.BlockSpec((CONV_F, TF), col),
                  pl.BlockSpec((1, TF), col),
                  pl.BlockSpec((1, D_MODEL), lambda i, j: (0, 0)),
                  pl.BlockSpec((None, nb, CONV_F - 1, TF), lambda i, j: (l, 0, 0, j))],
        out_specs=[pl.BlockSpec((nb, m // nb, D_MODEL), lambda i, j: (0, 0, 0)),
                   pl.BlockSpec((None, nb, CONV_F - 1, TF), lambda i, j: (0, 0, 0, j))],
        out_shape=[jax.ShapeDtypeStruct((nb, m // nb, D_MODEL), F32),
                   jax.ShapeDtypeStruct((1, nb, CONV_F - 1, D_FF), F32)],
        scratch_shapes=[pltpu.VMEM(((CONV_F - 1) * nb + m, TF), F32),
                        pltpu.VMEM((m, D_MODEL), F32)],
        compiler_params=_params("arbitrary", "arbitrary"),
        name="ffn_sample",
    )(h2, x1, wa, wg, wd, cw, cb, g_final.reshape(1, D_MODEL), state)


def kernel(x_prompt, x_sample, mem_prompt, cache_mem_k, cache_mem_v, state_conv, state_ffn_conv, g_mix, w_in, ln_v_g, ln_v_b, w_s, b_s, w_pa, conv_w, conv_b, ln_b_g, ln_b_b, w_pb, g_mem, w_k, w_v, w_pc, w_o, g_ffn, w_up, ffn_conv_w, ffn_conv_b, w_down, g_final):
    depth = g_mix.shape[0]
    assert depth == 1
    l = 0
    nbp, lp, _ = x_prompt.shape
    nbs, ls, _ = x_sample.shape
    mp = nbp * lp
    pad_ff = D_FF_PAD - D_FF

    w_in_b = w_in[l][:, :OFF_GATES].astype(BF16)
    cw = jnp.pad(ffn_conv_w[l], ((0, 0), (0, pad_ff)))
    cb = jnp.pad(ffn_conv_b[l], (0, pad_ff)).reshape(1, D_FF_PAD)

    reps = CHUNK // ls
    ws_p = w_s[l]
    bs_p = jnp.repeat(b_s[l].T, D_A // G_A, axis=1)
    bs_s = jnp.repeat(jnp.tile(b_s[l][:, :ls], (1, reps)).T, D_A // G_A, axis=1)

    xp = x_prompt.reshape(mp, D_MODEL)
    h, a, bact, q, ctail, w_pa_b, w_pb_b, w_pc_b, w_o_b = _inproj_prompt_call(
        xp, nbp, g_mix[l], w_in_b, ln_v_g[l], ln_v_b[l], ws_p, bs_p, conv_w[l], conv_b[l], ln_b_g[l], ln_b_b[l],
        (w_pa[l], w_pb[l], w_pc[l], w_o[l]))
    k5, v5, kb, vb = _memkv_call(mem_prompt, g_mem[l], w_k[l], w_v[l])
    cact, w_gates = _attn_prompt_call(q, kb, vb, w_in[l])
    mix, w_up_a, w_up_g, w_down_b = _mix_call(h, a, bact, cact, w_pa_b, w_pb_b, w_pc_b, w_gates, "mix_prompt",
                                              (w_up[l], w_down[l]))
    x1, h2 = _oproj_call(xp, mix, w_o_b, g_ffn[l])
    yp, tail = _ffn_prompt_call(h2, x1, w_up_a, w_up_g, w_down_b, cw, cb, g_final, nbp)
    nt = tail.shape[0] // nbp
    ffn_p = tail.reshape(nbp, nt, SUBLANE, D_FF_PAD)[:, nt - 1, SUBLANE - (CONV_F - 1):, :D_FF]
    nt = ctail.shape[0] // nbp
    conv_p = ctail.reshape(nbp, nt, HALO, D_B)[:, nt - 1, HALO - (CONV_B - 1):]

    h, a, q, glu_s, vn_s = _inproj_sample_call(x_sample, g_mix[l], w_in_b, ln_v_g[l], ln_v_b[l], ws_p, bs_s)
    bact, conv_s = _convb_sample_call(jnp.swapaxes(state_conv, 1, 2), l, glu_s, conv_w[l], conv_b[l],
                                      ln_b_g[l], ln_b_b[l])
    conv_s = jnp.swapaxes(conv_s, 1, 2)
    cact = _attn_sample_call(q, cache_mem_k, cache_mem_v, l, ls)
    (mix,) = _mix_call(h, a, bact, cact, w_pa_b, w_pb_b, w_pc_b, w_gates, "mix_sample")
    x1, h2 = _oproj_sample_call(x_sample, mix, w_o_b, g_ffn[l])
    ys, ffn_s = _ffn_sample_call(h2, x1, w_up_a, w_up_g, w_down_b, cw, cb, g_final, state_ffn_conv, l, nbs)

    return (yp.reshape(nbp, lp, D_MODEL), ys, k5, v5, conv_p[None], ffn_p[None],
            conv_s, ffn_s, vn_s)
```

```python
import functools

import jax
import jax.numpy as jnp
from jax import lax
from jax.experimental import pallas as pl
from jax.experimental.pallas import tpu as pltpu

F32 = jnp.float32
BF16 = jnp.bfloat16

D_MODEL = 2048
CHUNK = 128
D_A = D_MODEL // 2
G_A = 8
D_B = D_MODEL // 2
CONV_B = 31
N_MEM = 256
N_XHEADS = 4
XHEAD_DIM = D_MODEL // 8
D_C = N_XHEADS * XHEAD_DIM
D_FF = ((8 * D_MODEL // 3 + 127) // 128) * 128
CONV_F = 3
EPS = 1e-6

OFF_ZA = 0
OFF_ZB = 2 * D_A
OFF_Q = OFF_ZB + 2 * D_B
OFF_GATES = OFF_Q + D_C

LANE = 128
SUBLANE = 8
TM = 512
TF = 512
D_FF_PAD = ((D_FF + TF - 1) // TF) * TF
HALO = 32
MIX_TN = 512
ATTN_BT = 8
CONV_BT = 16
VMEM_LIMIT = 60 * 1024 * 1024


def _params(*sem):
    return pltpu.CompilerParams(dimension_semantics=sem, vmem_limit_bytes=VMEM_LIMIT)


def _rms(x, g):
    return x * lax.rsqrt(jnp.mean(x * x, axis=-1, keepdims=True) + EPS) * g


def _layer_norm(x, g, b):
    mu = jnp.mean(x, axis=-1, keepdims=True)
    d = x - mu
    var = jnp.mean(d * d, axis=-1, keepdims=True)
    return d * lax.rsqrt(var + EPS) * g + b


def _dot(a, b):
    return jnp.dot(a, b, preferred_element_type=F32)


def _softmax(s):
    e = jnp.exp(s - jnp.max(s, axis=-1, keepdims=True))
    return e / jnp.sum(e, axis=-1, keepdims=True)


def _spatial_gate(u, vb, ws_ref, bs_ref, a_ref, sample):
    tm = vb.shape[0]
    r = lax.broadcasted_iota(jnp.int32, (CHUNK, CHUNK), 0)
    c = lax.broadcasted_iota(jnp.int32, (CHUNK, CHUNK), 1)
    mask = r >= c
    if sample:
        nl = sample
        mask = mask & ((r // nl) == (c // nl))
        pick_rows = jnp.where(c == r % nl, 1.0, 0.0).astype(BF16)
        pick_cols = jnp.where(r == c % nl, 1.0, 0.0).astype(BF16)
    for g in range(G_A):
        cs = slice(g * LANE, (g + 1) * LANE)
        w = ws_ref[g]
        if sample:
            w = _dot(_dot(pick_rows, w.astype(BF16)).astype(BF16), pick_cols)
        wm = jnp.where(mask, w, 0.0).astype(BF16)
        for ch in range(tm // CHUNK):
            rs = slice(ch * CHUNK, (ch + 1) * CHUNK)
            s = _dot(wm, vb[rs, cs]) + bs_ref[:, cs]
            a_ref[rs, cs] = (u[rs, cs] * s).astype(BF16)


IN_SUB = 256
CONV_ROWS = 128


def _conv_taps(xc_ref, sh_ref, w_ref, bias, r0, cs):
    lead = HALO - (CONV_B - 1)
    acc = jnp.broadcast_to(bias, (CONV_ROWS, bias.shape[1]))
    for s in range(SUBLANE):
        qs = [q for q in range(HALO // SUBLANE + 1) if lead <= SUBLANE * q + s < lead + CONV_B]
        n = CONV_ROWS + SUBLANE * qs[-1]
        if s:
            sh_ref[0:n, :] = xc_ref[pl.ds(r0 + s, n), cs]
        for q in qs:
            k = SUBLANE * q + s - lead
            if s:
                win = sh_ref[SUBLANE * q:SUBLANE * q + CONV_ROWS, :]
            else:
                win = xc_ref[r0 + SUBLANE * q:r0 + SUBLANE * q + CONV_ROWS, cs]
            acc = acc + w_ref[k:k + 1, cs] * win
    return acc


def _inproj_prompt_kernel(x_ref, gm_ref, wa_ref, wb_ref, wq_ref, lvg_ref, lvb_ref, ws_ref, bs_ref,
                          cw_ref, cb_ref, lbg_ref, lbb_ref, wpa_f, wpb_f, wpc_f, wo_f,
                          h_ref, a_ref, b_ref, q_ref, ctail_ref, wpa_o, wpb_o, wpc_o, wo_o,
                          xc_ref, cv_ref, sh_ref, u_ref, v_ref, *, nt):
    tm = x_ref.shape[0]
    t = pl.program_id(0) % nt
    for src, dst in ((wpa_f, wpa_o), (wpb_f, wpb_o), (wpc_f, wpc_o), (wo_f, wo_o)):
        dst[...] = src[...].astype(BF16)

    @pl.when(t == 0)
    def _():
        xc_ref[0:HALO, :] = jnp.zeros((HALO, D_B), F32)

    @pl.when(t != 0)
    def _():
        xc_ref[0:HALO, :] = xc_ref[tm:tm + HALO, :]

    h = _rms(x_ref[...], gm_ref[...]).astype(BF16)
    h_ref[...] = h

    for c in range(D_B // IN_SUB):
        cs = slice(c * IN_SUB, (c + 1) * IN_SUB)
        za = _dot(h, wb_ref[:, cs])
        zb = _dot(h, wb_ref[:, D_B + c * IN_SUB:D_B + (c + 1) * IN_SUB])
        xc_ref[HALO:, cs] = za * jax.nn.sigmoid(zb)
        for rb in range(tm // CONV_ROWS):
            r0 = rb * CONV_ROWS
            cv_ref[r0:r0 + CONV_ROWS, cs] = _conv_taps(xc_ref, sh_ref, cw_ref, cb_ref[:, cs], r0, cs)
    ctail_ref[0] = xc_ref[tm:tm + HALO, :]
    y = _layer_norm(cv_ref[...], lbg_ref[...], lbb_ref[...])
    b_ref[...] = (y * jax.nn.sigmoid(y)).astype(BF16)

    for c in range(D_A // IN_SUB):
        cs = slice(c * IN_SUB, (c + 1) * IN_SUB)
        u_ref[:, cs] = jax.nn.gelu(_dot(h, wa_ref[:, cs]))
        v_ref[:, cs] = jax.nn.gelu(_dot(h, wa_ref[:, D_A + c * IN_SUB:D_A + (c + 1) * IN_SUB]))
    q_ref[...] = _dot(h, wq_ref[...]).astype(BF16)
    vb = _layer_norm(v_ref[...], lvg_ref[...], lvb_ref[...]).astype(BF16)
    _spatial_gate(u_ref, vb, ws_ref, bs_ref, a_ref, 0)


def _inproj_weight_specs(const_map):
    single = pl.Buffered(1)
    return [pl.BlockSpec((D_MODEL, 2 * D_A), lambda *i: (0, OFF_ZA // (2 * D_A)), pipeline_mode=single),
            pl.BlockSpec((D_MODEL, 2 * D_B), lambda *i: (0, OFF_ZB // (2 * D_B)), pipeline_mode=single),
            pl.BlockSpec((D_MODEL, D_C), lambda *i: (0, OFF_Q // D_C), pipeline_mode=single)]


def _inproj_prompt_call(x, nb, g_mix, w_in, ln_v_g, ln_v_b, ws, bs, conv_w, conv_b, ln_b_g, ln_b_b, casts):
    m = x.shape[0]
    tm = TM
    nt = m // nb // tm
    steps = m // tm
    slab = lambda w: pl.BlockSpec((w.shape[0] // steps, w.shape[1]), lambda i: (i, 0))
    const = lambda i: (0, 0)
    row = lambda i: (i, 0)
    vec = lambda n: pl.BlockSpec((1, n), const)
    return pl.pallas_call(
        functools.partial(_inproj_prompt_kernel, nt=nt),
        grid=(m // tm,),
        in_specs=[pl.BlockSpec((tm, D_MODEL), row), vec(D_MODEL)] + _inproj_weight_specs(const) + [
                  vec(D_A), vec(D_A),
                  pl.BlockSpec((G_A, CHUNK, CHUNK), lambda i: (0, 0, 0)),
                  pl.BlockSpec((CHUNK, D_A), const),
                  pl.BlockSpec((CONV_B, D_B), const),
                  vec(D_B), vec(D_B), vec(D_B)] + [slab(w) for w in casts],
        out_specs=[pl.BlockSpec((tm, D_MODEL), row),
                   pl.BlockSpec((tm, D_A), row),
                   pl.BlockSpec((tm, D_B), row),
                   pl.BlockSpec((tm, D_C), row),
                   pl.BlockSpec((1, HALO, D_B), lambda i: (i, 0, 0))] + [slab(w) for w in casts],
        out_shape=[jax.ShapeDtypeStruct((m, D_MODEL), BF16),
                   jax.ShapeDtypeStruct((m, D_A), BF16),
                   jax.ShapeDtypeStruct((m, D_B), BF16),
                   jax.ShapeDtypeStruct((m, D_C), BF16),
                   jax.ShapeDtypeStruct((m // tm, HALO, D_B), F32)]
                  + [jax.ShapeDtypeStruct(w.shape, BF16) for w in casts],
        scratch_shapes=[pltpu.VMEM((tm + HALO, D_B), F32),
                        pltpu.VMEM((tm, D_B), F32),
                        pltpu.VMEM((CONV_ROWS + HALO, IN_SUB), F32),
                        pltpu.VMEM((tm, D_A), F32),
                        pltpu.VMEM((tm, D_A), F32)],
        compiler_params=_params("arbitrary"),
        name="inproj_prompt",
    )(x, g_mix.reshape(1, D_MODEL), w_in, w_in, w_in, ln_v_g.reshape(1, D_A), ln_v_b.reshape(1, D_A), ws, bs,
      conv_w, conv_b.reshape(1, D_B), ln_b_g.reshape(1, D_B), ln_b_b.reshape(1, D_B), *casts)


def _inproj_sample_kernel(x_ref, gm_ref, wa_ref, wb_ref, wq_ref, lvg_ref, lvb_ref, ws_ref, bs_ref,
                          h_ref, a_ref, q_ref, glu_ref, vn_ref):
    nb, nl, _ = x_ref.shape
    m = nb * nl
    h = _rms(x_ref[...].reshape(m, D_MODEL), gm_ref[...]).astype(BF16)
    h_ref[...] = h
    zg = jax.nn.gelu(_dot(h, wa_ref[...]))
    vn = _layer_norm(zg[:, D_A:], lvg_ref[...], lvb_ref[...])
    vn_ref[...] = vn.reshape(nb, nl, D_A)
    _spatial_gate(zg[:, :D_A], vn.astype(BF16), ws_ref, bs_ref, a_ref, nl)
    z = _dot(h, wb_ref[...])
    glu_ref[...] = (z[:, :D_B] * jax.nn.sigmoid(z[:, D_B:])).reshape(nb, nl, D_B)
    q_ref[...] = _dot(h, wq_ref[...])


def _inproj_sample_call(x, g_mix, w_in, ln_v_g, ln_v_b, ws, bs):
    nb, nl, _ = x.shape
    m = nb * nl
    bt = 2 * CHUNK // nl
    const = lambda i: (0, 0)
    vec = lambda n: pl.BlockSpec((1, n), const)
    flat = lambda n: pl.BlockSpec((bt * nl, n), lambda i: (i, 0))
    return pl.pallas_call(
        _inproj_sample_kernel,
        grid=(nb // bt,),
        in_specs=[pl.BlockSpec((bt, nl, D_MODEL), lambda i: (i, 0, 0)), vec(D_MODEL)]
                 + _inproj_weight_specs(const) + [
                  vec(D_A), vec(D_A),
                  pl.BlockSpec((G_A, CHUNK, CHUNK), lambda i: (0, 0, 0)),
                  pl.BlockSpec((CHUNK, D_A), const)],
        out_specs=[flat(D_MODEL), flat(D_A), flat(D_C),
                   pl.BlockSpec((bt, nl, D_B), lambda i: (i, 0, 0)),
                   pl.BlockSpec((None, bt, nl, D_A), lambda i: (0, i, 0, 0))],
        out_shape=[jax.ShapeDtypeStruct((m, D_MODEL), BF16),
                   jax.ShapeDtypeStruct((m, D_A), BF16),
                   jax.ShapeDtypeStruct((m, D_C), F32),
                   jax.ShapeDtypeStruct((nb, nl, D_B), F32),
                   jax.ShapeDtypeStruct((1, nb, nl, D_A), F32)],
        compiler_params=_params("arbitrary"),
        name="inproj_sample",
    )(x, g_mix.reshape(1, D_MODEL), w_in, w_in, w_in, ln_v_g.reshape(1, D_A), ln_v_b.reshape(1, D_A), ws, bs)


def _convb_sample_kernel(hist_ref, x_ref, w_ref, cb_ref, lng_ref, lnb_ref, o_ref, nh_ref):
    bt, nl, _ = x_ref.shape
    nh = hist_ref.shape[0]
    new_rows = [x_ref[:, t, :] for t in range(nl)]
    window = lambda j: hist_ref[j] if j < nh else new_rows[j - nh]
    for t in range(nl):
        acc = jnp.broadcast_to(cb_ref[...], (bt, D_B))
        for k in range(CONV_B):
            acc = acc + w_ref[k:k + 1, :] * window(t + k)
        y = _layer_norm(acc, lng_ref[...], lnb_ref[...])
        o_ref[:, t, :] = y * jax.nn.sigmoid(y)
    for j in range(nh):
        nh_ref[j] = window(j + nl)


def _convb_sample_call(hist, l, x, conv_w, conv_b, ln_g, ln_b):
    nb, nl, _ = x.shape
    nh = hist.shape[1]
    bt = CONV_BT
    const = lambda i: (0, 0)
    state_spec = pl.BlockSpec((None, nh, bt, D_B), lambda i: (l, 0, i, 0))
    new_spec = pl.BlockSpec((bt, nl, D_B), lambda i: (i, 0, 0))
    return pl.pallas_call(
        _convb_sample_kernel,
        grid=(nb // bt,),
        in_specs=[state_spec, new_spec,
                  pl.BlockSpec((CONV_B, D_B), const),
                  pl.BlockSpec((1, D_B), const),
                  pl.BlockSpec((1, D_B), const),
                  pl.BlockSpec((1, D_B), const)],
        out_specs=[new_spec, pl.BlockSpec((None, nh, bt, D_B), lambda i: (0, 0, i, 0))],
        out_shape=[jax.ShapeDtypeStruct((nb, nl, D_B), F32),
                   jax.ShapeDtypeStruct((1, nh, nb, D_B), F32)],
        compiler_params=_params("parallel"),
        name="convb_sample",
    )(hist, x, conv_w, conv_b.reshape(1, D_B), ln_g.reshape(1, D_B), ln_b.reshape(1, D_B))


def _memkv_kernel(m_ref, g_ref, wk_ref, wv_ref, k5_ref, v5_ref, kb_ref, vb_ref, wkb_ref, wvb_ref):
    @pl.when(pl.program_id(0) == 0)
    def _():
        wkb_ref[...] = wk_ref[...].astype(BF16)
        wvb_ref[...] = wv_ref[...].astype(BF16)

    mn = _rms(m_ref[...], g_ref[...]).astype(BF16)
    k = _dot(mn, wkb_ref[...])
    v = _dot(mn, wvb_ref[...])
    k5_ref[...] = k.reshape(N_MEM, N_XHEADS, XHEAD_DIM)
    v5_ref[...] = v.reshape(N_MEM, N_XHEADS, XHEAD_DIM)
    kb_ref[...] = k.astype(BF16)
    vb_ref[...] = v.astype(BF16)


def _memkv_call(mem, g_mem, w_k, w_v):
    nb = mem.shape[0]
    const = lambda i: (0, 0)
    cache_spec = pl.BlockSpec((None, None, N_MEM, N_XHEADS, XHEAD_DIM), lambda i: (0, i, 0, 0, 0))
    seq_spec = pl.BlockSpec((None, N_MEM, D_C), lambda i: (i, 0, 0))
    cache_shape = jax.ShapeDtypeStruct((1, nb, N_MEM, N_XHEADS, XHEAD_DIM), F32)
    return pl.pallas_call(
        _memkv_kernel,
        grid=(nb,),
        in_specs=[pl.BlockSpec((None, N_MEM, D_MODEL), lambda i: (i, 0, 0)),
                  pl.BlockSpec((1, D_MODEL), const),
                  pl.BlockSpec((D_MODEL, D_C), const, pipeline_mode=pl.Buffered(1)),
                  pl.BlockSpec((D_MODEL, D_C), const, pipeline_mode=pl.Buffered(1))],
        out_specs=[cache_spec, cache_spec, seq_spec, seq_spec],
        out_shape=[cache_shape, cache_shape,
                   jax.ShapeDtypeStruct((nb, N_MEM, D_C), BF16),
                   jax.ShapeDtypeStruct((nb, N_MEM, D_C), BF16)],
        scratch_shapes=[pltpu.VMEM((D_MODEL, D_C), BF16), pltpu.VMEM((D_MODEL, D_C), BF16)],
        compiler_params=_params("arbitrary"),
        name="memory_kv",
    )(mem, g_mem.reshape(1, D_MODEL), w_k, w_v)


GATE_COLS = 1024


def _attn_prompt_kernel(q_ref, k_ref, v_ref, *refs):
    gate_refs, o_ref, gates_o = refs[:-2], refs[-2], refs[-1]
    for n, g_ref in enumerate(gate_refs):
        gates_o[:, n * GATE_COLS:(n + 1) * GATE_COLS] = g_ref[...].astype(BF16)
    for h in range(N_XHEADS):
        cs = slice(h * XHEAD_DIM, (h + 1) * XHEAD_DIM)
        s = lax.dot_general(q_ref[:, cs], k_ref[:, cs], (((1,), (1,)), ((), ())), preferred_element_type=F32)
        p = _softmax(s * (XHEAD_DIM ** -0.5)).astype(BF16)
        o_ref[:, cs] = _dot(p, v_ref[:, cs]).astype(BF16)


def _attn_prompt_call(q, k, v, w_in):
    m = q.shape[0]
    nb = k.shape[0]
    tm = TM
    nt = m // nb // tm
    n_gate = 3 * D_MODEL
    slab = D_MODEL // (nb * nt)
    assert OFF_GATES % GATE_COLS == 0 and n_gate % GATE_COLS == 0 and D_MODEL % (nb * nt) == 0
    gate_specs = [pl.BlockSpec((slab, GATE_COLS), lambda b, t, c=c: (b * nt + t, OFF_GATES // GATE_COLS + c))
                  for c in range(n_gate // GATE_COLS)]
    return pl.pallas_call(
        _attn_prompt_kernel,
        grid=(nb, nt),
        in_specs=[pl.BlockSpec((tm, D_C), lambda b, t: (b * nt + t, 0)),
                  pl.BlockSpec((None, N_MEM, D_C), lambda b, t: (b, 0, 0)),
                  pl.BlockSpec((None, N_MEM, D_C), lambda b, t: (b, 0, 0))] + gate_specs,
        out_specs=[pl.BlockSpec((tm, D_C), lambda b, t: (b * nt + t, 0)),
                   pl.BlockSpec((slab, n_gate), lambda b, t: (b * nt + t, 0))],
        out_shape=[jax.ShapeDtypeStruct((m, D_C), BF16),
                   jax.ShapeDtypeStruct((D_MODEL, n_gate), BF16)],
        compiler_params=_params("arbitrary", "arbitrary"),
        name="attn_prompt",
    )(q, k, v, *([w_in] * len(gate_specs)))


def _attn_sample_kernel(q_ref, k_ref, v_ref, o_ref):
    bt = k_ref.shape[0]
    nl = q_ref.shape[0] // bt
    rows = lax.broadcasted_iota(jnp.int32, (N_XHEADS * nl, N_MEM * N_XHEADS), 0)
    cols = lax.broadcasted_iota(jnp.int32, (N_XHEADS * nl, N_MEM * N_XHEADS), 1)
    same_head = (cols % N_XHEADS) == (rows // nl)
    outs = []
    for b in range(bt):
        q = q_ref[b * nl:(b + 1) * nl, :]
        qs = jnp.concatenate([q[:, h * XHEAD_DIM:(h + 1) * XHEAD_DIM] for h in range(N_XHEADS)], axis=0)
        k2 = k_ref[b].reshape(N_MEM * N_XHEADS, XHEAD_DIM).astype(BF16)
        v2 = v_ref[b].reshape(N_MEM * N_XHEADS, XHEAD_DIM).astype(BF16)
        s = lax.dot_general(qs.astype(BF16), k2, (((1,), (1,)), ((), ())), preferred_element_type=F32)
        p = _softmax(jnp.where(same_head, s * (XHEAD_DIM ** -0.5), -jnp.inf)).astype(BF16)
        o = _dot(p, v2)
        outs.append(jnp.concatenate([o[h * nl:(h + 1) * nl, :] for h in range(N_XHEADS)], axis=1))
    o_ref[...] = jnp.concatenate(outs, axis=0).astype(BF16)


def _attn_sample_call(q, k, v, l, nl):
    m = q.shape[0]
    bt = ATTN_BT
    cache_spec = pl.BlockSpec((None, bt, N_MEM, N_XHEADS, XHEAD_DIM), lambda i: (l, i, 0, 0, 0))
    return pl.pallas_call(
        _attn_sample_kernel,
        grid=(m // (bt * nl),),
        in_specs=[pl.BlockSpec((bt * nl, D_C), lambda i: (i, 0)), cache_spec, cache_spec],
        out_specs=pl.BlockSpec((bt * nl, D_C), lambda i: (i, 0)),
        out_shape=jax.ShapeDtypeStruct((m, D_C), BF16),
        compiler_params=_params("parallel"),
        name="attn_sample",
    )(q, k, v)


def _cast_ffn_weights(wu_f, wd_f, wa_o, wg_o, wd_o, step):
    wu = wu_f[...]
    zpad = jnp.zeros((wu.shape[0], D_FF_PAD - D_FF), BF16)
    wa_o[:, :D_FF] = wu[:, :D_FF].astype(BF16)
    wa_o[:, D_FF:] = zpad
    wg_o[:, :D_FF] = wu[:, D_FF:].astype(BF16)
    wg_o[:, D_FF:] = zpad
    live_blocks = D_FF // wd_f.shape[0]

    @pl.when(step < live_blocks)
    def _():
        wd_o[...] = wd_f[...].astype(BF16)

    @pl.when(step >= live_blocks)
    def _():
        wd_o[...] = jnp.zeros(wd_o.shape, BF16)


def _mix_kernel(*refs, cast_ffn):
    h_ref, a_ref, b_ref, c_ref, wpa_ref, wpb_ref, wpc_ref, wga_ref, wgb_ref, wgc_ref = refs[:10]
    if cast_ffn:
        wu_f, wd_f, o_ref, wa_o, wg_o, wd_o = refs[10:]
        _cast_ffn_weights(wu_f, wd_f, wa_o, wg_o, wd_o, pl.program_id(0) * pl.num_programs(1) + pl.program_id(1))
    else:
        (o_ref,) = refs[10:]
    h = h_ref[...]
    b = b_ref[...].reshape(h.shape[0], D_B).astype(BF16)
    mix = jax.nn.sigmoid(_dot(h, wga_ref[...])) * _dot(a_ref[...], wpa_ref[...])
    mix = mix + jax.nn.sigmoid(_dot(h, wgb_ref[...])) * _dot(b, wpb_ref[...])
    mix = mix + jax.nn.sigmoid(_dot(h, wgc_ref[...])) * _dot(c_ref[...], wpc_ref[...])
    o_ref[...] = mix.astype(BF16)


def _mix_call(h, a, b, c, w_pa, w_pb, w_pc, w_gates, name, ffn_weights=None):
    m = h.shape[0]
    tm = min(TM, m)
    tn = MIX_TN
    ni = m // tm
    nj = D_MODEL // tn
    per_gate = D_MODEL // tn
    row = lambda j, i: (i, 0)
    col = lambda j, i: (0, j)
    if b.ndim == 3:
        assert m == tm
        b_spec = pl.BlockSpec(b.shape, lambda j, i: (0, 0, 0))
    else:
        b_spec = pl.BlockSpec((tm, D_B), row)
    in_specs = [pl.BlockSpec((tm, D_MODEL), row),
                pl.BlockSpec((tm, D_A), row),
                b_spec,
                pl.BlockSpec((tm, D_C), row),
                pl.BlockSpec((D_A, tn), col),
                pl.BlockSpec((D_B, tn), col),
                pl.BlockSpec((D_C, tn), col),
                pl.BlockSpec((D_MODEL, tn), lambda j, i: (0, j)),
                pl.BlockSpec((D_MODEL, tn), lambda j, i: (0, per_gate + j)),
                pl.BlockSpec((D_MODEL, tn), lambda j, i: (0, 2 * per_gate + j))]
    out_specs = [pl.BlockSpec((tm, tn), lambda j, i: (i, j))]
    out_shape = [jax.ShapeDtypeStruct((m, D_MODEL), BF16)]
    args = [h, a, b, c, w_pa, w_pb, w_pc, w_gates, w_gates, w_gates]
    if ffn_weights is not None:
        w_up, w_down = ffn_weights
        steps = ni * nj
        up_rows = D_MODEL // steps
        down_rows = CHUNK
        down_blocks = D_FF_PAD // down_rows
        assert D_MODEL % steps == 0 and D_FF % down_rows == 0 and down_blocks <= steps
        step = lambda j, i: j * ni + i
        in_specs += [pl.BlockSpec((up_rows, 2 * D_FF), lambda j, i: (step(j, i), 0)),
                     pl.BlockSpec((down_rows, D_MODEL),
                                  lambda j, i: (jnp.minimum(step(j, i), D_FF // down_rows - 1), 0))]
        out_specs += [pl.BlockSpec((up_rows, D_FF_PAD), lambda j, i: (step(j, i), 0)),
                      pl.BlockSpec((up_rows, D_FF_PAD), lambda j, i: (step(j, i), 0)),
                      pl.BlockSpec((down_rows, D_MODEL),
                                   lambda j, i: (jnp.minimum(step(j, i), down_blocks - 1), 0))]
        out_shape += [jax.ShapeDtypeStruct((D_MODEL, D_FF_PAD), BF16),
                      jax.ShapeDtypeStruct((D_MODEL, D_FF_PAD), BF16),
                      jax.ShapeDtypeStruct((D_FF_PAD, D_MODEL), BF16)]
        args += [w_up, w_down]
    return pl.pallas_call(
        functools.partial(_mix_kernel, cast_ffn=ffn_weights is not None),
        grid=(nj, ni),
        in_specs=in_specs,
        out_specs=out_specs,
        out_shape=out_shape,
        compiler_params=_params("arbitrary", "arbitrary"),
        name=name,
    )(*args)


def _oproj_kernel(x_ref, mix_ref, wo_ref, g_ref, x1_ref, h2_ref):
    x1 = x_ref[...] + _dot(mix_ref[...], wo_ref[...])
    x1_ref[...] = x1
    h2_ref[...] = _rms(x1, g_ref[...]).astype(BF16)


def _oproj_call(x, mix, w_o, g_ffn):
    m = mix.shape[0]
    tm = TM
    row = lambda i: (i, 0)
    return pl.pallas_call(
        _oproj_kernel,
        grid=(m // tm,),
        in_specs=[pl.BlockSpec((tm, D_MODEL), row),
                  pl.BlockSpec((tm, D_MODEL), row),
                  pl.BlockSpec((D_MODEL, D_MODEL), lambda i: (0, 0)),
                  pl.BlockSpec((1, D_MODEL), lambda i: (0, 0))],
        out_specs=[pl.BlockSpec((tm, D_MODEL), row),
                   pl.BlockSpec((tm, D_MODEL), row)],
        out_shape=[jax.ShapeDtypeStruct((m, D_MODEL), F32),
                   jax.ShapeDtypeStruct((m, D_MODEL), BF16)],
        compiler_params=_params("parallel"),
        name="oproj_prompt",
    )(x, mix, w_o, g_ffn.reshape(1, D_MODEL))


def _oproj_sample_kernel(x_ref, mix_ref, wo_ref, g_ref, x1_ref, h2_ref, x3_ref):
    nb, nl, _ = x_ref.shape
    x3_ref[...] = x_ref[...] + _dot(mix_ref[...], wo_ref[...]).reshape(nb, nl, D_MODEL)
    x1 = jnp.concatenate([x3_ref[:, t, :] for t in range(nl)], axis=0)
    x1_ref[...] = x1
    h2_ref[...] = _rms(x1, g_ref[...]).astype(BF16)


def _oproj_sample_call(x, mix, w_o, g_ffn):
    nb, nl, _ = x.shape
    m = nb * nl
    const = lambda i: (0, 0)
    single = pl.Buffered(1)
    return pl.pallas_call(
        _oproj_sample_kernel,
        grid=(1,),
        in_specs=[pl.BlockSpec((nb, nl, D_MODEL), lambda i: (0, 0, 0), pipeline_mode=single),
                  pl.BlockSpec((m, D_MODEL), const, pipeline_mode=single),
                  pl.BlockSpec((D_MODEL, D_MODEL), const, pipeline_mode=single),
                  pl.BlockSpec((1, D_MODEL), const)],
        out_specs=[pl.BlockSpec((m, D_MODEL), const),
                   pl.BlockSpec((m, D_MODEL), const)],
        out_shape=[jax.ShapeDtypeStruct((m, D_MODEL), F32),
                   jax.ShapeDtypeStruct((m, D_MODEL), BF16)],
        scratch_shapes=[pltpu.VMEM((nb, nl, D_MODEL), F32)],
        compiler_params=_params("arbitrary"),
        name="oproj_sample",
    )(x, mix, w_o, g_ffn.reshape(1, D_MODEL))


FFN_TM = 1024
FFN_SUB = 256


def _ffn_hidden(h, wa_ref, wg_ref, cw_ref, cb_ref, fx_ref, base, lag):
    tm = h.shape[0]
    ps = []
    for c in range(wa_ref.shape[1] // FFN_SUB):
        cs = slice(c * FFN_SUB, (c + 1) * FFN_SUB)
        fa = _dot(h, wa_ref[:, cs])
        fg = _dot(h, wg_ref[:, cs])
        fx_ref[base:base + tm, cs] = fa
        prev1 = fx_ref[pl.ds(base - lag, tm), cs]
        prev2 = fx_ref[pl.ds(base - 2 * lag, tm), cs]
        fc = cw_ref[2:3, cs] * fa + cw_ref[1:2, cs] * prev1 + cw_ref[0:1, cs] * prev2 + cb_ref[:, cs]
        ps.append((jax.nn.gelu(fc) * fg).astype(BF16))
    return jnp.concatenate(ps, axis=1)


def _ffn_prompt_kernel(h_ref, x1_hbm, wa_ref, wg_ref, wd_ref, cw_ref, cb_ref, g_ref, y_ref, tail_ref,
                       fx_ref, carry_ref, x1_ref, x1_sem, *, nt):
    tm = h_ref.shape[0]
    tf = wa_ref.shape[1]
    i = pl.program_id(0)
    j = pl.program_id(1)
    x1_copy = pltpu.make_async_copy(x1_hbm.at[pl.ds(i * tm, tm), :], x1_ref, x1_sem)

    @pl.when(j == 0)
    def _():
        x1_copy.start()
        y_ref[...] = jnp.zeros(y_ref.shape, F32)

    @pl.when(i % nt == 0)
    def _():
        fx_ref[0:SUBLANE, :] = jnp.zeros((SUBLANE, tf), F32)

    @pl.when(i % nt != 0)
    def _():
        fx_ref[0:SUBLANE, :] = carry_ref[j]

    p = _ffn_hidden(h_ref[...], wa_ref, wg_ref, cw_ref, cb_ref, fx_ref, SUBLANE, 1)
    last = fx_ref[tm:tm + SUBLANE, :]
    carry_ref[j] = last
    tail_ref[0] = last
    y_ref[...] += _dot(p, wd_ref[...])

    @pl.when(j == pl.num_programs(1) - 1)
    def _():
        x1_copy.wait()

        def norm_rows(r, carry):
            rs = pl.ds(pl.multiple_of(r * CHUNK, CHUNK), CHUNK)
            y_ref[rs, :] = _rms(x1_ref[rs, :] + y_ref[rs, :], g_ref[...])
            return carry

        lax.fori_loop(0, tm // CHUNK, norm_rows, 0)


def _ffn_prompt_call(h2, x1, wa, wg, wd, cw, cb, g_final, nb):
    m = h2.shape[0]
    tm = FFN_TM
    nt = m // nb // tm
    nj = D_FF_PAD // TF
    row = lambda i, j: (i, 0)
    col = lambda i, j: (0, j)
    return pl.pallas_call(
        functools.partial(_ffn_prompt_kernel, nt=nt),
        grid=(m // tm, nj),
        in_specs=[pl.BlockSpec((tm, D_MODEL), row),
                  pl.BlockSpec(memory_space=pl.ANY),
                  pl.BlockSpec((D_MODEL, TF), col),
                  pl.BlockSpec((D_MODEL, TF), col),
                  pl.BlockSpec((TF, D_MODEL), lambda i, j: (j, 0)),
                  pl.BlockSpec((CONV_F, TF), col),
                  pl.BlockSpec((1, TF), col),
                  pl.BlockSpec((1, D_MODEL), lambda i, j: (0, 0))],
        out_specs=[pl.BlockSpec((tm, D_MODEL), row),
                   pl.BlockSpec((1, SUBLANE, TF), lambda i, j: (i, 0, j))],
        out_shape=[jax.ShapeDtypeStruct((m, D_MODEL), F32),
                   jax.ShapeDtypeStruct((m // tm, SUBLANE, D_FF_PAD), F32)],
        scratch_shapes=[pltpu.VMEM((tm + SUBLANE, TF), F32),
                        pltpu.VMEM((nj, SUBLANE, TF), F32),
                        pltpu.VMEM((tm, D_MODEL), F32),
                        pltpu.SemaphoreType.DMA(())],
        compiler_params=_params("arbitrary", "arbitrary"),
        name="ffn_prompt",
    )(h2, x1, wa, wg, wd, cw, cb, g_final.reshape(1, D_MODEL))


def _ffn_sample_kernel(h_ref, x1_ref, wa_ref, wg_ref, wd_ref, cw_ref, cb_ref, g_ref, hist_ref,
                       y_ref, st_ref, fx_ref, acc_ref):
    nb = hist_ref.shape[0]
    tm = h_ref.shape[0]
    tf = wa_ref.shape[1]
    j = pl.program_id(1)
    live = (j * tf + lax.broadcasted_iota(jnp.int32, (nb, tf), 1)) < D_FF
    for r in range(CONV_F - 1):
        fx_ref[r * nb:(r + 1) * nb, :] = jnp.where(live, hist_ref[:, r, :], 0.0)
    base = (CONV_F - 1) * nb
    p = _ffn_hidden(h_ref[...], wa_ref, wg_ref, cw_ref, cb_ref, fx_ref, base, nb)
    for r in range(CONV_F - 1):
        st_ref[:, r, :] = fx_ref[tm + r * nb:tm + (r + 1) * nb, :]

    @pl.when(j == 0)
    def _():
        acc_ref[...] = x1_ref[...]

    acc_ref[...] += _dot(p, wd_ref[...])

    @pl.when(j == pl.num_programs(1) - 1)
    def _():
        for t in range(tm // nb):
            y_ref[:, t, :] = _rms(acc_ref[t * nb:(t + 1) * nb, :], g_ref[...])


def _ffn_sample_call(h2, x1, wa, wg, wd, cw, cb, g_final, state, l, nb):
    m = h2.shape[0]
    nj = D_FF_PAD // TF
    row = lambda i, j: (i, 0)
    col = lambda i, j: (0, j)
    return pl.pallas_call(
        _ffn_sample_kernel,
        grid=(1, nj),
        in_specs=[pl.BlockSpec((m, D_MODEL), row, pipeline_mode=pl.Buffered(1)),
                  pl.BlockSpec((m, D_MODEL), row, pipeline_mode=pl.Buffered(1)),
                  pl.BlockSpec((D_MODEL, TF), col),
                  pl.BlockSpec((D_MODEL, TF), col),
                  pl.BlockSpec((TF, D_MODEL), lambda i, j: (j, 0)),
                  pl.BlockSpec((CONV_F, TF), col),
                  pl.BlockSpec((1, TF), col),
                  pl.BlockSpec((1, D_MODEL), lambda i, j: (0, 0)),
                  pl.BlockSpec((None, nb, CONV_F - 1, TF), lambda i, j: (l, 0, 0, j))],
        out_specs=[pl.BlockSpec((nb, m // nb, D_MODEL), lambda i, j: (0, 0, 0)),
                   pl.BlockSpec((None, nb, CONV_F - 1, TF), lambda i, j: (0, 0, 0, j))],
        out_shape=[jax.ShapeDtypeStruct((nb, m // nb, D_MODEL), F32),
                   jax.ShapeDtypeStruct((1, nb, CONV_F - 1, D_FF), F32)],
        scratch_shapes=[pltpu.VMEM(((CONV_F - 1) * nb + m, TF), F32),
                        pltpu.VMEM((m, D_MODEL), F32)],
        compiler_params=_params("arbitrary", "arbitrary"),
        name="ffn_sample",
    )(h2, x1, wa, wg, wd, cw, cb, g_final.reshape(1, D_MODEL), state)


def kernel(x_prompt, x_sample, mem_prompt, cache_mem_k, cache_mem_v, state_conv, state_ffn_conv, g_mix, w_in, ln_v_g, ln_v_b, w_s, b_s, w_pa, conv_w, conv_b, ln_b_g, ln_b_b, w_pb, g_mem, w_k, w_v, w_pc, w_o, g_ffn, w_up, ffn_conv_w, ffn_conv_b, w_down, g_final):
    depth = g_mix.shape[0]
    assert depth == 1
    l = 0
    nbp, lp, _ = x_prompt.shape
    nbs, ls, _ = x_sample.shape
    mp = nbp * lp
    pad_ff = D_FF_PAD - D_FF

    w_in_b = w_in[l][:, :OFF_GATES].astype(BF16)
    cw = jnp.pad(ffn_conv_w[l], ((0, 0), (0, pad_ff)))
    cb = jnp.pad(ffn_conv_b[l], (0, pad_ff)).reshape(1, D_FF_PAD)

    reps = CHUNK // ls
    ws_p = w_s[l]
    bs_p = jnp.repeat(b_s[l].T, D_A // G_A, axis=1)
    bs_s = jnp.repeat(jnp.tile(b_s[l][:, :ls], (1, reps)).T, D_A // G_A, axis=1)

    xp = x_prompt.reshape(mp, D_MODEL)
    h, a, bact, q, ctail, w_pa_b, w_pb_b, w_pc_b, w_o_b = _inproj_prompt_call(
        xp, nbp, g_mix[l], w_in_b, ln_v_g[l], ln_v_b[l], ws_p, bs_p, conv_w[l], conv_b[l], ln_b_g[l], ln_b_b[l],
        (w_pa[l], w_pb[l], w_pc[l], w_o[l]))
    k5, v5, kb, vb = _memkv_call(mem_prompt, g_mem[l], w_k[l], w_v[l])
    cact, w_gates = _attn_prompt_call(q, kb, vb, w_in[l])
    mix, w_up_a, w_up_g, w_down_b = _mix_call(h, a, bact, cact, w_pa_b, w_pb_b, w_pc_b, w_gates, "mix_prompt",
                                              (w_up[l], w_down[l]))
    x1, h2 = _oproj_call(xp, mix, w_o_b, g_ffn[l])
    yp, tail = _ffn_prompt_call(h2, x1, w_up_a, w_up_g, w_down_b, cw, cb, g_final, nbp)
    nt = tail.shape[0] // nbp
    ffn_p = tail.reshape(nbp, nt, SUBLANE, D_FF_PAD)[:, nt - 1, SUBLANE - (CONV_F - 1):, :D_FF]
    nt = ctail.shape[0] // nbp
    conv_p = ctail.reshape(nbp, nt, HALO, D_B)[:, nt - 1, HALO - (CONV_B - 1):]

    h, a, q, glu_s, vn_s = _inproj_sample_call(x_sample, g_mix[l], w_in_b, ln_v_g[l], ln_v_b[l], ws_p, bs_s)
    bact, conv_s = _convb_sample_call(jnp.swapaxes(state_conv, 1, 2), l, glu_s, conv_w[l], conv_b[l],
                                      ln_b_g[l], ln_b_b[l])
    conv_s = jnp.swapaxes(conv_s, 1, 2)
    cact = _attn_sample_call(q, cache_mem_k, cache_mem_v, l, ls)
    (mix,) = _mix_call(h, a, bact, cact, w_pa_b, w_pb_b, w_pc_b, w_gates, "mix_sample")
    x1, h2 = _oproj_sample_call(x_sample, mix, w_o_b, g_ffn[l])
    ys, ffn_s = _ffn_sample_call(h2, x1, w_up_a, w_up_g, w_down_b, cw, cb, g_final, state_ffn_conv, l, nbs)

    return (yp.reshape(nbp, lp, D_MODEL), ys, k5, v5, conv_p[None], ffn_p[None],
            conv_s, ffn_s, vn_s)
```

```python
import functools

import jax
import jax.numpy as jnp
from jax import lax
from jax.experimental import pallas as pl
from jax.experimental.pallas import tpu as pltpu

F32 = jnp.float32
BF16 = jnp.bfloat16

D_MODEL = 2048
CHUNK = 128
D_A = D_MODEL // 2
G_A = 8
D_B = D_MODEL // 2
CONV_B = 31
N_MEM = 256
N_XHEADS = 4
XHEAD_DIM = D_MODEL // 8
D_C = N_XHEADS * XHEAD_DIM
D_FF = ((8 * D_MODEL // 3 + 127) // 128) * 128
CONV_F = 3
EPS = 1e-6

OFF_ZA = 0
OFF_ZB = 2 * D_A
OFF_Q = OFF_ZB + 2 * D_B
OFF_GATES = OFF_Q + D_C

LANE = 128
SUBLANE = 8
TM = 512
TF = 512
D_FF_PAD = ((D_FF + TF - 1) // TF) * TF
HALO = 32
MIX_TN = 512
ATTN_BT = 8
CONV_BT = 16
VMEM_LIMIT = 60 * 1024 * 1024


def _params(*sem):
    return pltpu.CompilerParams(dimension_semantics=sem, vmem_limit_bytes=VMEM_LIMIT)


def _rms(x, g):
    return x * lax.rsqrt(jnp.mean(x * x, axis=-1, keepdims=True) + EPS) * g


def _layer_norm(x, g, b):
    mu = jnp.mean(x, axis=-1, keepdims=True)
    d = x - mu
    var = jnp.mean(d * d, axis=-1, keepdims=True)
    return d * lax.rsqrt(var + EPS) * g + b


def _dot(a, b):
    return jnp.dot(a, b, preferred_element_type=F32)


def _softmax(s):
    e = jnp.exp(s - jnp.max(s, axis=-1, keepdims=True))
    return e / jnp.sum(e, axis=-1, keepdims=True)


def _spatial_gate(u, vb, ws_ref, bs_ref, a_ref, sample):
    tm = vb.shape[0]
    r = lax.broadcasted_iota(jnp.int32, (CHUNK, CHUNK), 0)
    c = lax.broadcasted_iota(jnp.int32, (CHUNK, CHUNK), 1)
    mask = r >= c
    if sample:
        nl = sample
        mask = mask & ((r // nl) == (c // nl))
        pick_rows = jnp.where(c == r % nl, 1.0, 0.0).astype(BF16)
        pick_cols = jnp.where(r == c % nl, 1.0, 0.0).astype(BF16)
    for g in range(G_A):
        cs = slice(g * LANE, (g + 1) * LANE)
        w = ws_ref[g]
        if sample:
            w = _dot(_dot(pick_rows, w.astype(BF16)).astype(BF16), pick_cols)
        wm = jnp.where(mask, w, 0.0).astype(BF16)
        for ch in range(tm // CHUNK):
            rs = slice(ch * CHUNK, (ch + 1) * CHUNK)
            s = _dot(wm, vb[rs, cs]) + bs_ref[:, cs]
            a_ref[rs, cs] = (u[rs, cs] * s).astype(BF16)


IN_SUB = 256
CONV_ROWS = 128


def _conv_taps(xc_ref, sh_ref, w_ref, bias, r0, cs):
    lead = HALO - (CONV_B - 1)
    acc = jnp.broadcast_to(bias, (CONV_ROWS, bias.shape[1]))
    for s in range(SUBLANE):
        qs = [q for q in range(HALO // SUBLANE + 1) if lead <= SUBLANE * q + s < lead + CONV_B]
        n = CONV_ROWS + SUBLANE * qs[-1]
        if s:
            sh_ref[0:n, :] = xc_ref[pl.ds(r0 + s, n), cs]
        for q in qs:
            k = SUBLANE * q + s - lead
            if s:
                win = sh_ref[SUBLANE * q:SUBLANE * q + CONV_ROWS, :]
            else:
                win = xc_ref[r0 + SUBLANE * q:r0 + SUBLANE * q + CONV_ROWS, cs]
            acc = acc + w_ref[k:k + 1, cs] * win
    return acc


def _inproj_prompt_kernel(x_ref, gm_ref, wa_ref, wb_ref, wq_ref, lvg_ref, lvb_ref, ws_ref, bs_ref,
                          cw_ref, cb_ref, lbg_ref, lbb_ref, wpa_f, wpb_f, wpc_f, wo_f,
                          h_ref, a_ref, b_ref, q_ref, ctail_ref, wpa_o, wpb_o, wpc_o, wo_o,
                          xc_ref, cv_ref, sh_ref, u_ref, v_ref, *, nt):
    tm = x_ref.shape[0]
    t = pl.program_id(0) % nt
    for src, dst in ((wpa_f, wpa_o), (wpb_f, wpb_o), (wpc_f, wpc_o), (wo_f, wo_o)):
        dst[...] = src[...].astype(BF16)

    @pl.when(t == 0)
    def _():
        xc_ref[0:HALO, :] = jnp.zeros((HALO, D_B), F32)

    @pl.when(t != 0)
    def _():
        xc_ref[0:HALO, :] = xc_ref[tm:tm + HALO, :]

    h = _rms(x_ref[...], gm_ref[...]).astype(BF16)
    h_ref[...] = h

    for c in range(D_B // IN_SUB):
        cs = slice(c * IN_SUB, (c + 1) * IN_SUB)
        za = _dot(h, wb_ref[:, cs])
        zb = _dot(h, wb_ref[:, D_B + c * IN_SUB:D_B + (c + 1) * IN_SUB])
        xc_ref[HALO:, cs] = za * jax.nn.sigmoid(zb)
        for rb in range(tm // CONV_ROWS):
            r0 = rb * CONV_ROWS
            cv_ref[r0:r0 + CONV_ROWS, cs] = _conv_taps(xc_ref, sh_ref, cw_ref, cb_ref[:, cs], r0, cs)
    ctail_ref[0] = xc_ref[tm:tm + HALO, :]
    y = _layer_norm(cv_ref[...], lbg_ref[...], lbb_ref[...])
    b_ref[...] = (y * jax.nn.sigmoid(y)).astype(BF16)

    for c in range(D_A // IN_SUB):
        cs = slice(c * IN_SUB, (c + 1) * IN_SUB)
        u_ref[:, cs] = jax.nn.gelu(_dot(h, wa_ref[:, cs]))
        v_ref[:, cs] = jax.nn.gelu(_dot(h, wa_ref[:, D_A + c * IN_SUB:D_A + (c + 1) * IN_SUB]))
    q_ref[...] = _dot(h, wq_ref[...]).astype(BF16)
    vb = _layer_norm(v_ref[...], lvg_ref[...], lvb_ref[...]).astype(BF16)
    _spatial_gate(u_ref, vb, ws_ref, bs_ref, a_ref, 0)


def _inproj_weight_specs(const_map):
    single = pl.Buffered(1)
    return [pl.BlockSpec((D_MODEL, 2 * D_A), lambda *i: (0, OFF_ZA // (2 * D_A)), pipeline_mode=single),
            pl.BlockSpec((D_MODEL, 2 * D_B), lambda *i: (0, OFF_ZB // (2 * D_B)), pipeline_mode=single),
            pl.BlockSpec((D_MODEL, D_C), lambda *i: (0, OFF_Q // D_C), pipeline_mode=single)]


def _inproj_prompt_call(x, nb, g_mix, w_in, ln_v_g, ln_v_b, ws, bs, conv_w, conv_b, ln_b_g, ln_b_b, casts):
    m = x.shape[0]
    tm = TM
    nt = m // nb // tm
    steps = m // tm
    slab = lambda w: pl.BlockSpec((w.shape[0] // steps, w.shape[1]), lambda i: (i, 0))
    const = lambda i: (0, 0)
    row = lambda i: (i, 0)
    vec = lambda n: pl.BlockSpec((1, n), const)
    return pl.pallas_call(
        functools.partial(_inproj_prompt_kernel, nt=nt),
        grid=(m // tm,),
        in_specs=[pl.BlockSpec((tm, D_MODEL), row), vec(D_MODEL)] + _inproj_weight_specs(const) + [
                  vec(D_A), vec(D_A),
                  pl.BlockSpec((G_A, CHUNK, CHUNK), lambda i: (0, 0, 0)),
                  pl.BlockSpec((CHUNK, D_A), const),
                  pl.BlockSpec((CONV_B, D_B), const),
                  vec(D_B), vec(D_B), vec(D_B)] + [slab(w) for w in casts],
        out_specs=[pl.BlockSpec((tm, D_MODEL), row),
                   pl.BlockSpec((tm, D_A), row),
                   pl.BlockSpec((tm, D_B), row),
                   pl.BlockSpec((tm, D_C), row),
                   pl.BlockSpec((1, HALO, D_B), lambda i: (i, 0, 0))] + [slab(w) for w in casts],
        out_shape=[jax.ShapeDtypeStruct((m, D_MODEL), BF16),
                   jax.ShapeDtypeStruct((m, D_A), BF16),
                   jax.ShapeDtypeStruct((m, D_B), BF16),
                   jax.ShapeDtypeStruct((m, D_C), BF16),
                   jax.ShapeDtypeStruct((m // tm, HALO, D_B), F32)]
                  + [jax.ShapeDtypeStruct(w.shape, BF16) for w in casts],
        scratch_shapes=[pltpu.VMEM((tm + HALO, D_B), F32),
                        pltpu.VMEM((tm, D_B), F32),
                        pltpu.VMEM((CONV_ROWS + HALO, IN_SUB), F32),
                        pltpu.VMEM((tm, D_A), F32),
                        pltpu.VMEM((tm, D_A), F32)],
        compiler_params=_params("arbitrary"),
        name="inproj_prompt",
    )(x, g_mix.reshape(1, D_MODEL), w_in, w_in, w_in, ln_v_g.reshape(1, D_A), ln_v_b.reshape(1, D_A), ws, bs,
      conv_w, conv_b.reshape(1, D_B), ln_b_g.reshape(1, D_B), ln_b_b.reshape(1, D_B), *casts)


def _inproj_sample_kernel(x_ref, gm_ref, wa_ref, wb_ref, wq_ref, lvg_ref, lvb_ref, ws_ref, bs_ref,
                          h_ref, a_ref, q_ref, glu_ref, vn_ref):
    nb, nl, _ = x_ref.shape
    m = nb * nl
    h = _rms(x_ref[...].reshape(m, D_MODEL), gm_ref[...]).astype(BF16)
    h_ref[...] = h
    zg = jax.nn.gelu(_dot(h, wa_ref[...]))
    vn = _layer_norm(zg[:, D_A:], lvg_ref[...], lvb_ref[...])
    vn_ref[...] = vn.reshape(nb, nl, D_A)
    _spatial_gate(zg[:, :D_A], vn.astype(BF16), ws_ref, bs_ref, a_ref, nl)
    z = _dot(h, wb_ref[...])
    glu_ref[...] = (z[:, :D_B] * jax.nn.sigmoid(z[:, D_B:])).reshape(nb, nl, D_B)
    q_ref[...] = _dot(h, wq_ref[...])


def _inproj_sample_call(x, g_mix, w_in, ln_v_g, ln_v_b, ws, bs):
    nb, nl, _ = x.shape
    m = nb * nl
    bt = 2 * CHUNK // nl
    const = lambda i: (0, 0)
    vec = lambda n: pl.BlockSpec((1, n), const)
    flat = lambda n: pl.BlockSpec((bt * nl, n), lambda i: (i, 0))
    return pl.pallas_call(
        _inproj_sample_kernel,
        grid=(nb // bt,),
        in_specs=[pl.BlockSpec((bt, nl, D_MODEL), lambda i: (i, 0, 0)), vec(D_MODEL)]
                 + _inproj_weight_specs(const) + [
                  vec(D_A), vec(D_A),
                  pl.BlockSpec((G_A, CHUNK, CHUNK), lambda i: (0, 0, 0)),
                  pl.BlockSpec((CHUNK, D_A), const)],
        out_specs=[flat(D_MODEL), flat(D_A), flat(D_C),
                   pl.BlockSpec((bt, nl, D_B), lambda i: (i, 0, 0)),
                   pl.BlockSpec((None, bt, nl, D_A), lambda i: (0, i, 0, 0))],
        out_shape=[jax.ShapeDtypeStruct((m, D_MODEL), BF16),
                   jax.ShapeDtypeStruct((m, D_A), BF16),
                   jax.ShapeDtypeStruct((m, D_C), F32),
                   jax.ShapeDtypeStruct((nb, nl, D_B), F32),
                   jax.ShapeDtypeStruct((1, nb, nl, D_A), F32)],
        compiler_params=_params("arbitrary"),
        name="inproj_sample",
    )(x, g_mix.reshape(1, D_MODEL), w_in, w_in, w_in, ln_v_g.reshape(1, D_A), ln_v_b.reshape(1, D_A), ws, bs)


def _convb_sample_kernel(hist_ref, x_ref, w_ref, cb_ref, lng_ref, lnb_ref, o_ref, nh_ref):
    bt, nl, _ = x_ref.shape
    nh = hist_ref.shape[0]
    new_rows = [x_ref[:, t, :] for t in range(nl)]
    window = lambda j: hist_ref[j] if j < nh else new_rows[j - nh]
    for t in range(nl):
        acc = jnp.broadcast_to(cb_ref[...], (bt, D_B))
        for k in range(CONV_B):
            acc = acc + w_ref[k:k + 1, :] * window(t + k)
        y = _layer_norm(acc, lng_ref[...], lnb_ref[...])
        o_ref[:, t, :] = y * jax.nn.sigmoid(y)
    for j in range(nh):
        nh_ref[j] = window(j + nl)


def _convb_sample_call(hist, l, x, conv_w, conv_b, ln_g, ln_b):
    nb, nl, _ = x.shape
    nh = hist.shape[1]
    bt = CONV_BT
    const = lambda i: (0, 0)
    state_spec = pl.BlockSpec((None, nh, bt, D_B), lambda i: (l, 0, i, 0))
    new_spec = pl.BlockSpec((bt, nl, D_B), lambda i: (i, 0, 0))
    return pl.pallas_call(
        _convb_sample_kernel,
        grid=(nb // bt,),
        in_specs=[state_spec, new_spec,
                  pl.BlockSpec((CONV_B, D_B), const),
                  pl.BlockSpec((1, D_B), const),
                  pl.BlockSpec((1, D_B), const),
                  pl.BlockSpec((1, D_B), const)],
        out_specs=[new_spec, pl.BlockSpec((None, nh, bt, D_B), lambda i: (0, 0, i, 0))],
        out_shape=[jax.ShapeDtypeStruct((nb, nl, D_B), F32),
                   jax.ShapeDtypeStruct((1, nh, nb, D_B), F32)],
        compiler_params=_params("parallel"),
        name="convb_sample",
    )(hist, x, conv_w, conv_b.reshape(1, D_B), ln_g.reshape(1, D_B), ln_b.reshape(1, D_B))


GATE_COLS = 1024


def _cast_gate_columns(gate_refs, gates_o):
    for n, g_ref in enumerate(gate_refs):
        gates_o[:, n * GATE_COLS:(n + 1) * GATE_COLS] = g_ref[...].astype(BF16)


def _gate_column_specs(first, count, rows, step):
    base = OFF_GATES // GATE_COLS + first
    return [pl.BlockSpec((rows, GATE_COLS), lambda *ids, c=c: (step(*ids), base + c)) for c in range(count)]


def _memkv_kernel(m_ref, g_ref, wk_ref, wv_ref, *refs):
    gate_refs = refs[:-7]
    k5_ref, v5_ref, kb_ref, vb_ref, gates_o, wkb_ref, wvb_ref = refs[-7:]
    _cast_gate_columns(gate_refs, gates_o)

    @pl.when(pl.program_id(0) == 0)
    def _():
        wkb_ref[...] = wk_ref[...].astype(BF16)
        wvb_ref[...] = wv_ref[...].astype(BF16)

    mn = _rms(m_ref[...], g_ref[...]).astype(BF16)
    k = _dot(mn, wkb_ref[...])
    v = _dot(mn, wvb_ref[...])
    k5_ref[...] = k.reshape(N_MEM, N_XHEADS, XHEAD_DIM)
    v5_ref[...] = v.reshape(N_MEM, N_XHEADS, XHEAD_DIM)
    kb_ref[...] = k.astype(BF16)
    vb_ref[...] = v.astype(BF16)


def _memkv_call(mem, g_mem, w_k, w_v, w_in):
    nb = mem.shape[0]
    const = lambda i: (0, 0)
    slab = D_MODEL // nb
    gate_specs = _gate_column_specs(0, D_MODEL // GATE_COLS, slab, lambda i: i)
    cache_spec = pl.BlockSpec((None, None, N_MEM, N_XHEADS, XHEAD_DIM), lambda i: (0, i, 0, 0, 0))
    seq_spec = pl.BlockSpec((None, N_MEM, D_C), lambda i: (i, 0, 0))
    cache_shape = jax.ShapeDtypeStruct((1, nb, N_MEM, N_XHEADS, XHEAD_DIM), F32)
    return pl.pallas_call(
        _memkv_kernel,
        grid=(nb,),
        in_specs=[pl.BlockSpec((None, N_MEM, D_MODEL), lambda i: (i, 0, 0)),
                  pl.BlockSpec((1, D_MODEL), const),
                  pl.BlockSpec((D_MODEL, D_C), const, pipeline_mode=pl.Buffered(1)),
                  pl.BlockSpec((D_MODEL, D_C), const, pipeline_mode=pl.Buffered(1))] + gate_specs,
        out_specs=[cache_spec, cache_spec, seq_spec, seq_spec,
                   pl.BlockSpec((slab, D_MODEL), lambda i: (i, 0))],
        out_shape=[cache_shape, cache_shape,
                   jax.ShapeDtypeStruct((nb, N_MEM, D_C), BF16),
                   jax.ShapeDtypeStruct((nb, N_MEM, D_C), BF16),
                   jax.ShapeDtypeStruct((D_MODEL, D_MODEL), BF16)],
        scratch_shapes=[pltpu.VMEM((D_MODEL, D_C), BF16), pltpu.VMEM((D_MODEL, D_C), BF16)],
        compiler_params=_params("arbitrary"),
        name="memory_kv",
    )(mem, g_mem.reshape(1, D_MODEL), w_k, w_v, *([w_in] * len(gate_specs)))


def _attn_prompt_kernel(q_ref, k_ref, v_ref, *refs):
    gate_refs, o_ref, gates_o = refs[:-2], refs[-2], refs[-1]
    _cast_gate_columns(gate_refs, gates_o)
    for h in range(N_XHEADS):
        cs = slice(h * XHEAD_DIM, (h + 1) * XHEAD_DIM)
        s = lax.dot_general(q_ref[:, cs], k_ref[:, cs], (((1,), (1,)), ((), ())), preferred_element_type=F32)
        p = _softmax(s * (XHEAD_DIM ** -0.5)).astype(BF16)
        o_ref[:, cs] = _dot(p, v_ref[:, cs]).astype(BF16)


def _attn_prompt_call(q, k, v, w_in):
    m = q.shape[0]
    nb = k.shape[0]
    tm = TM
    nt = m // nb // tm
    n_gate = 2 * D_MODEL
    slab = D_MODEL // (nb * nt)
    assert OFF_GATES % GATE_COLS == 0 and D_MODEL % GATE_COLS == 0 and D_MODEL % (nb * nt) == 0
    gate_specs = _gate_column_specs(D_MODEL // GATE_COLS, n_gate // GATE_COLS, slab, lambda b, t: b * nt + t)
    return pl.pallas_call(
        _attn_prompt_kernel,
        grid=(nb, nt),
        in_specs=[pl.BlockSpec((tm, D_C), lambda b, t: (b * nt + t, 0)),
                  pl.BlockSpec((None, N_MEM, D_C), lambda b, t: (b, 0, 0)),
                  pl.BlockSpec((None, N_MEM, D_C), lambda b, t: (b, 0, 0))] + gate_specs,
        out_specs=[pl.BlockSpec((tm, D_C), lambda b, t: (b * nt + t, 0)),
                   pl.BlockSpec((slab, n_gate), lambda b, t: (b * nt + t, 0))],
        out_shape=[jax.ShapeDtypeStruct((m, D_C), BF16),
                   jax.ShapeDtypeStruct((D_MODEL, n_gate), BF16)],
        compiler_params=_params("arbitrary", "arbitrary"),
        name="attn_prompt",
    )(q, k, v, *([w_in] * len(gate_specs)))


def _attn_sample_kernel(q_ref, k_ref, v_ref, o_ref):
    bt = k_ref.shape[0]
    nl = q_ref.shape[0] // bt
    rows = lax.broadcasted_iota(jnp.int32, (N_XHEADS * nl, N_MEM * N_XHEADS), 0)
    cols = lax.broadcasted_iota(jnp.int32, (N_XHEADS * nl, N_MEM * N_XHEADS), 1)
    same_head = (cols % N_XHEADS) == (rows // nl)
    outs = []
    for b in range(bt):
        q = q_ref[b * nl:(b + 1) * nl, :]
        qs = jnp.concatenate([q[:, h * XHEAD_DIM:(h + 1) * XHEAD_DIM] for h in range(N_XHEADS)], axis=0)
        k2 = k_ref[b].reshape(N_MEM * N_XHEADS, XHEAD_DIM).astype(BF16)
        v2 = v_ref[b].reshape(N_MEM * N_XHEADS, XHEAD_DIM).astype(BF16)
        s = lax.dot_general(qs.astype(BF16), k2, (((1,), (1,)), ((), ())), preferred_element_type=F32)
        p = _softmax(jnp.where(same_head, s * (XHEAD_DIM ** -0.5), -jnp.inf)).astype(BF16)
        o = _dot(p, v2)
        outs.append(jnp.concatenate([o[h * nl:(h + 1) * nl, :] for h in range(N_XHEADS)], axis=1))
    o_ref[...] = jnp.concatenate(outs, axis=0).astype(BF16)


def _attn_sample_call(q, k, v, l, nl):
    m = q.shape[0]
    bt = ATTN_BT
    cache_spec = pl.BlockSpec((None, bt, N_MEM, N_XHEADS, XHEAD_DIM), lambda i: (l, i, 0, 0, 0))
    return pl.pallas_call(
        _attn_sample_kernel,
        grid=(m // (bt * nl),),
        in_specs=[pl.BlockSpec((bt * nl, D_C), lambda i: (i, 0)), cache_spec, cache_spec],
        out_specs=pl.BlockSpec((bt * nl, D_C), lambda i: (i, 0)),
        out_shape=jax.ShapeDtypeStruct((m, D_C), BF16),
        compiler_params=_params("parallel"),
        name="attn_sample",
    )(q, k, v)


def _cast_ffn_weights(wu_f, wd_f, wa_o, wg_o, wd_o, step):
    wu = wu_f[...]
    zpad = jnp.zeros((wu.shape[0], D_FF_PAD - D_FF), BF16)
    wa_o[:, :D_FF] = wu[:, :D_FF].astype(BF16)
    wa_o[:, D_FF:] = zpad
    wg_o[:, :D_FF] = wu[:, D_FF:].astype(BF16)
    wg_o[:, D_FF:] = zpad
    live_blocks = D_FF // wd_f.shape[0]

    @pl.when(step < live_blocks)
    def _():
        wd_o[...] = wd_f[...].astype(BF16)

    @pl.when(step >= live_blocks)
    def _():
        wd_o[...] = jnp.zeros(wd_o.shape, BF16)


def _mix_kernel(*refs, cast_ffn):
    h_ref, a_ref, b_ref, c_ref, wpa_ref, wpb_ref, wpc_ref, wga_ref, wgb_ref, wgc_ref = refs[:10]
    if cast_ffn:
        wu_f, wd_f, o_ref, wa_o, wg_o, wd_o = refs[10:]
        _cast_ffn_weights(wu_f, wd_f, wa_o, wg_o, wd_o, pl.program_id(0) * pl.num_programs(1) + pl.program_id(1))
    else:
        (o_ref,) = refs[10:]
    h = h_ref[...]
    b = b_ref[...].reshape(h.shape[0], D_B).astype(BF16)
    mix = jax.nn.sigmoid(_dot(h, wga_ref[...])) * _dot(a_ref[...], wpa_ref[...])
    mix = mix + jax.nn.sigmoid(_dot(h, wgb_ref[...])) * _dot(b, wpb_ref[...])
    mix = mix + jax.nn.sigmoid(_dot(h, wgc_ref[...])) * _dot(c_ref[...], wpc_ref[...])
    o_ref[...] = mix.astype(BF16)


def _mix_call(h, a, b, c, w_pa, w_pb, w_pc, w_gate_a, w_gate_bc, name, ffn_weights=None):
    m = h.shape[0]
    tm = min(TM, m)
    tn = MIX_TN
    ni = m // tm
    nj = D_MODEL // tn
    per_gate = D_MODEL // tn
    row = lambda j, i: (i, 0)
    col = lambda j, i: (0, j)
    if b.ndim == 3:
        assert m == tm
        b_spec = pl.BlockSpec(b.shape, lambda j, i: (0, 0, 0))
    else:
        b_spec = pl.BlockSpec((tm, D_B), row)
    in_specs = [pl.BlockSpec((tm, D_MODEL), row),
                pl.BlockSpec((tm, D_A), row),
                b_spec,
                pl.BlockSpec((tm, D_C), row),
                pl.BlockSpec((D_A, tn), col),
                pl.BlockSpec((D_B, tn), col),
                pl.BlockSpec((D_C, tn), col),
                pl.BlockSpec((D_MODEL, tn), lambda j, i: (0, j)),
                pl.BlockSpec((D_MODEL, tn), lambda j, i: (0, j)),
                pl.BlockSpec((D_MODEL, tn), lambda j, i: (0, per_gate + j))]
    out_specs = [pl.BlockSpec((tm, tn), lambda j, i: (i, j))]
    out_shape = [jax.ShapeDtypeStruct((m, D_MODEL), BF16)]
    args = [h, a, b, c, w_pa, w_pb, w_pc, w_gate_a, w_gate_bc, w_gate_bc]
    if ffn_weights is not None:
        w_up, w_down = ffn_weights
        steps = ni * nj
        up_rows = D_MODEL // steps
        down_rows = CHUNK
        down_blocks = D_FF_PAD // down_rows
        assert D_MODEL % steps == 0 and D_FF % down_rows == 0 and down_blocks <= steps
        step = lambda j, i: j * ni + i
        in_specs += [pl.BlockSpec((up_rows, 2 * D_FF), lambda j, i: (step(j, i), 0)),
                     pl.BlockSpec((down_rows, D_MODEL),
                                  lambda j, i: (jnp.minimum(step(j, i), D_FF // down_rows - 1), 0))]
        out_specs += [pl.BlockSpec((up_rows, D_FF_PAD), lambda j, i: (step(j, i), 0)),
                      pl.BlockSpec((up_rows, D_FF_PAD), lambda j, i: (step(j, i), 0)),
                      pl.BlockSpec((down_rows, D_MODEL),
                                   lambda j, i: (jnp.minimum(step(j, i), down_blocks - 1), 0))]
        out_shape += [jax.ShapeDtypeStruct((D_MODEL, D_FF_PAD), BF16),
                      jax.ShapeDtypeStruct((D_MODEL, D_FF_PAD), BF16),
                      jax.ShapeDtypeStruct((D_FF_PAD, D_MODEL), BF16)]
        args += [w_up, w_down]
    return pl.pallas_call(
        functools.partial(_mix_kernel, cast_ffn=ffn_weights is not None),
        grid=(nj, ni),
        in_specs=in_specs,
        out_specs=out_specs,
        out_shape=out_shape,
        compiler_params=_params("arbitrary", "arbitrary"),
        name=name,
    )(*args)


def _oproj_kernel(x_ref, mix_ref, wo_ref, g_ref, x1_ref, h2_ref):
    x1 = x_ref[...] + _dot(mix_ref[...], wo_ref[...])
    x1_ref[...] = x1
    h2_ref[...] = _rms(x1, g_ref[...]).astype(BF16)


def _oproj_call(x, mix, w_o, g_ffn):
    m = mix.shape[0]
    tm = TM
    row = lambda i: (i, 0)
    return pl.pallas_call(
        _oproj_kernel,
        grid=(m // tm,),
        in_specs=[pl.BlockSpec((tm, D_MODEL), row),
                  pl.BlockSpec((tm, D_MODEL), row),
                  pl.BlockSpec((D_MODEL, D_MODEL), lambda i: (0, 0)),
                  pl.BlockSpec((1, D_MODEL), lambda i: (0, 0))],
        out_specs=[pl.BlockSpec((tm, D_MODEL), row),
                   pl.BlockSpec((tm, D_MODEL), row)],
        out_shape=[jax.ShapeDtypeStruct((m, D_MODEL), F32),
                   jax.ShapeDtypeStruct((m, D_MODEL), BF16)],
        compiler_params=_params("parallel"),
        name="oproj_prompt",
    )(x, mix, w_o, g_ffn.reshape(1, D_MODEL))


def _oproj_sample_kernel(x_ref, mix_ref, wo_ref, g_ref, x1_ref, h2_ref, x3_ref):
    nb, nl, _ = x_ref.shape
    x3_ref[...] = x_ref[...] + _dot(mix_ref[...], wo_ref[...]).reshape(nb, nl, D_MODEL)
    x1 = jnp.concatenate([x3_ref[:, t, :] for t in range(nl)], axis=0)
    x1_ref[...] = x1
    h2_ref[...] = _rms(x1, g_ref[...]).astype(BF16)


def _oproj_sample_call(x, mix, w_o, g_ffn):
    nb, nl, _ = x.shape
    m = nb * nl
    const = lambda i: (0, 0)
    single = pl.Buffered(1)
    return pl.pallas_call(
        _oproj_sample_kernel,
        grid=(1,),
        in_specs=[pl.BlockSpec((nb, nl, D_MODEL), lambda i: (0, 0, 0), pipeline_mode=single),
                  pl.BlockSpec((m, D_MODEL), const, pipeline_mode=single),
                  pl.BlockSpec((D_MODEL, D_MODEL), const, pipeline_mode=single),
                  pl.BlockSpec((1, D_MODEL), const)],
        out_specs=[pl.BlockSpec((m, D_MODEL), const),
                   pl.BlockSpec((m, D_MODEL), const)],
        out_shape=[jax.ShapeDtypeStruct((m, D_MODEL), F32),
                   jax.ShapeDtypeStruct((m, D_MODEL), BF16)],
        scratch_shapes=[pltpu.VMEM((nb, nl, D_MODEL), F32)],
        compiler_params=_params("arbitrary"),
        name="oproj_sample",
    )(x, mix, w_o, g_ffn.reshape(1, D_MODEL))


FFN_TM = 1024
FFN_SUB = 256


def _ffn_hidden(h, wa_ref, wg_ref, cw_ref, cb_ref, fx_ref, base, lag):
    tm = h.shape[0]
    ps = []
    for c in range(wa_ref.shape[1] // FFN_SUB):
        cs = slice(c * FFN_SUB, (c + 1) * FFN_SUB)
        fa = _dot(h, wa_ref[:, cs])
        fg = _dot(h, wg_ref[:, cs])
        fx_ref[base:base + tm, cs] = fa
        prev1 = fx_ref[pl.ds(base - lag, tm), cs]
        prev2 = fx_ref[pl.ds(base - 2 * lag, tm), cs]
        fc = cw_ref[2:3, cs] * fa + cw_ref[1:2, cs] * prev1 + cw_ref[0:1, cs] * prev2 + cb_ref[:, cs]
        ps.append((jax.nn.gelu(fc) * fg).astype(BF16))
    return jnp.concatenate(ps, axis=1)


def _ffn_prompt_kernel(h_ref, x1_hbm, wa_ref, wg_ref, wd_ref, cw_ref, cb_ref, g_ref, y_ref, tail_ref,
                       fx_ref, carry_ref, x1_ref, x1_sem, *, nt):
    tm = h_ref.shape[0]
    tf = wa_ref.shape[1]
    i = pl.program_id(0)
    j = pl.program_id(1)
    x1_copy = pltpu.make_async_copy(x1_hbm.at[pl.ds(i * tm, tm), :], x1_ref, x1_sem)

    @pl.when(j == 0)
    def _():
        x1_copy.start()
        y_ref[...] = jnp.zeros(y_ref.shape, F32)

    @pl.when(i % nt == 0)
    def _():
        fx_ref[0:SUBLANE, :] = jnp.zeros((SUBLANE, tf), F32)

    @pl.when(i % nt != 0)
    def _():
        fx_ref[0:SUBLANE, :] = carry_ref[j]

    p = _ffn_hidden(h_ref[...], wa_ref, wg_ref, cw_ref, cb_ref, fx_ref, SUBLANE, 1)
    last = fx_ref[tm:tm + SUBLANE, :]
    carry_ref[j] = last
    tail_ref[0] = last
    y_ref[...] += _dot(p, wd_ref[...])

    @pl.when(j == pl.num_programs(1) - 1)
    def _():
        x1_copy.wait()

        def norm_rows(r, carry):
            rs = pl.ds(pl.multiple_of(r * CHUNK, CHUNK), CHUNK)
            y_ref[rs, :] = _rms(x1_ref[rs, :] + y_ref[rs, :], g_ref[...])
            return carry

        lax.fori_loop(0, tm // CHUNK, norm_rows, 0)


def _ffn_prompt_call(h2, x1, wa, wg, wd, cw, cb, g_final, nb):
    m = h2.shape[0]
    tm = FFN_TM
    nt = m // nb // tm
    nj = D_FF_PAD // TF
    row = lambda i, j: (i, 0)
    col = lambda i, j: (0, j)
    return pl.pallas_call(
        functools.partial(_ffn_prompt_kernel, nt=nt),
        grid=(m // tm, nj),
        in_specs=[pl.BlockSpec((tm, D_MODEL), row),
                  pl.BlockSpec(memory_space=pl.ANY),
                  pl.BlockSpec((D_MODEL, TF), col),
                  pl.BlockSpec((D_MODEL, TF), col),
                  pl.BlockSpec((TF, D_MODEL), lambda i, j: (j, 0)),
                  pl.BlockSpec((CONV_F, TF), col),
                  pl.BlockSpec((1, TF), col),
                  pl.BlockSpec((1, D_MODEL), lambda i, j: (0, 0))],
        out_specs=[pl.BlockSpec((tm, D_MODEL), row),
                   pl.BlockSpec((1, SUBLANE, TF), lambda i, j: (i, 0, j))],
        out_shape=[jax.ShapeDtypeStruct((m, D_MODEL), F32),
                   jax.ShapeDtypeStruct((m // tm, SUBLANE, D_FF_PAD), F32)],
        scratch_shapes=[pltpu.VMEM((tm + SUBLANE, TF), F32),
                        pltpu.VMEM((nj, SUBLANE, TF), F32),
                        pltpu.VMEM((tm, D_MODEL), F32),
                        pltpu.SemaphoreType.DMA(())],
        compiler_params=_params("arbitrary", "arbitrary"),
        name="ffn_prompt",
    )(h2, x1, wa, wg, wd, cw, cb, g_final.reshape(1, D_MODEL))


def _ffn_sample_kernel(h_ref, x1_ref, wa_ref, wg_ref, wd_ref, cw_ref, cb_ref, g_ref, hist_ref,
                       y_ref, st_ref, fx_ref, acc_ref):
    nb = hist_ref.shape[0]
    tm = h_ref.shape[0]
    tf = wa_ref.shape[1]
    j = pl.program_id(1)
    live = (j * tf + lax.broadcasted_iota(jnp.int32, (nb, tf), 1)) < D_FF
    for r in range(CONV_F - 1):
        fx_ref[r * nb:(r + 1) * nb, :] = jnp.where(live, hist_ref[:, r, :], 0.0)
    base = (CONV_F - 1) * nb
    p = _ffn_hidden(h_ref[...], wa_ref, wg_ref, cw_ref, cb_ref, fx_ref, base, nb)
    for r in range(CONV_F - 1):
        st_ref[:, r, :] = fx_ref[tm + r * nb:tm + (r + 1) * nb, :]

    @pl.when(j == 0)
    def _():
        acc_ref[...] = x1_ref[...]

    acc_ref[...] += _dot(p, wd_ref[...])

    @pl.when(j == pl.num_programs(1) - 1)
    def _():
        for t in range(tm // nb):
            y_ref[:, t, :] = _rms(acc_ref[t * nb:(t + 1) * nb, :], g_ref[...])


def _ffn_sample_call(h2, x1, wa, wg, wd, cw, cb, g_final, state, l, nb):
    m = h2.shape[0]
    nj = D_FF_PAD // TF
    row = lambda i, j: (i, 0)
    col = lambda i, j: (0, j)
    return pl.pallas_call(
        _ffn_sample_kernel,
        grid=(1, nj),
        in_specs=[pl.BlockSpec((m, D_MODEL), row, pipeline_mode=pl.Buffered(1)),
                  pl.BlockSpec((m, D_MODEL), row, pipeline_mode=pl.Buffered(1)),
                  pl.BlockSpec((D_MODEL, TF), col),
                  pl.BlockSpec((D_MODEL, TF), col),
                  pl.BlockSpec((TF, D_MODEL), lambda i, j: (j, 0)),
                  pl.BlockSpec((CONV_F, TF), col),
                  pl.BlockSpec((1, TF), col),
                  pl.BlockSpec((1, D_MODEL), lambda i, j: (0, 0)),
                  pl.BlockSpec((None, nb, CONV_F - 1, TF), lambda i, j: (l, 0, 0, j))],
        out_specs=[pl.BlockSpec((nb, m // nb, D_MODEL), lambda i, j: (0, 0, 0)),
                   pl.BlockSpec((None, nb, CONV_F - 1, TF), lambda i, j: (0, 0, 0, j))],
        out_shape=[jax.ShapeDtypeStruct((nb, m // nb, D_MODEL), F32),
                   jax.ShapeDtypeStruct((1, nb, CONV_F - 1, D_FF), F32)],
        scratch_shapes=[pltpu.VMEM(((CONV_F - 1) * nb + m, TF), F32),
                        pltpu.VMEM((m, D_MODEL), F32)],
        compiler_params=_params("arbitrary", "arbitrary"),
        name="ffn_sample",
    )(h2, x1, wa, wg, wd, cw, cb, g_final.reshape(1, D_MODEL), state)


def kernel(x_prompt, x_sample, mem_prompt, cache_mem_k, cache_mem_v, state_conv, state_ffn_conv, g_mix, w_in, ln_v_g, ln_v_b, w_s, b_s, w_pa, conv_w, conv_b, ln_b_g, ln_b_b, w_pb, g_mem, w_k, w_v, w_pc, w_o, g_ffn, w_up, ffn_conv_w, ffn_conv_b, w_down, g_final):
    depth = g_mix.shape[0]
    assert depth == 1
    l = 0
    nbp, lp, _ = x_prompt.shape
    nbs, ls, _ = x_sample.shape
    mp = nbp * lp
    pad_ff = D_FF_PAD - D_FF

    w_in_b = w_in[l][:, :OFF_GATES].astype(BF16)
    cw = jnp.pad(ffn_conv_w[l], ((0, 0), (0, pad_ff)))
    cb = jnp.pad(ffn_conv_b[l], (0, pad_ff)).reshape(1, D_FF_PAD)

    reps = CHUNK // ls
    ws_p = w_s[l]
    bs_p = jnp.repeat(b_s[l].T, D_A // G_A, axis=1)
    bs_s = jnp.repeat(jnp.tile(b_s[l][:, :ls], (1, reps)).T, D_A // G_A, axis=1)

    xp = x_prompt.reshape(mp, D_MODEL)
    h, a, bact, q, ctail, w_pa_b, w_pb_b, w_pc_b, w_o_b = _inproj_prompt_call(
        xp, nbp, g_mix[l], w_in_b, ln_v_g[l], ln_v_b[l], ws_p, bs_p, conv_w[l], conv_b[l], ln_b_g[l], ln_b_b[l],
        (w_pa[l], w_pb[l], w_pc[l], w_o[l]))
    k5, v5, kb, vb, w_gate_a = _memkv_call(mem_prompt, g_mem[l], w_k[l], w_v[l], w_in[l])
    cact, w_gate_bc = _attn_prompt_call(q, kb, vb, w_in[l])
    mix, w_up_a, w_up_g, w_down_b = _mix_call(h, a, bact, cact, w_pa_b, w_pb_b, w_pc_b, w_gate_a, w_gate_bc,
                                              "mix_prompt",
                                              (w_up[l], w_down[l]))
    x1, h2 = _oproj_call(xp, mix, w_o_b, g_ffn[l])
    yp, tail = _ffn_prompt_call(h2, x1, w_up_a, w_up_g, w_down_b, cw, cb, g_final, nbp)
    nt = tail.shape[0] // nbp
    ffn_p = tail.reshape(nbp, nt, SUBLANE, D_FF_PAD)[:, nt - 1, SUBLANE - (CONV_F - 1):, :D_FF]
    nt = ctail.shape[0] // nbp
    conv_p = ctail.reshape(nbp, nt, HALO, D_B)[:, nt - 1, HALO - (CONV_B - 1):]

    h, a, q, glu_s, vn_s = _inproj_sample_call(x_sample, g_mix[l], w_in_b, ln_v_g[l], ln_v_b[l], ws_p, bs_s)
    bact, conv_s = _convb_sample_call(jnp.swapaxes(state_conv, 1, 2), l, glu_s, conv_w[l], conv_b[l],
                                      ln_b_g[l], ln_b_b[l])
    conv_s = jnp.swapaxes(conv_s, 1, 2)
    cact = _attn_sample_call(q, cache_mem_k, cache_mem_v, l, ls)
    (mix,) = _mix_call(h, a, bact, cact, w_pa_b, w_pb_b, w_pc_b, w_gate_a, w_gate_bc, "mix_sample")
    x1, h2 = _oproj_sample_call(x_sample, mix, w_o_b, g_ffn[l])
    ys, ffn_s = _ffn_sample_call(h2, x1, w_up_a, w_up_g, w_down_b, cw, cb, g_final, state_ffn_conv, l, nbs)

    return (yp.reshape(nbp, lp, D_MODEL), ys, k5, v5, conv_p[None], ffn_p[None],
            conv_s, ffn_s, vn_s)
```

```python
import functools

import jax
import jax.numpy as jnp
from jax import lax
from jax.experimental import pallas as pl
from jax.experimental.pallas import tpu as pltpu

F32 = jnp.float32
BF16 = jnp.bfloat16

D_MODEL = 2048
CHUNK = 128
D_A = D_MODEL // 2
G_A = 8
D_B = D_MODEL // 2
CONV_B = 31
N_MEM = 256
N_XHEADS = 4
XHEAD_DIM = D_MODEL // 8
D_C = N_XHEADS * XHEAD_DIM
D_FF = ((8 * D_MODEL // 3 + 127) // 128) * 128
CONV_F = 3
EPS = 1e-6

OFF_ZA = 0
OFF_ZB = 2 * D_A
OFF_Q = OFF_ZB + 2 * D_B
OFF_GATES = OFF_Q + D_C

LANE = 128
SUBLANE = 8
TM = 512
TF = 512
D_FF_PAD = ((D_FF + TF - 1) // TF) * TF
HALO = 32
MIX_TN = 512
CONV_BT = 16
VMEM_LIMIT = 60 * 1024 * 1024


def _params(*sem):
    return pltpu.CompilerParams(dimension_semantics=sem, vmem_limit_bytes=VMEM_LIMIT)


def _rms(x, g):
    return x * lax.rsqrt(jnp.mean(x * x, axis=-1, keepdims=True) + EPS) * g


def _layer_norm(x, g, b):
    mu = jnp.mean(x, axis=-1, keepdims=True)
    d = x - mu
    var = jnp.mean(d * d, axis=-1, keepdims=True)
    return d * lax.rsqrt(var + EPS) * g + b


def _dot(a, b):
    return jnp.dot(a, b, preferred_element_type=F32)


def _softmax(s):
    e = jnp.exp(s - jnp.max(s, axis=-1, keepdims=True))
    return e / jnp.sum(e, axis=-1, keepdims=True)


def _spatial_gate(u, vb, ws_ref, bs_ref, a_ref, sample):
    tm = vb.shape[0]
    r = lax.broadcasted_iota(jnp.int32, (CHUNK, CHUNK), 0)
    c = lax.broadcasted_iota(jnp.int32, (CHUNK, CHUNK), 1)
    mask = r >= c
    if sample:
        nl = sample
        mask = mask & ((r // nl) == (c // nl))
        pick_rows = jnp.where(c == r % nl, 1.0, 0.0).astype(BF16)
        pick_cols = jnp.where(r == c % nl, 1.0, 0.0).astype(BF16)
    for g in range(G_A):
        cs = slice(g * LANE, (g + 1) * LANE)
        w = ws_ref[g]
        if sample:
            w = _dot(_dot(pick_rows, w.astype(BF16)).astype(BF16), pick_cols)
        wm = jnp.where(mask, w, 0.0).astype(BF16)
        for ch in range(tm // CHUNK):
            rs = slice(ch * CHUNK, (ch + 1) * CHUNK)
            s = _dot(wm, vb[rs, cs]) + bs_ref[:, cs]
            a_ref[rs, cs] = (u[rs, cs] * s).astype(BF16)


IN_SUB = 256
CONV_ROWS = 128


def _conv_taps(xc_ref, sh_ref, w_ref, bias, r0, cs):
    lead = HALO - (CONV_B - 1)
    acc = jnp.broadcast_to(bias, (CONV_ROWS, bias.shape[1]))
    for s in range(SUBLANE):
        qs = [q for q in range(HALO // SUBLANE + 1) if lead <= SUBLANE * q + s < lead + CONV_B]
        n = CONV_ROWS + SUBLANE * qs[-1]
        if s:
            sh_ref[0:n, :] = xc_ref[pl.ds(r0 + s, n), cs]
        for q in qs:
            k = SUBLANE * q + s - lead
            if s:
                win = sh_ref[SUBLANE * q:SUBLANE * q + CONV_ROWS, :]
            else:
                win = xc_ref[r0 + SUBLANE * q:r0 + SUBLANE * q + CONV_ROWS, cs]
            acc = acc + w_ref[k:k + 1, cs] * win
    return acc


def _inproj_prompt_kernel(x_ref, gm_ref, wa_ref, wb_ref, wq_ref, lvg_ref, lvb_ref, ws_ref, bs_ref,
                          cw_ref, cb_ref, lbg_ref, lbb_ref, wpa_f, wpb_f, wpc_f, wo_f,
                          h_ref, a_ref, b_ref, q_ref, ctail_ref, wpa_o, wpb_o, wpc_o, wo_o,
                          xc_ref, cv_ref, sh_ref, u_ref, v_ref, *, nt):
    tm = x_ref.shape[0]
    t = pl.program_id(0) % nt
    for src, dst in ((wpa_f, wpa_o), (wpb_f, wpb_o), (wpc_f, wpc_o), (wo_f, wo_o)):
        dst[...] = src[...].astype(BF16)

    @pl.when(t == 0)
    def _():
        xc_ref[0:HALO, :] = jnp.zeros((HALO, D_B), F32)

    @pl.when(t != 0)
    def _():
        xc_ref[0:HALO, :] = xc_ref[tm:tm + HALO, :]

    h = _rms(x_ref[...], gm_ref[...]).astype(BF16)
    h_ref[...] = h

    for c in range(D_B // IN_SUB):
        cs = slice(c * IN_SUB, (c + 1) * IN_SUB)
        za = _dot(h, wb_ref[:, cs])
        zb = _dot(h, wb_ref[:, D_B + c * IN_SUB:D_B + (c + 1) * IN_SUB])
        xc_ref[HALO:, cs] = za * jax.nn.sigmoid(zb)
        for rb in range(tm // CONV_ROWS):
            r0 = rb * CONV_ROWS
            cv_ref[r0:r0 + CONV_ROWS, cs] = _conv_taps(xc_ref, sh_ref, cw_ref, cb_ref[:, cs], r0, cs)
    ctail_ref[0] = xc_ref[tm:tm + HALO, :]
    y = _layer_norm(cv_ref[...], lbg_ref[...], lbb_ref[...])
    b_ref[...] = (y * jax.nn.sigmoid(y)).astype(BF16)

    for c in range(D_A // IN_SUB):
        cs = slice(c * IN_SUB, (c + 1) * IN_SUB)
        u_ref[:, cs] = jax.nn.gelu(_dot(h, wa_ref[:, cs]))
        v_ref[:, cs] = jax.nn.gelu(_dot(h, wa_ref[:, D_A + c * IN_SUB:D_A + (c + 1) * IN_SUB]))
    q_ref[...] = _dot(h, wq_ref[...]).astype(BF16)
    vb = _layer_norm(v_ref[...], lvg_ref[...], lvb_ref[...]).astype(BF16)
    _spatial_gate(u_ref, vb, ws_ref, bs_ref, a_ref, 0)


def _inproj_weight_specs(const_map):
    single = pl.Buffered(1)
    return [pl.BlockSpec((D_MODEL, 2 * D_A), lambda *i: (0, OFF_ZA // (2 * D_A)), pipeline_mode=single),
            pl.BlockSpec((D_MODEL, 2 * D_B), lambda *i: (0, OFF_ZB // (2 * D_B)), pipeline_mode=single),
            pl.BlockSpec((D_MODEL, D_C), lambda *i: (0, OFF_Q // D_C), pipeline_mode=single)]


def _inproj_prompt_call(x, nb, g_mix, w_in, ln_v_g, ln_v_b, ws, bs, conv_w, conv_b, ln_b_g, ln_b_b, casts):
    m = x.shape[0]
    tm = TM
    nt = m // nb // tm
    steps = m // tm
    slab = lambda w: pl.BlockSpec((w.shape[0] // steps, w.shape[1]), lambda i: (i, 0))
    const = lambda i: (0, 0)
    row = lambda i: (i, 0)
    vec = lambda n: pl.BlockSpec((1, n), const)
    return pl.pallas_call(
        functools.partial(_inproj_prompt_kernel, nt=nt),
        grid=(m // tm,),
        in_specs=[pl.BlockSpec((tm, D_MODEL), row), vec(D_MODEL)] + _inproj_weight_specs(const) + [
                  vec(D_A), vec(D_A),
                  pl.BlockSpec((G_A, CHUNK, CHUNK), lambda i: (0, 0, 0)),
                  pl.BlockSpec((CHUNK, D_A), const),
                  pl.BlockSpec((CONV_B, D_B), const),
                  vec(D_B), vec(D_B), vec(D_B)] + [slab(w) for w in casts],
        out_specs=[pl.BlockSpec((tm, D_MODEL), row),
                   pl.BlockSpec((tm, D_A), row),
                   pl.BlockSpec((tm, D_B), row),
                   pl.BlockSpec((tm, D_C), row),
                   pl.BlockSpec((1, HALO, D_B), lambda i: (i, 0, 0))] + [slab(w) for w in casts],
        out_shape=[jax.ShapeDtypeStruct((m, D_MODEL), BF16),
                   jax.ShapeDtypeStruct((m, D_A), BF16),
                   jax.ShapeDtypeStruct((m, D_B), BF16),
                   jax.ShapeDtypeStruct((m, D_C), BF16),
                   jax.ShapeDtypeStruct((m // tm, HALO, D_B), F32)]
                  + [jax.ShapeDtypeStruct(w.shape, BF16) for w in casts],
        scratch_shapes=[pltpu.VMEM((tm + HALO, D_B), F32),
                        pltpu.VMEM((tm, D_B), F32),
                        pltpu.VMEM((CONV_ROWS + HALO, IN_SUB), F32),
                        pltpu.VMEM((tm, D_A), F32),
                        pltpu.VMEM((tm, D_A), F32)],
        compiler_params=_params("arbitrary"),
        name="inproj_prompt",
    )(x, g_mix.reshape(1, D_MODEL), w_in, w_in, w_in, ln_v_g.reshape(1, D_A), ln_v_b.reshape(1, D_A), ws, bs,
      conv_w, conv_b.reshape(1, D_B), ln_b_g.reshape(1, D_B), ln_b_b.reshape(1, D_B), *casts)


def _inproj_sample_kernel(x_ref, gm_ref, wa_ref, wb_ref, wq_ref, lvg_ref, lvb_ref, ws_ref, bs_ref,
                          h_ref, a_ref, q_ref, glu_ref, vn_ref):
    nb, nl, _ = x_ref.shape
    m = nb * nl
    h = _rms(x_ref[...].reshape(m, D_MODEL), gm_ref[...]).astype(BF16)
    h_ref[...] = h
    zg = jax.nn.gelu(_dot(h, wa_ref[...]))
    vn = _layer_norm(zg[:, D_A:], lvg_ref[...], lvb_ref[...])
    vn_ref[...] = vn.reshape(nb, nl, D_A)
    _spatial_gate(zg[:, :D_A], vn.astype(BF16), ws_ref, bs_ref, a_ref, nl)
    z = _dot(h, wb_ref[...])
    glu_ref[...] = (z[:, :D_B] * jax.nn.sigmoid(z[:, D_B:])).reshape(nb, nl, D_B)
    q_ref[...] = _dot(h, wq_ref[...])


def _inproj_sample_call(x, g_mix, w_in, ln_v_g, ln_v_b, ws, bs):
    nb, nl, _ = x.shape
    m = nb * nl
    bt = 2 * CHUNK // nl
    const = lambda i: (0, 0)
    vec = lambda n: pl.BlockSpec((1, n), const)
    flat = lambda n: pl.BlockSpec((bt * nl, n), lambda i: (i, 0))
    return pl.pallas_call(
        _inproj_sample_kernel,
        grid=(nb // bt,),
        in_specs=[pl.BlockSpec((bt, nl, D_MODEL), lambda i: (i, 0, 0)), vec(D_MODEL)]
                 + _inproj_weight_specs(const) + [
                  vec(D_A), vec(D_A),
                  pl.BlockSpec((G_A, CHUNK, CHUNK), lambda i: (0, 0, 0)),
                  pl.BlockSpec((CHUNK, D_A), const)],
        out_specs=[flat(D_MODEL), flat(D_A), flat(D_C),
                   pl.BlockSpec((bt, nl, D_B), lambda i: (i, 0, 0)),
                   pl.BlockSpec((None, bt, nl, D_A), lambda i: (0, i, 0, 0))],
        out_shape=[jax.ShapeDtypeStruct((m, D_MODEL), BF16),
                   jax.ShapeDtypeStruct((m, D_A), BF16),
                   jax.ShapeDtypeStruct((m, D_C), F32),
                   jax.ShapeDtypeStruct((nb, nl, D_B), F32),
                   jax.ShapeDtypeStruct((1, nb, nl, D_A), F32)],
        compiler_params=_params("arbitrary"),
        name="inproj_sample",
    )(x, g_mix.reshape(1, D_MODEL), w_in, w_in, w_in, ln_v_g.reshape(1, D_A), ln_v_b.reshape(1, D_A), ws, bs)


def _convb_sample_kernel(hist_ref, x_ref, w_ref, cb_ref, lng_ref, lnb_ref, o_ref, nh_ref):
    bt, nl, _ = x_ref.shape
    nh = hist_ref.shape[0]
    new_rows = [x_ref[:, t, :] for t in range(nl)]
    window = lambda j: hist_ref[j] if j < nh else new_rows[j - nh]
    for t in range(nl):
        acc = jnp.broadcast_to(cb_ref[...], (bt, D_B))
        for k in range(CONV_B):
            acc = acc + w_ref[k:k + 1, :] * window(t + k)
        y = _layer_norm(acc, lng_ref[...], lnb_ref[...])
        o_ref[:, t, :] = y * jax.nn.sigmoid(y)
    for j in range(nh):
        nh_ref[j] = window(j + nl)


def _convb_sample_call(hist, l, x, conv_w, conv_b, ln_g, ln_b):
    nb, nl, _ = x.shape
    nh = hist.shape[1]
    bt = CONV_BT
    const = lambda i: (0, 0)
    state_spec = pl.BlockSpec((None, nh, bt, D_B), lambda i: (l, 0, i, 0))
    new_spec = pl.BlockSpec((bt, nl, D_B), lambda i: (i, 0, 0))
    return pl.pallas_call(
        _convb_sample_kernel,
        grid=(nb // bt,),
        in_specs=[state_spec, new_spec,
                  pl.BlockSpec((CONV_B, D_B), const),
                  pl.BlockSpec((1, D_B), const),
                  pl.BlockSpec((1, D_B), const),
                  pl.BlockSpec((1, D_B), const)],
        out_specs=[new_spec, pl.BlockSpec((None, nh, bt, D_B), lambda i: (0, 0, i, 0))],
        out_shape=[jax.ShapeDtypeStruct((nb, nl, D_B), F32),
                   jax.ShapeDtypeStruct((1, nh, nb, D_B), F32)],
        compiler_params=_params("parallel"),
        name="convb_sample",
    )(hist, x, conv_w, conv_b.reshape(1, D_B), ln_g.reshape(1, D_B), ln_b.reshape(1, D_B))


GATE_COLS = 1024


def _cast_gate_columns(gate_refs, gates_o):
    for n, g_ref in enumerate(gate_refs):
        gates_o[:, n * GATE_COLS:(n + 1) * GATE_COLS] = g_ref[...].astype(BF16)


def _gate_column_specs(first, count, rows, step):
    base = OFF_GATES // GATE_COLS + first
    return [pl.BlockSpec((rows, GATE_COLS), lambda *ids, c=c: (step(*ids), base + c)) for c in range(count)]


def _memkv_kernel(m_ref, g_ref, wk_ref, wv_ref, *refs):
    gate_refs = refs[:-7]
    k5_ref, v5_ref, kb_ref, vb_ref, gates_o, wkb_ref, wvb_ref = refs[-7:]
    _cast_gate_columns(gate_refs, gates_o)

    @pl.when(pl.program_id(0) == 0)
    def _():
        wkb_ref[...] = wk_ref[...].astype(BF16)
        wvb_ref[...] = wv_ref[...].astype(BF16)

    mn = _rms(m_ref[...], g_ref[...]).astype(BF16)
    k = _dot(mn, wkb_ref[...])
    v = _dot(mn, wvb_ref[...])
    k5_ref[...] = k.reshape(N_MEM, N_XHEADS, XHEAD_DIM)
    v5_ref[...] = v.reshape(N_MEM, N_XHEADS, XHEAD_DIM)
    kb_ref[...] = k.astype(BF16)
    vb_ref[...] = v.astype(BF16)


def _memkv_call(mem, g_mem, w_k, w_v, w_in):
    nb = mem.shape[0]
    const = lambda i: (0, 0)
    slab = D_MODEL // nb
    gate_specs = _gate_column_specs(0, D_MODEL // GATE_COLS, slab, lambda i: i)
    cache_spec = pl.BlockSpec((None, None, N_MEM, N_XHEADS, XHEAD_DIM), lambda i: (0, i, 0, 0, 0))
    seq_spec = pl.BlockSpec((None, N_MEM, D_C), lambda i: (i, 0, 0))
    cache_shape = jax.ShapeDtypeStruct((1, nb, N_MEM, N_XHEADS, XHEAD_DIM), F32)
    return pl.pallas_call(
        _memkv_kernel,
        grid=(nb,),
        in_specs=[pl.BlockSpec((None, N_MEM, D_MODEL), lambda i: (i, 0, 0)),
                  pl.BlockSpec((1, D_MODEL), const),
                  pl.BlockSpec((D_MODEL, D_C), const, pipeline_mode=pl.Buffered(1)),
                  pl.BlockSpec((D_MODEL, D_C), const, pipeline_mode=pl.Buffered(1))] + gate_specs,
        out_specs=[cache_spec, cache_spec, seq_spec, seq_spec,
                   pl.BlockSpec((slab, D_MODEL), lambda i: (i, 0))],
        out_shape=[cache_shape, cache_shape,
                   jax.ShapeDtypeStruct((nb, N_MEM, D_C), BF16),
                   jax.ShapeDtypeStruct((nb, N_MEM, D_C), BF16),
                   jax.ShapeDtypeStruct((D_MODEL, D_MODEL), BF16)],
        scratch_shapes=[pltpu.VMEM((D_MODEL, D_C), BF16), pltpu.VMEM((D_MODEL, D_C), BF16)],
        compiler_params=_params("arbitrary"),
        name="memory_kv",
    )(mem, g_mem.reshape(1, D_MODEL), w_k, w_v, *([w_in] * len(gate_specs)))


def _attn_prompt_kernel(q_ref, k_ref, v_ref, *refs):
    gate_refs, o_ref, gates_o = refs[:-2], refs[-2], refs[-1]
    _cast_gate_columns(gate_refs, gates_o)
    for h in range(N_XHEADS):
        cs = slice(h * XHEAD_DIM, (h + 1) * XHEAD_DIM)
        s = lax.dot_general(q_ref[:, cs], k_ref[:, cs], (((1,), (1,)), ((), ())), preferred_element_type=F32)
        p = _softmax(s * (XHEAD_DIM ** -0.5)).astype(BF16)
        o_ref[:, cs] = _dot(p, v_ref[:, cs]).astype(BF16)


def _attn_prompt_call(q, k, v, w_in):
    m = q.shape[0]
    nb = k.shape[0]
    tm = TM
    nt = m // nb // tm
    n_gate = 2 * D_MODEL
    slab = D_MODEL // (nb * nt)
    assert OFF_GATES % GATE_COLS == 0 and D_MODEL % GATE_COLS == 0 and D_MODEL % (nb * nt) == 0
    gate_specs = _gate_column_specs(D_MODEL // GATE_COLS, n_gate // GATE_COLS, slab, lambda b, t: b * nt + t)
    return pl.pallas_call(
        _attn_prompt_kernel,
        grid=(nb, nt),
        in_specs=[pl.BlockSpec((tm, D_C), lambda b, t: (b * nt + t, 0)),
                  pl.BlockSpec((None, N_MEM, D_C), lambda b, t: (b, 0, 0)),
                  pl.BlockSpec((None, N_MEM, D_C), lambda b, t: (b, 0, 0))] + gate_specs,
        out_specs=[pl.BlockSpec((tm, D_C), lambda b, t: (b * nt + t, 0)),
                   pl.BlockSpec((slab, n_gate), lambda b, t: (b * nt + t, 0))],
        out_shape=[jax.ShapeDtypeStruct((m, D_C), BF16),
                   jax.ShapeDtypeStruct((D_MODEL, n_gate), BF16)],
        compiler_params=_params("arbitrary", "arbitrary"),
        name="attn_prompt",
    )(q, k, v, *([w_in] * len(gate_specs)))


def _sample_attention(q_ref, k_ref, v_ref, o_ref):
    bt = k_ref.shape[0]
    nl = q_ref.shape[0] // bt
    rows = lax.broadcasted_iota(jnp.int32, (N_XHEADS * nl, N_MEM * N_XHEADS), 0)
    cols = lax.broadcasted_iota(jnp.int32, (N_XHEADS * nl, N_MEM * N_XHEADS), 1)
    same_head = (cols % N_XHEADS) == (rows // nl)
    outs = []
    for b in range(bt):
        q = q_ref[b * nl:(b + 1) * nl, :]
        qs = jnp.concatenate([q[:, h * XHEAD_DIM:(h + 1) * XHEAD_DIM] for h in range(N_XHEADS)], axis=0)
        k2 = k_ref[b].reshape(N_MEM * N_XHEADS, XHEAD_DIM).astype(BF16)
        v2 = v_ref[b].reshape(N_MEM * N_XHEADS, XHEAD_DIM).astype(BF16)
        s = lax.dot_general(qs.astype(BF16), k2, (((1,), (1,)), ((), ())), preferred_element_type=F32)
        p = _softmax(jnp.where(same_head, s * (XHEAD_DIM ** -0.5), -jnp.inf)).astype(BF16)
        o = _dot(p, v2)
        outs.append(jnp.concatenate([o[h * nl:(h + 1) * nl, :] for h in range(N_XHEADS)], axis=1))
    o_ref[...] = jnp.concatenate(outs, axis=0).astype(o_ref.dtype)


def _cast_ffn_weights(wu_f, wd_f, wa_o, wg_o, wd_o, step):
    wu = wu_f[...]
    zpad = jnp.zeros((wu.shape[0], D_FF_PAD - D_FF), BF16)
    wa_o[:, :D_FF] = wu[:, :D_FF].astype(BF16)
    wa_o[:, D_FF:] = zpad
    wg_o[:, :D_FF] = wu[:, D_FF:].astype(BF16)
    wg_o[:, D_FF:] = zpad
    live_blocks = D_FF // wd_f.shape[0]

    @pl.when(step < live_blocks)
    def _():
        wd_o[...] = wd_f[...].astype(BF16)

    @pl.when(step >= live_blocks)
    def _():
        wd_o[...] = jnp.zeros(wd_o.shape, BF16)


def _mix_kernel(*refs, side_jobs):
    h_ref, a_ref, b_ref, c_ref, wpa_ref, wpb_ref, wpc_ref, wga_ref, wgb_ref, wgc_ref = refs[:10]
    if side_jobs:
        wu_f, wd_f, qs_ref, ks_ref, vs_ref, o_ref, wa_o, wg_o, wd_o, os_ref = refs[10:]
        _cast_ffn_weights(wu_f, wd_f, wa_o, wg_o, wd_o, pl.program_id(0) * pl.num_programs(1) + pl.program_id(1))
        _sample_attention(qs_ref, ks_ref, vs_ref, os_ref)
    else:
        (o_ref,) = refs[10:]
    h = h_ref[...]
    b = b_ref[...].reshape(h.shape[0], D_B).astype(BF16)
    mix = jax.nn.sigmoid(_dot(h, wga_ref[...])) * _dot(a_ref[...], wpa_ref[...])
    mix = mix + jax.nn.sigmoid(_dot(h, wgb_ref[...])) * _dot(b, wpb_ref[...])
    mix = mix + jax.nn.sigmoid(_dot(h, wgc_ref[...])) * _dot(c_ref[...].astype(BF16), wpc_ref[...])
    o_ref[...] = mix.astype(BF16)


def _mix_call(h, a, b, c, w_pa, w_pb, w_pc, w_gate_a, w_gate_bc, name, ffn_weights=None, sample_attn=None):
    m = h.shape[0]
    tm = min(TM, m)
    tn = MIX_TN
    ni = m // tm
    nj = D_MODEL // tn
    per_gate = D_MODEL // tn
    row = lambda j, i: (i, 0)
    col = lambda j, i: (0, j)
    if b.ndim == 3:
        assert m == tm
        b_spec = pl.BlockSpec(b.shape, lambda j, i: (0, 0, 0))
    else:
        b_spec = pl.BlockSpec((tm, D_B), row)
    in_specs = [pl.BlockSpec((tm, D_MODEL), row),
                pl.BlockSpec((tm, D_A), row),
                b_spec,
                pl.BlockSpec((tm, D_C), row),
                pl.BlockSpec((D_A, tn), col),
                pl.BlockSpec((D_B, tn), col),
                pl.BlockSpec((D_C, tn), col),
                pl.BlockSpec((D_MODEL, tn), lambda j, i: (0, j)),
                pl.BlockSpec((D_MODEL, tn), lambda j, i: (0, j)),
                pl.BlockSpec((D_MODEL, tn), lambda j, i: (0, per_gate + j))]
    out_specs = [pl.BlockSpec((tm, tn), lambda j, i: (i, j))]
    out_shape = [jax.ShapeDtypeStruct((m, D_MODEL), BF16)]
    args = [h, a, b, c, w_pa, w_pb, w_pc, w_gate_a, w_gate_bc, w_gate_bc]
    if ffn_weights is not None:
        w_up, w_down = ffn_weights
        steps = ni * nj
        up_rows = D_MODEL // steps
        down_rows = CHUNK
        down_blocks = D_FF_PAD // down_rows
        assert D_MODEL % steps == 0 and D_FF % down_rows == 0 and down_blocks <= steps
        step = lambda j, i: j * ni + i
        in_specs += [pl.BlockSpec((up_rows, 2 * D_FF), lambda j, i: (step(j, i), 0)),
                     pl.BlockSpec((down_rows, D_MODEL),
                                  lambda j, i: (jnp.minimum(step(j, i), D_FF // down_rows - 1), 0))]
        out_specs += [pl.BlockSpec((up_rows, D_FF_PAD), lambda j, i: (step(j, i), 0)),
                      pl.BlockSpec((up_rows, D_FF_PAD), lambda j, i: (step(j, i), 0)),
                      pl.BlockSpec((down_rows, D_MODEL),
                                   lambda j, i: (jnp.minimum(step(j, i), down_blocks - 1), 0))]
        out_shape += [jax.ShapeDtypeStruct((D_MODEL, D_FF_PAD), BF16),
                      jax.ShapeDtypeStruct((D_MODEL, D_FF_PAD), BF16),
                      jax.ShapeDtypeStruct((D_FF_PAD, D_MODEL), BF16)]
        args += [w_up, w_down]
        q_s, cache_k, cache_v, l, nl = sample_attn
        bt = cache_k.shape[1] // steps
        assert cache_k.shape[1] % steps == 0 and (bt * nl) % SUBLANE == 0
        cache_spec = pl.BlockSpec((None, bt, N_MEM, N_XHEADS, XHEAD_DIM), lambda j, i: (l, step(j, i), 0, 0, 0))
        in_specs += [pl.BlockSpec((bt * nl, D_C), lambda j, i: (step(j, i), 0)), cache_spec, cache_spec]
        out_specs += [pl.BlockSpec((bt * nl, D_C), lambda j, i: (step(j, i), 0))]
        out_shape += [jax.ShapeDtypeStruct(q_s.shape, F32)]
        args += [q_s, cache_k, cache_v]
    return pl.pallas_call(
        functools.partial(_mix_kernel, side_jobs=ffn_weights is not None),
        grid=(nj, ni),
        in_specs=in_specs,
        out_specs=out_specs,
        out_shape=out_shape,
        compiler_params=_params("arbitrary", "arbitrary"),
        name=name,
    )(*args)


def _oproj_kernel(x_ref, mix_ref, wo_ref, g_ref, x1_ref, h2_ref):
    x1 = x_ref[...] + _dot(mix_ref[...], wo_ref[...])
    x1_ref[...] = x1
    h2_ref[...] = _rms(x1, g_ref[...]).astype(BF16)


def _oproj_call(x, mix, w_o, g_ffn):
    m = mix.shape[0]
    tm = TM
    row = lambda i: (i, 0)
    return pl.pallas_call(
        _oproj_kernel,
        grid=(m // tm,),
        in_specs=[pl.BlockSpec((tm, D_MODEL), row),
                  pl.BlockSpec((tm, D_MODEL), row),
                  pl.BlockSpec((D_MODEL, D_MODEL), lambda i: (0, 0)),
                  pl.BlockSpec((1, D_MODEL), lambda i: (0, 0))],
        out_specs=[pl.BlockSpec((tm, D_MODEL), row),
                   pl.BlockSpec((tm, D_MODEL), row)],
        out_shape=[jax.ShapeDtypeStruct((m, D_MODEL), F32),
                   jax.ShapeDtypeStruct((m, D_MODEL), BF16)],
        compiler_params=_params("parallel"),
        name="oproj_prompt",
    )(x, mix, w_o, g_ffn.reshape(1, D_MODEL))


def _oproj_sample_kernel(x_ref, mix_ref, wo_ref, g_ref, x1_ref, h2_ref, x3_ref):
    nb, nl, _ = x_ref.shape
    x3_ref[...] = x_ref[...] + _dot(mix_ref[...], wo_ref[...]).reshape(nb, nl, D_MODEL)
    x1 = jnp.concatenate([x3_ref[:, t, :] for t in range(nl)], axis=0)
    x1_ref[...] = x1
    h2_ref[...] = _rms(x1, g_ref[...]).astype(BF16)


def _oproj_sample_call(x, mix, w_o, g_ffn):
    nb, nl, _ = x.shape
    m = nb * nl
    const = lambda i: (0, 0)
    single = pl.Buffered(1)
    return pl.pallas_call(
        _oproj_sample_kernel,
        grid=(1,),
        in_specs=[pl.BlockSpec((nb, nl, D_MODEL), lambda i: (0, 0, 0), pipeline_mode=single),
                  pl.BlockSpec((m, D_MODEL), const, pipeline_mode=single),
                  pl.BlockSpec((D_MODEL, D_MODEL), const, pipeline_mode=single),
                  pl.BlockSpec((1, D_MODEL), const)],
        out_specs=[pl.BlockSpec((m, D_MODEL), const),
                   pl.BlockSpec((m, D_MODEL), const)],
        out_shape=[jax.ShapeDtypeStruct((m, D_MODEL), F32),
                   jax.ShapeDtypeStruct((m, D_MODEL), BF16)],
        scratch_shapes=[pltpu.VMEM((nb, nl, D_MODEL), F32)],
        compiler_params=_params("arbitrary"),
        name="oproj_sample",
    )(x, mix, w_o, g_ffn.reshape(1, D_MODEL))


FFN_TM = 1024
FFN_SUB = 256


def _ffn_hidden(h, wa_ref, wg_ref, cw_ref, cb_ref, fx_ref, base, lag):
    tm = h.shape[0]
    ps = []
    for c in range(wa_ref.shape[1] // FFN_SUB):
        cs = slice(c * FFN_SUB, (c + 1) * FFN_SUB)
        fa = _dot(h, wa_ref[:, cs])
        fg = _dot(h, wg_ref[:, cs])
        fx_ref[base:base + tm, cs] = fa
        prev1 = fx_ref[pl.ds(base - lag, tm), cs]
        prev2 = fx_ref[pl.ds(base - 2 * lag, tm), cs]
        fc = cw_ref[2:3, cs] * fa + cw_ref[1:2, cs] * prev1 + cw_ref[0:1, cs] * prev2 + cb_ref[:, cs]
        ps.append((jax.nn.gelu(fc) * fg).astype(BF16))
    return jnp.concatenate(ps, axis=1)


def _ffn_prompt_kernel(h_ref, x1_hbm, wa_ref, wg_ref, wd_ref, cw_ref, cb_ref, g_ref, y_ref, tail_ref,
                       fx_ref, carry_ref, x1_ref, x1_sem, *, nt):
    tm = h_ref.shape[0]
    tf = wa_ref.shape[1]
    i = pl.program_id(0)
    j = pl.program_id(1)
    x1_copy = pltpu.make_async_copy(x1_hbm.at[pl.ds(i * tm, tm), :], x1_ref, x1_sem)

    @pl.when(j == 0)
    def _():
        x1_copy.start()
        y_ref[...] = jnp.zeros(y_ref.shape, F32)

    @pl.when(i % nt == 0)
    def _():
        fx_ref[0:SUBLANE, :] = jnp.zeros((SUBLANE, tf), F32)

    @pl.when(i % nt != 0)
    def _():
        fx_ref[0:SUBLANE, :] = carry_ref[j]

    p = _ffn_hidden(h_ref[...], wa_ref, wg_ref, cw_ref, cb_ref, fx_ref, SUBLANE, 1)
    last = fx_ref[tm:tm + SUBLANE, :]
    carry_ref[j] = last
    tail_ref[0] = last
    y_ref[...] += _dot(p, wd_ref[...])

    @pl.when(j == pl.num_programs(1) - 1)
    def _():
        x1_copy.wait()

        def norm_rows(r, carry):
            rs = pl.ds(pl.multiple_of(r * CHUNK, CHUNK), CHUNK)
            y_ref[rs, :] = _rms(x1_ref[rs, :] + y_ref[rs, :], g_ref[...])
            return carry

        lax.fori_loop(0, tm // CHUNK, norm_rows, 0)


def _ffn_prompt_call(h2, x1, wa, wg, wd, cw, cb, g_final, nb):
    m = h2.shape[0]
    tm = FFN_TM
    nt = m // nb // tm
    nj = D_FF_PAD // TF
    row = lambda i, j: (i, 0)
    col = lambda i, j: (0, j)
    return pl.pallas_call(
        functools.partial(_ffn_prompt_kernel, nt=nt),
        grid=(m // tm, nj),
        in_specs=[pl.BlockSpec((tm, D_MODEL), row),
                  pl.BlockSpec(memory_space=pl.ANY),
                  pl.BlockSpec((D_MODEL, TF), col),
                  pl.BlockSpec((D_MODEL, TF), col),
                  pl.BlockSpec((TF, D_MODEL), lambda i, j: (j, 0)),
                  pl.BlockSpec((CONV_F, TF), col),
                  pl.BlockSpec((1, TF), col),
                  pl.BlockSpec((1, D_MODEL), lambda i, j: (0, 0))],
        out_specs=[pl.BlockSpec((tm, D_MODEL), row),
                   pl.BlockSpec((1, SUBLANE, TF), lambda i, j: (i, 0, j))],
        out_shape=[jax.ShapeDtypeStruct((m, D_MODEL), F32),
                   jax.ShapeDtypeStruct((m // tm, SUBLANE, D_FF_PAD), F32)],
        scratch_shapes=[pltpu.VMEM((tm + SUBLANE, TF), F32),
                        pltpu.VMEM((nj, SUBLANE, TF), F32),
                        pltpu.VMEM((tm, D_MODEL), F32),
                        pltpu.SemaphoreType.DMA(())],
        compiler_params=_params("arbitrary", "arbitrary"),
        name="ffn_prompt",
    )(h2, x1, wa, wg, wd, cw, cb, g_final.reshape(1, D_MODEL))


def _ffn_sample_kernel(h_ref, x1_ref, wa_ref, wg_ref, wd_ref, cw_ref, cb_ref, g_ref, hist_ref,
                       y_ref, st_ref, fx_ref, acc_ref):
    nb = hist_ref.shape[0]
    tm = h_ref.shape[0]
    tf = wa_ref.shape[1]
    j = pl.program_id(1)
    live = (j * tf + lax.broadcasted_iota(jnp.int32, (nb, tf), 1)) < D_FF
    for r in range(CONV_F - 1):
        fx_ref[r * nb:(r + 1) * nb, :] = jnp.where(live, hist_ref[:, r, :], 0.0)
    base = (CONV_F - 1) * nb
    p = _ffn_hidden(h_ref[...], wa_ref, wg_ref, cw_ref, cb_ref, fx_ref, base, nb)
    for r in range(CONV_F - 1):
        st_ref[:, r, :] = fx_ref[tm + r * nb:tm + (r + 1) * nb, :]

    @pl.when(j == 0)
    def _():
        acc_ref[...] = x1_ref[...]

    acc_ref[...] += _dot(p, wd_ref[...])

    @pl.when(j == pl.num_programs(1) - 1)
    def _():
        for t in range(tm // nb):
            y_ref[:, t, :] = _rms(acc_ref[t * nb:(t + 1) * nb, :], g_ref[...])


def _ffn_sample_call(h2, x1, wa, wg, wd, cw, cb, g_final, state, l, nb):
    m = h2.shape[0]
    nj = D_FF_PAD // TF
    row = lambda i, j: (i, 0)
    col = lambda i, j: (0, j)
    return pl.pallas_call(
        _ffn_sample_kernel,
        grid=(1, nj),
        in_specs=[pl.BlockSpec((m, D_MODEL), row, pipeline_mode=pl.Buffered(1)),
                  pl.BlockSpec((m, D_MODEL), row, pipeline_mode=pl.Buffered(1)),
                  pl.BlockSpec((D_MODEL, TF), col),
                  pl.BlockSpec((D_MODEL, TF), col),
                  pl.BlockSpec((TF, D_MODEL), lambda i, j: (j, 0)),
                  pl.BlockSpec((CONV_F, TF), col),
                  pl.BlockSpec((1, TF), col),
                  pl.BlockSpec((1, D_MODEL), lambda i, j: (0, 0)),
                  pl.BlockSpec((None, nb, CONV_F - 1, TF), lambda i, j: (l, 0, 0, j))],
        out_specs=[pl.BlockSpec((nb, m // nb, D_MODEL), lambda i, j: (0, 0, 0)),
                   pl.BlockSpec((None, nb, CONV_F - 1, TF), lambda i, j: (0, 0, 0, j))],
        out_shape=[jax.ShapeDtypeStruct((nb, m // nb, D_MODEL), F32),
                   jax.ShapeDtypeStruct((1, nb, CONV_F - 1, D_FF), F32)],
        scratch_shapes=[pltpu.VMEM(((CONV_F - 1) * nb + m, TF), F32),
                        pltpu.VMEM((m, D_MODEL), F32)],
        compiler_params=_params("arbitrary", "arbitrary"),
        name="ffn_sample",
    )(h2, x1, wa, wg, wd, cw, cb, g_final.reshape(1, D_MODEL), state)


def kernel(x_prompt, x_sample, mem_prompt, cache_mem_k, cache_mem_v, state_conv, state_ffn_conv, g_mix, w_in, ln_v_g, ln_v_b, w_s, b_s, w_pa, conv_w, conv_b, ln_b_g, ln_b_b, w_pb, g_mem, w_k, w_v, w_pc, w_o, g_ffn, w_up, ffn_conv_w, ffn_conv_b, w_down, g_final):
    depth = g_mix.shape[0]
    assert depth == 1
    l = 0
    nbp, lp, _ = x_prompt.shape
    nbs, ls, _ = x_sample.shape
    mp = nbp * lp
    pad_ff = D_FF_PAD - D_FF

    w_in_b = w_in[l][:, :OFF_GATES].astype(BF16)
    cw = jnp.pad(ffn_conv_w[l], ((0, 0), (0, pad_ff)))
    cb = jnp.pad(ffn_conv_b[l], (0, pad_ff)).reshape(1, D_FF_PAD)

    reps = CHUNK // ls
    ws_p = w_s[l]
    bs_p = jnp.repeat(b_s[l].T, D_A // G_A, axis=1)
    bs_s = jnp.repeat(jnp.tile(b_s[l][:, :ls], (1, reps)).T, D_A // G_A, axis=1)

    h_s, a_s, q_s, glu_s, vn_s = _inproj_sample_call(x_sample, g_mix[l], w_in_b, ln_v_g[l], ln_v_b[l], ws_p, bs_s)

    xp = x_prompt.reshape(mp, D_MODEL)
    h, a, bact, q, ctail, w_pa_b, w_pb_b, w_pc_b, w_o_b = _inproj_prompt_call(
        xp, nbp, g_mix[l], w_in_b, ln_v_g[l], ln_v_b[l], ws_p, bs_p, conv_w[l], conv_b[l], ln_b_g[l], ln_b_b[l],
        (w_pa[l], w_pb[l], w_pc[l], w_o[l]))
    k5, v5, kb, vb, w_gate_a = _memkv_call(mem_prompt, g_mem[l], w_k[l], w_v[l], w_in[l])
    cact, w_gate_bc = _attn_prompt_call(q, kb, vb, w_in[l])
    mix, w_up_a, w_up_g, w_down_b, cact_s = _mix_call(
        h, a, bact, cact, w_pa_b, w_pb_b, w_pc_b, w_gate_a, w_gate_bc, "mix_prompt",
        (w_up[l], w_down[l]), (q_s, cache_mem_k, cache_mem_v, l, ls))
    x1, h2 = _oproj_call(xp, mix, w_o_b, g_ffn[l])
    yp, tail = _ffn_prompt_call(h2, x1, w_up_a, w_up_g, w_down_b, cw, cb, g_final, nbp)
    nt = tail.shape[0] // nbp
    ffn_p = tail.reshape(nbp, nt, SUBLANE, D_FF_PAD)[:, nt - 1, SUBLANE - (CONV_F - 1):, :D_FF]
    nt = ctail.shape[0] // nbp
    conv_p = ctail.reshape(nbp, nt, HALO, D_B)[:, nt - 1, HALO - (CONV_B - 1):]

    bact, conv_s = _convb_sample_call(jnp.swapaxes(state_conv, 1, 2), l, glu_s, conv_w[l], conv_b[l],
                                      ln_b_g[l], ln_b_b[l])
    conv_s = jnp.swapaxes(conv_s, 1, 2)
    (mix,) = _mix_call(h_s, a_s, bact, cact_s, w_pa_b, w_pb_b, w_pc_b, w_gate_a, w_gate_bc, "mix_sample")
    x1, h2 = _oproj_sample_call(x_sample, mix, w_o_b, g_ffn[l])
    ys, ffn_s = _ffn_sample_call(h2, x1, w_up_a, w_up_g, w_down_b, cw, cb, g_final, state_ffn_conv, l, nbs)

    return (yp.reshape(nbp, lp, D_MODEL), ys, k5, v5, conv_p[None], ffn_p[None],
            conv_s, ffn_s, vn_s)
```

```python
import functools

import jax
import jax.numpy as jnp
from jax import lax
from jax.experimental import pallas as pl
from jax.experimental.pallas import tpu as pltpu

F32 = jnp.float32
BF16 = jnp.bfloat16

D_MODEL = 2048
CHUNK = 128
D_A = D_MODEL // 2
G_A = 8
D_B = D_MODEL // 2
CONV_B = 31
N_MEM = 256
N_XHEADS = 4
XHEAD_DIM = D_MODEL // 8
D_C = N_XHEADS * XHEAD_DIM
D_FF = ((8 * D_MODEL // 3 + 127) // 128) * 128
CONV_F = 3
EPS = 1e-6

OFF_ZA = 0
OFF_ZB = 2 * D_A
OFF_Q = OFF_ZB + 2 * D_B
OFF_GATES = OFF_Q + D_C

LANE = 128
SUBLANE = 8
TM = 512
TF = 512
D_FF_PAD = ((D_FF + TF - 1) // TF) * TF
HALO = 32
MIX_TN = 512
CONV_BT = 16
VMEM_LIMIT = 60 * 1024 * 1024


def _params(*sem):
    return pltpu.CompilerParams(dimension_semantics=sem, vmem_limit_bytes=VMEM_LIMIT)


def _rms(x, g):
    return x * lax.rsqrt(jnp.mean(x * x, axis=-1, keepdims=True) + EPS) * g


def _layer_norm(x, g, b):
    mu = jnp.mean(x, axis=-1, keepdims=True)
    d = x - mu
    var = jnp.mean(d * d, axis=-1, keepdims=True)
    return d * lax.rsqrt(var + EPS) * g + b


def _dot(a, b):
    return jnp.dot(a, b, preferred_element_type=F32)


def _softmax(s):
    e = jnp.exp(s - jnp.max(s, axis=-1, keepdims=True))
    return e / jnp.sum(e, axis=-1, keepdims=True)


def _spatial_gate(u, vb, ws_ref, bs_ref, a_ref, sample):
    tm = vb.shape[0]
    r = lax.broadcasted_iota(jnp.int32, (CHUNK, CHUNK), 0)
    c = lax.broadcasted_iota(jnp.int32, (CHUNK, CHUNK), 1)
    mask = r >= c
    if sample:
        nl = sample
        mask = mask & ((r // nl) == (c // nl))
        pick_rows = jnp.where(c == r % nl, 1.0, 0.0).astype(BF16)
        pick_cols = jnp.where(r == c % nl, 1.0, 0.0).astype(BF16)
    for g in range(G_A):
        cs = slice(g * LANE, (g + 1) * LANE)
        w = ws_ref[g]
        if sample:
            w = _dot(_dot(pick_rows, w.astype(BF16)).astype(BF16), pick_cols)
        wm = jnp.where(mask, w, 0.0).astype(BF16)
        for ch in range(tm // CHUNK):
            rs = slice(ch * CHUNK, (ch + 1) * CHUNK)
            s = _dot(wm, vb[rs, cs]) + bs_ref[:, cs]
            a_ref[rs, cs] = (u[rs, cs] * s).astype(BF16)


IN_SUB = 256
CONV_ROWS = 128


def _conv_taps(xc_ref, sh_ref, w_ref, bias, r0, cs):
    lead = HALO - (CONV_B - 1)
    acc = jnp.broadcast_to(bias, (CONV_ROWS, bias.shape[1]))
    for s in range(SUBLANE):
        qs = [q for q in range(HALO // SUBLANE + 1) if lead <= SUBLANE * q + s < lead + CONV_B]
        n = CONV_ROWS + SUBLANE * qs[-1]
        if s:
            sh_ref[0:n, :] = xc_ref[pl.ds(r0 + s, n), cs]
        for q in qs:
            k = SUBLANE * q + s - lead
            if s:
                win = sh_ref[SUBLANE * q:SUBLANE * q + CONV_ROWS, :]
            else:
                win = xc_ref[r0 + SUBLANE * q:r0 + SUBLANE * q + CONV_ROWS, cs]
            acc = acc + w_ref[k:k + 1, cs] * win
    return acc


def _inproj_prompt_kernel(x_ref, gm_ref, wa_ref, wb_ref, wq_ref, lvg_ref, lvb_ref, ws_ref, bs_ref,
                          cw_ref, cb_ref, lbg_ref, lbb_ref, wpa_f, wpb_f, wpc_f, wo_f,
                          h_ref, a_ref, b_ref, q_ref, ctail_ref, wpa_o, wpb_o, wpc_o, wo_o,
                          xc_ref, cv_ref, sh_ref, u_ref, v_ref, *, nt):
    tm = x_ref.shape[0]
    t = pl.program_id(0) % nt
    for src, dst in ((wpa_f, wpa_o), (wpb_f, wpb_o), (wpc_f, wpc_o), (wo_f, wo_o)):
        dst[...] = src[...].astype(BF16)

    @pl.when(t == 0)
    def _():
        xc_ref[0:HALO, :] = jnp.zeros((HALO, D_B), F32)

    @pl.when(t != 0)
    def _():
        xc_ref[0:HALO, :] = xc_ref[tm:tm + HALO, :]

    h = _rms(x_ref[...], gm_ref[...]).astype(BF16)
    h_ref[...] = h

    for c in range(D_B // IN_SUB):
        cs = slice(c * IN_SUB, (c + 1) * IN_SUB)
        za = _dot(h, wb_ref[:, cs])
        zb = _dot(h, wb_ref[:, D_B + c * IN_SUB:D_B + (c + 1) * IN_SUB])
        xc_ref[HALO:, cs] = za * jax.nn.sigmoid(zb)
        for rb in range(tm // CONV_ROWS):
            r0 = rb * CONV_ROWS
            cv_ref[r0:r0 + CONV_ROWS, cs] = _conv_taps(xc_ref, sh_ref, cw_ref, cb_ref[:, cs], r0, cs)
    ctail_ref[0] = xc_ref[tm:tm + HALO, :]
    y = _layer_norm(cv_ref[...], lbg_ref[...], lbb_ref[...])
    b_ref[...] = (y * jax.nn.sigmoid(y)).astype(BF16)

    for c in range(D_A // IN_SUB):
        cs = slice(c * IN_SUB, (c + 1) * IN_SUB)
        u_ref[:, cs] = jax.nn.gelu(_dot(h, wa_ref[:, cs]))
        v_ref[:, cs] = jax.nn.gelu(_dot(h, wa_ref[:, D_A + c * IN_SUB:D_A + (c + 1) * IN_SUB]))
    q_ref[...] = _dot(h, wq_ref[...]).astype(BF16)
    vb = _layer_norm(v_ref[...], lvg_ref[...], lvb_ref[...]).astype(BF16)
    _spatial_gate(u_ref, vb, ws_ref, bs_ref, a_ref, 0)


def _inproj_weight_specs(const_map):
    single = pl.Buffered(1)
    return [pl.BlockSpec((D_MODEL, 2 * D_A), lambda *i: (0, OFF_ZA // (2 * D_A)), pipeline_mode=single),
            pl.BlockSpec((D_MODEL, 2 * D_B), lambda *i: (0, OFF_ZB // (2 * D_B)), pipeline_mode=single),
            pl.BlockSpec((D_MODEL, D_C), lambda *i: (0, OFF_Q // D_C), pipeline_mode=single)]


def _inproj_prompt_call(x, nb, g_mix, w_in, ln_v_g, ln_v_b, ws, bs, conv_w, conv_b, ln_b_g, ln_b_b, casts):
    m = x.shape[0]
    tm = TM
    nt = m // nb // tm
    steps = m // tm
    slab = lambda w: pl.BlockSpec((w.shape[0] // steps, w.shape[1]), lambda i: (i, 0))
    const = lambda i: (0, 0)
    row = lambda i: (i, 0)
    vec = lambda n: pl.BlockSpec((1, n), const)
    return pl.pallas_call(
        functools.partial(_inproj_prompt_kernel, nt=nt),
        grid=(m // tm,),
        in_specs=[pl.BlockSpec((tm, D_MODEL), row), vec(D_MODEL)] + _inproj_weight_specs(const) + [
                  vec(D_A), vec(D_A),
                  pl.BlockSpec((G_A, CHUNK, CHUNK), lambda i: (0, 0, 0)),
                  pl.BlockSpec((CHUNK, D_A), const),
                  pl.BlockSpec((CONV_B, D_B), const),
                  vec(D_B), vec(D_B), vec(D_B)] + [slab(w) for w in casts],
        out_specs=[pl.BlockSpec((tm, D_MODEL), row),
                   pl.BlockSpec((tm, D_A), row),
                   pl.BlockSpec((tm, D_B), row),
                   pl.BlockSpec((tm, D_C), row),
                   pl.BlockSpec((1, HALO, D_B), lambda i: (i, 0, 0))] + [slab(w) for w in casts],
        out_shape=[jax.ShapeDtypeStruct((m, D_MODEL), BF16),
                   jax.ShapeDtypeStruct((m, D_A), BF16),
                   jax.ShapeDtypeStruct((m, D_B), BF16),
                   jax.ShapeDtypeStruct((m, D_C), BF16),
                   jax.ShapeDtypeStruct((m // tm, HALO, D_B), F32)]
                  + [jax.ShapeDtypeStruct(w.shape, BF16) for w in casts],
        scratch_shapes=[pltpu.VMEM((tm + HALO, D_B), F32),
                        pltpu.VMEM((tm, D_B), F32),
                        pltpu.VMEM((CONV_ROWS + HALO, IN_SUB), F32),
                        pltpu.VMEM((tm, D_A), F32),
                        pltpu.VMEM((tm, D_A), F32)],
        compiler_params=_params("arbitrary"),
        name="inproj_prompt",
    )(x, g_mix.reshape(1, D_MODEL), w_in, w_in, w_in, ln_v_g.reshape(1, D_A), ln_v_b.reshape(1, D_A), ws, bs,
      conv_w, conv_b.reshape(1, D_B), ln_b_g.reshape(1, D_B), ln_b_b.reshape(1, D_B), *casts)


def _inproj_sample_kernel(x_ref, gm_ref, wa_ref, wb_ref, wq_ref, lvg_ref, lvb_ref, ws_ref, bs_ref,
                          h_ref, a_ref, q_ref, glu_ref, vn_ref):
    nb, nl, _ = x_ref.shape
    m = nb * nl
    h = _rms(x_ref[...].reshape(m, D_MODEL), gm_ref[...]).astype(BF16)
    h_ref[...] = h
    zg = jax.nn.gelu(_dot(h, wa_ref[...]))
    vn = _layer_norm(zg[:, D_A:], lvg_ref[...], lvb_ref[...])
    vn_ref[...] = vn.reshape(nb, nl, D_A)
    _spatial_gate(zg[:, :D_A], vn.astype(BF16), ws_ref, bs_ref, a_ref, nl)
    z = _dot(h, wb_ref[...])
    glu_ref[...] = (z[:, :D_B] * jax.nn.sigmoid(z[:, D_B:])).reshape(nb, nl, D_B)
    q_ref[...] = _dot(h, wq_ref[...])


def _inproj_sample_call(x, g_mix, w_in, ln_v_g, ln_v_b, ws, bs):
    nb, nl, _ = x.shape
    m = nb * nl
    bt = 2 * CHUNK // nl
    const = lambda i: (0, 0)
    vec = lambda n: pl.BlockSpec((1, n), const)
    flat = lambda n: pl.BlockSpec((bt * nl, n), lambda i: (i, 0))
    return pl.pallas_call(
        _inproj_sample_kernel,
        grid=(nb // bt,),
        in_specs=[pl.BlockSpec((bt, nl, D_MODEL), lambda i: (i, 0, 0)), vec(D_MODEL)]
                 + _inproj_weight_specs(const) + [
                  vec(D_A), vec(D_A),
                  pl.BlockSpec((G_A, CHUNK, CHUNK), lambda i: (0, 0, 0)),
                  pl.BlockSpec((CHUNK, D_A), const)],
        out_specs=[flat(D_MODEL), flat(D_A), flat(D_C),
                   pl.BlockSpec((bt, nl, D_B), lambda i: (i, 0, 0)),
                   pl.BlockSpec((None, bt, nl, D_A), lambda i: (0, i, 0, 0))],
        out_shape=[jax.ShapeDtypeStruct((m, D_MODEL), BF16),
                   jax.ShapeDtypeStruct((m, D_A), BF16),
                   jax.ShapeDtypeStruct((m, D_C), F32),
                   jax.ShapeDtypeStruct((nb, nl, D_B), F32),
                   jax.ShapeDtypeStruct((1, nb, nl, D_A), F32)],
        compiler_params=_params("arbitrary"),
        name="inproj_sample",
    )(x, g_mix.reshape(1, D_MODEL), w_in, w_in, w_in, ln_v_g.reshape(1, D_A), ln_v_b.reshape(1, D_A), ws, bs)


def _convb_sample_kernel(hist_ref, x_ref, w_ref, cb_ref, lng_ref, lnb_ref, o_ref, nh_ref):
    bt, nl, _ = x_ref.shape
    nh = hist_ref.shape[0]
    new_rows = [x_ref[:, t, :] for t in range(nl)]
    window = lambda j: hist_ref[j] if j < nh else new_rows[j - nh]
    for t in range(nl):
        acc = jnp.broadcast_to(cb_ref[...], (bt, D_B))
        for k in range(CONV_B):
            acc = acc + w_ref[k:k + 1, :] * window(t + k)
        y = _layer_norm(acc, lng_ref[...], lnb_ref[...])
        o_ref[:, t, :] = y * jax.nn.sigmoid(y)
    for j in range(nh):
        nh_ref[j] = window(j + nl)


def _convb_sample_call(hist, l, x, conv_w, conv_b, ln_g, ln_b):
    nb, nl, _ = x.shape
    nh = hist.shape[1]
    bt = CONV_BT
    const = lambda i: (0, 0)
    state_spec = pl.BlockSpec((None, nh, bt, D_B), lambda i: (l, 0, i, 0))
    new_spec = pl.BlockSpec((bt, nl, D_B), lambda i: (i, 0, 0))
    return pl.pallas_call(
        _convb_sample_kernel,
        grid=(nb // bt,),
        in_specs=[state_spec, new_spec,
                  pl.BlockSpec((CONV_B, D_B), const),
                  pl.BlockSpec((1, D_B), const),
                  pl.BlockSpec((1, D_B), const),
                  pl.BlockSpec((1, D_B), const)],
        out_specs=[new_spec, pl.BlockSpec((None, nh, bt, D_B), lambda i: (0, 0, i, 0))],
        out_shape=[jax.ShapeDtypeStruct((nb, nl, D_B), F32),
                   jax.ShapeDtypeStruct((1, nh, nb, D_B), F32)],
        compiler_params=_params("parallel"),
        name="convb_sample",
    )(hist, x, conv_w, conv_b.reshape(1, D_B), ln_g.reshape(1, D_B), ln_b.reshape(1, D_B))


GATE_COLS = 1024


def _cast_gate_columns(gate_refs, gates_o):
    for n, g_ref in enumerate(gate_refs):
        gates_o[:, n * GATE_COLS:(n + 1) * GATE_COLS] = g_ref[...].astype(BF16)


def _gate_column_specs(first, count, rows, step):
    base = OFF_GATES // GATE_COLS + first
    return [pl.BlockSpec((rows, GATE_COLS), lambda *ids, c=c: (step(*ids), base + c)) for c in range(count)]


def _memkv_kernel(m_ref, g_ref, wk_ref, wv_ref, *refs):
    gate_refs = refs[:-7]
    k5_ref, v5_ref, kb_ref, vb_ref, gates_o, wkb_ref, wvb_ref = refs[-7:]
    _cast_gate_columns(gate_refs, gates_o)

    @pl.when(pl.program_id(0) == 0)
    def _():
        wkb_ref[...] = wk_ref[...].astype(BF16)
        wvb_ref[...] = wv_ref[...].astype(BF16)

    mn = _rms(m_ref[...], g_ref[...]).astype(BF16)
    k = _dot(mn, wkb_ref[...])
    v = _dot(mn, wvb_ref[...])
    k5_ref[...] = k.reshape(N_MEM, N_XHEADS, XHEAD_DIM)
    v5_ref[...] = v.reshape(N_MEM, N_XHEADS, XHEAD_DIM)
    kb_ref[...] = k.astype(BF16)
    vb_ref[...] = v.astype(BF16)


def _memkv_call(mem, g_mem, w_k, w_v, w_in):
    nb = mem.shape[0]
    const = lambda i: (0, 0)
    slab = D_MODEL // nb
    gate_specs = _gate_column_specs(0, D_MODEL // GATE_COLS, slab, lambda i: i)
    cache_spec = pl.BlockSpec((None, None, N_MEM, N_XHEADS, XHEAD_DIM), lambda i: (0, i, 0, 0, 0))
    seq_spec = pl.BlockSpec((None, N_MEM, D_C), lambda i: (i, 0, 0))
    cache_shape = jax.ShapeDtypeStruct((1, nb, N_MEM, N_XHEADS, XHEAD_DIM), F32)
    return pl.pallas_call(
        _memkv_kernel,
        grid=(nb,),
        in_specs=[pl.BlockSpec((None, N_MEM, D_MODEL), lambda i: (i, 0, 0)),
                  pl.BlockSpec((1, D_MODEL), const),
                  pl.BlockSpec((D_MODEL, D_C), const, pipeline_mode=pl.Buffered(1)),
                  pl.BlockSpec((D_MODEL, D_C), const, pipeline_mode=pl.Buffered(1))] + gate_specs,
        out_specs=[cache_spec, cache_spec, seq_spec, seq_spec,
                   pl.BlockSpec((slab, D_MODEL), lambda i: (i, 0))],
        out_shape=[cache_shape, cache_shape,
                   jax.ShapeDtypeStruct((nb, N_MEM, D_C), BF16),
                   jax.ShapeDtypeStruct((nb, N_MEM, D_C), BF16),
                   jax.ShapeDtypeStruct((D_MODEL, D_MODEL), BF16)],
        scratch_shapes=[pltpu.VMEM((D_MODEL, D_C), BF16), pltpu.VMEM((D_MODEL, D_C), BF16)],
        compiler_params=_params("arbitrary"),
        name="memory_kv",
    )(mem, g_mem.reshape(1, D_MODEL), w_k, w_v, *([w_in] * len(gate_specs)))


def _attn_prompt_kernel(q_ref, k_ref, v_ref, *refs):
    gate_refs, o_ref, gates_o = refs[:-2], refs[-2], refs[-1]
    _cast_gate_columns(gate_refs, gates_o)
    for h in range(N_XHEADS):
        cs = slice(h * XHEAD_DIM, (h + 1) * XHEAD_DIM)
        s = lax.dot_general(q_ref[:, cs], k_ref[:, cs], (((1,), (1,)), ((), ())), preferred_element_type=F32)
        p = _softmax(s * (XHEAD_DIM ** -0.5)).astype(BF16)
        o_ref[:, cs] = _dot(p, v_ref[:, cs]).astype(BF16)


def _attn_prompt_call(q, k, v, w_in):
    m = q.shape[0]
    nb = k.shape[0]
    tm = TM
    nt = m // nb // tm
    n_gate = 2 * D_MODEL
    slab = D_MODEL // (nb * nt)
    assert OFF_GATES % GATE_COLS == 0 and D_MODEL % GATE_COLS == 0 and D_MODEL % (nb * nt) == 0
    gate_specs = _gate_column_specs(D_MODEL // GATE_COLS, n_gate // GATE_COLS, slab, lambda b, t: b * nt + t)
    return pl.pallas_call(
        _attn_prompt_kernel,
        grid=(nb, nt),
        in_specs=[pl.BlockSpec((tm, D_C), lambda b, t: (b * nt + t, 0)),
                  pl.BlockSpec((None, N_MEM, D_C), lambda b, t: (b, 0, 0)),
                  pl.BlockSpec((None, N_MEM, D_C), lambda b, t: (b, 0, 0))] + gate_specs,
        out_specs=[pl.BlockSpec((tm, D_C), lambda b, t: (b * nt + t, 0)),
                   pl.BlockSpec((slab, n_gate), lambda b, t: (b * nt + t, 0))],
        out_shape=[jax.ShapeDtypeStruct((m, D_C), BF16),
                   jax.ShapeDtypeStruct((D_MODEL, n_gate), BF16)],
        compiler_params=_params("arbitrary", "arbitrary"),
        name="attn_prompt",
    )(q, k, v, *([w_in] * len(gate_specs)))


def _sample_attention(q_ref, k_ref, v_ref, o_ref):
    bt = k_ref.shape[0]
    nl = q_ref.shape[0] // bt
    rows = lax.broadcasted_iota(jnp.int32, (N_XHEADS * nl, N_MEM * N_XHEADS), 0)
    cols = lax.broadcasted_iota(jnp.int32, (N_XHEADS * nl, N_MEM * N_XHEADS), 1)
    same_head = (cols % N_XHEADS) == (rows // nl)
    outs = []
    for b in range(bt):
        q = q_ref[b * nl:(b + 1) * nl, :]
        qs = jnp.concatenate([q[:, h * XHEAD_DIM:(h + 1) * XHEAD_DIM] for h in range(N_XHEADS)], axis=0)
        k2 = k_ref[b].reshape(N_MEM * N_XHEADS, XHEAD_DIM).astype(BF16)
        v2 = v_ref[b].reshape(N_MEM * N_XHEADS, XHEAD_DIM).astype(BF16)
        s = lax.dot_general(qs.astype(BF16), k2, (((1,), (1,)), ((), ())), preferred_element_type=F32)
        p = _softmax(jnp.where(same_head, s * (XHEAD_DIM ** -0.5), -jnp.inf)).astype(BF16)
        o = _dot(p, v2)
        outs.append(jnp.concatenate([o[h * nl:(h + 1) * nl, :] for h in range(N_XHEADS)], axis=1))
    o_ref[...] = jnp.concatenate(outs, axis=0).astype(o_ref.dtype)


def _cast_ffn_weights(wu_f, wd_f, wa_o, wg_o, wd_o, step):
    wu = wu_f[...]
    zpad = jnp.zeros((wu.shape[0], D_FF_PAD - D_FF), BF16)
    wa_o[:, :D_FF] = wu[:, :D_FF].astype(BF16)
    wa_o[:, D_FF:] = zpad
    wg_o[:, :D_FF] = wu[:, D_FF:].astype(BF16)
    wg_o[:, D_FF:] = zpad
    live_blocks = D_FF // wd_f.shape[0]

    @pl.when(step < live_blocks)
    def _():
        wd_o[...] = wd_f[...].astype(BF16)

    @pl.when(step >= live_blocks)
    def _():
        wd_o[...] = jnp.zeros(wd_o.shape, BF16)


def _mix_kernel(*refs, side_jobs):
    h_ref, a_ref, b_ref, c_ref, wpa_ref, wpb_ref, wpc_ref, wga_ref, wgb_ref, wgc_ref = refs[:10]
    if side_jobs:
        wu_f, wd_f, qs_ref, ks_ref, vs_ref, o_ref, wa_o, wg_o, wd_o, os_ref = refs[10:]
        _cast_ffn_weights(wu_f, wd_f, wa_o, wg_o, wd_o, pl.program_id(0) * pl.num_programs(1) + pl.program_id(1))
        _sample_attention(qs_ref, ks_ref, vs_ref, os_ref)
    else:
        (o_ref,) = refs[10:]
    h = h_ref[...]
    b = b_ref[...].reshape(h.shape[0], D_B).astype(BF16)
    mix = jax.nn.sigmoid(_dot(h, wga_ref[...])) * _dot(a_ref[...], wpa_ref[...])
    mix = mix + jax.nn.sigmoid(_dot(h, wgb_ref[...])) * _dot(b, wpb_ref[...])
    mix = mix + jax.nn.sigmoid(_dot(h, wgc_ref[...])) * _dot(c_ref[...].astype(BF16), wpc_ref[...])
    o_ref[...] = mix.astype(BF16)


def _mix_call(h, a, b, c, w_pa, w_pb, w_pc, w_gate_a, w_gate_bc, name, ffn_weights=None, sample_attn=None):
    m = h.shape[0]
    tm = min(TM, m)
    tn = MIX_TN
    ni = m // tm
    nj = D_MODEL // tn
    per_gate = D_MODEL // tn
    row = lambda j, i: (i, 0)
    col = lambda j, i: (0, j)
    if b.ndim == 3:
        assert m == tm
        b_spec = pl.BlockSpec(b.shape, lambda j, i: (0, 0, 0))
    else:
        b_spec = pl.BlockSpec((tm, D_B), row)
    in_specs = [pl.BlockSpec((tm, D_MODEL), row),
                pl.BlockSpec((tm, D_A), row),
                b_spec,
                pl.BlockSpec((tm, D_C), row),
                pl.BlockSpec((D_A, tn), col),
                pl.BlockSpec((D_B, tn), col),
                pl.BlockSpec((D_C, tn), col),
                pl.BlockSpec((D_MODEL, tn), lambda j, i: (0, j)),
                pl.BlockSpec((D_MODEL, tn), lambda j, i: (0, j)),
                pl.BlockSpec((D_MODEL, tn), lambda j, i: (0, per_gate + j))]
    out_specs = [pl.BlockSpec((tm, tn), lambda j, i: (i, j))]
    out_shape = [jax.ShapeDtypeStruct((m, D_MODEL), BF16)]
    args = [h, a, b, c, w_pa, w_pb, w_pc, w_gate_a, w_gate_bc, w_gate_bc]
    if ffn_weights is not None:
        w_up, w_down = ffn_weights
        steps = ni * nj
        up_rows = D_MODEL // steps
        down_rows = CHUNK
        down_blocks = D_FF_PAD // down_rows
        assert D_MODEL % steps == 0 and D_FF % down_rows == 0 and down_blocks <= steps
        step = lambda j, i: j * ni + i
        in_specs += [pl.BlockSpec((up_rows, 2 * D_FF), lambda j, i: (step(j, i), 0)),
                     pl.BlockSpec((down_rows, D_MODEL),
                                  lambda j, i: (jnp.minimum(step(j, i), D_FF // down_rows - 1), 0))]
        out_specs += [pl.BlockSpec((up_rows, D_FF_PAD), lambda j, i: (step(j, i), 0)),
                      pl.BlockSpec((up_rows, D_FF_PAD), lambda j, i: (step(j, i), 0)),
                      pl.BlockSpec((down_rows, D_MODEL),
                                   lambda j, i: (jnp.minimum(step(j, i), down_blocks - 1), 0))]
        out_shape += [jax.ShapeDtypeStruct((D_MODEL, D_FF_PAD), BF16),
                      jax.ShapeDtypeStruct((D_MODEL, D_FF_PAD), BF16),
                      jax.ShapeDtypeStruct((D_FF_PAD, D_MODEL), BF16)]
        args += [w_up, w_down]
        q_s, cache_k, cache_v, l, nl = sample_attn
        bt = cache_k.shape[1] // steps
        assert cache_k.shape[1] % steps == 0 and (bt * nl) % SUBLANE == 0
        cache_spec = pl.BlockSpec((None, bt, N_MEM, N_XHEADS, XHEAD_DIM), lambda j, i: (l, step(j, i), 0, 0, 0))
        in_specs += [pl.BlockSpec((bt * nl, D_C), lambda j, i: (step(j, i), 0)), cache_spec, cache_spec]
        out_specs += [pl.BlockSpec((bt * nl, D_C), lambda j, i: (step(j, i), 0))]
        out_shape += [jax.ShapeDtypeStruct(q_s.shape, F32)]
        args += [q_s, cache_k, cache_v]
    return pl.pallas_call(
        functools.partial(_mix_kernel, side_jobs=ffn_weights is not None),
        grid=(nj, ni),
        in_specs=in_specs,
        out_specs=out_specs,
        out_shape=out_shape,
        compiler_params=_params("arbitrary", "arbitrary"),
        name=name,
    )(*args)


def _oproj_kernel(x_ref, mix_ref, wo_ref, g_ref, x1_ref, h2_ref):
    x1 = x_ref[...] + _dot(mix_ref[...], wo_ref[...])
    x1_ref[...] = x1
    h2_ref[...] = _rms(x1, g_ref[...]).astype(BF16)


def _oproj_call(x, mix, w_o, g_ffn):
    m = mix.shape[0]
    tm = TM
    row = lambda i: (i, 0)
    return pl.pallas_call(
        _oproj_kernel,
        grid=(m // tm,),
        in_specs=[pl.BlockSpec((tm, D_MODEL), row),
                  pl.BlockSpec((tm, D_MODEL), row),
                  pl.BlockSpec((D_MODEL, D_MODEL), lambda i: (0, 0)),
                  pl.BlockSpec((1, D_MODEL), lambda i: (0, 0))],
        out_specs=[pl.BlockSpec((tm, D_MODEL), row),
                   pl.BlockSpec((tm, D_MODEL), row)],
        out_shape=[jax.ShapeDtypeStruct((m, D_MODEL), F32),
                   jax.ShapeDtypeStruct((m, D_MODEL), BF16)],
        compiler_params=_params("parallel"),
        name="oproj_prompt",
    )(x, mix, w_o, g_ffn.reshape(1, D_MODEL))


def _oproj_sample_kernel(x_ref, mix_ref, wo_ref, g_ref, x1_ref, h2_ref, x3_ref):
    nb, nl, _ = x_ref.shape
    x3_ref[...] = x_ref[...] + _dot(mix_ref[...], wo_ref[...]).reshape(nb, nl, D_MODEL)
    x1 = jnp.concatenate([x3_ref[:, t, :] for t in range(nl)], axis=0)
    x1_ref[...] = x1
    h2_ref[...] = _rms(x1, g_ref[...]).astype(BF16)


def _oproj_sample_call(x, mix, w_o, g_ffn):
    nb, nl, _ = x.shape
    m = nb * nl
    const = lambda i: (0, 0)
    single = pl.Buffered(1)
    return pl.pallas_call(
        _oproj_sample_kernel,
        grid=(1,),
        in_specs=[pl.BlockSpec((nb, nl, D_MODEL), lambda i: (0, 0, 0), pipeline_mode=single),
                  pl.BlockSpec((m, D_MODEL), const, pipeline_mode=single),
                  pl.BlockSpec((D_MODEL, D_MODEL), const, pipeline_mode=single),
                  pl.BlockSpec((1, D_MODEL), const)],
        out_specs=[pl.BlockSpec((m, D_MODEL), const),
                   pl.BlockSpec((m, D_MODEL), const)],
        out_shape=[jax.ShapeDtypeStruct((m, D_MODEL), F32),
                   jax.ShapeDtypeStruct((m, D_MODEL), BF16)],
        scratch_shapes=[pltpu.VMEM((nb, nl, D_MODEL), F32)],
        compiler_params=_params("arbitrary"),
        name="oproj_sample",
    )(x, mix, w_o, g_ffn.reshape(1, D_MODEL))


FFN_TM = 1024


def _ffn_hidden(h, wa_ref, wg_ref, cw_ref, cb_ref, fx_ref, base, lag):
    tm = h.shape[0]
    fa = _dot(h, wa_ref[...])
    fx_ref[base:base + tm, :] = fa
    prev1 = fx_ref[pl.ds(base - lag, tm), :]
    prev2 = fx_ref[pl.ds(base - 2 * lag, tm), :]
    fc = cw_ref[2:3, :] * fa + cw_ref[1:2, :] * prev1 + cw_ref[0:1, :] * prev2 + cb_ref[...]
    act = jax.nn.gelu(fc)
    return (act * _dot(h, wg_ref[...])).astype(BF16)


def _ffn_prompt_kernel(h_ref, x1_hbm, wa_ref, wg_ref, wd_ref, cw_ref, cb_ref, g_ref, y_ref, tail_ref,
                       fx_ref, carry_ref, x1_ref, x1_sem, *, nt):
    tm = h_ref.shape[0]
    tf = wa_ref.shape[1]
    i = pl.program_id(0)
    j = pl.program_id(1)
    x1_copy = pltpu.make_async_copy(x1_hbm.at[pl.ds(i * tm, tm), :], x1_ref, x1_sem)

    @pl.when(j == 0)
    def _():
        x1_copy.start()
        y_ref[...] = jnp.zeros(y_ref.shape, F32)

    @pl.when(i % nt == 0)
    def _():
        fx_ref[0:SUBLANE, :] = jnp.zeros((SUBLANE, tf), F32)

    @pl.when(i % nt != 0)
    def _():
        fx_ref[0:SUBLANE, :] = carry_ref[j]

    p = _ffn_hidden(h_ref[...], wa_ref, wg_ref, cw_ref, cb_ref, fx_ref, SUBLANE, 1)
    last = fx_ref[tm:tm + SUBLANE, :]
    carry_ref[j] = last
    tail_ref[0] = last
    y_ref[...] += _dot(p, wd_ref[...])

    @pl.when(j == pl.num_programs(1) - 1)
    def _():
        x1_copy.wait()

        def norm_rows(r, carry):
            rs = pl.ds(pl.multiple_of(r * CHUNK, CHUNK), CHUNK)
            y_ref[rs, :] = _rms(x1_ref[rs, :] + y_ref[rs, :], g_ref[...])
            return carry

        lax.fori_loop(0, tm // CHUNK, norm_rows, 0)


def _ffn_prompt_call(h2, x1, wa, wg, wd, cw, cb, g_final, nb):
    m = h2.shape[0]
    tm = FFN_TM
    nt = m // nb // tm
    nj = D_FF_PAD // TF
    row = lambda i, j: (i, 0)
    col = lambda i, j: (0, j)
    return pl.pallas_call(
        functools.partial(_ffn_prompt_kernel, nt=nt),
        grid=(m // tm, nj),
        in_specs=[pl.BlockSpec((tm, D_MODEL), row),
                  pl.BlockSpec(memory_space=pl.ANY),
                  pl.BlockSpec((D_MODEL, TF), col),
                  pl.BlockSpec((D_MODEL, TF), col),
                  pl.BlockSpec((TF, D_MODEL), lambda i, j: (j, 0)),
                  pl.BlockSpec((CONV_F, TF), col),
                  pl.BlockSpec((1, TF), col),
                  pl.BlockSpec((1, D_MODEL), lambda i, j: (0, 0))],
        out_specs=[pl.BlockSpec((tm, D_MODEL), row),
                   pl.BlockSpec((1, SUBLANE, TF), lambda i, j: (i, 0, j))],
        out_shape=[jax.ShapeDtypeStruct((m, D_MODEL), F32),
                   jax.ShapeDtypeStruct((m // tm, SUBLANE, D_FF_PAD), F32)],
        scratch_shapes=[pltpu.VMEM((tm + SUBLANE, TF), F32),
                        pltpu.VMEM((nj, SUBLANE, TF), F32),
                        pltpu.VMEM((tm, D_MODEL), F32),
                        pltpu.SemaphoreType.DMA(())],
        compiler_params=_params("arbitrary", "arbitrary"),
        name="ffn_prompt",
    )(h2, x1, wa, wg, wd, cw, cb, g_final.reshape(1, D_MODEL))


def _ffn_sample_kernel(h_ref, x1_ref, wa_ref, wg_ref, wd_ref, cw_ref, cb_ref, g_ref, hist_ref,
                       y_ref, st_ref, fx_ref, acc_ref):
    nb = hist_ref.shape[0]
    tm = h_ref.shape[0]
    tf = wa_ref.shape[1]
    j = pl.program_id(1)
    live = (j * tf + lax.broadcasted_iota(jnp.int32, (nb, tf), 1)) < D_FF
    for r in range(CONV_F - 1):
        fx_ref[r * nb:(r + 1) * nb, :] = jnp.where(live, hist_ref[:, r, :], 0.0)
    base = (CONV_F - 1) * nb
    p = _ffn_hidden(h_ref[...], wa_ref, wg_ref, cw_ref, cb_ref, fx_ref, base, nb)
    for r in range(CONV_F - 1):
        st_ref[:, r, :] = fx_ref[tm + r * nb:tm + (r + 1) * nb, :]

    @pl.when(j == 0)
    def _():
        acc_ref[...] = x1_ref[...]

    acc_ref[...] += _dot(p, wd_ref[...])

    @pl.when(j == pl.num_programs(1) - 1)
    def _():
        for t in range(tm // nb):
            y_ref[:, t, :] = _rms(acc_ref[t * nb:(t + 1) * nb, :], g_ref[...])


def _ffn_sample_call(h2, x1, wa, wg, wd, cw, cb, g_final, state, l, nb):
    m = h2.shape[0]
    nj = D_FF_PAD // TF
    row = lambda i, j: (i, 0)
    col = lambda i, j: (0, j)
    return pl.pallas_call(
        _ffn_sample_kernel,
        grid=(1, nj),
        in_specs=[pl.BlockSpec((m, D_MODEL), row, pipeline_mode=pl.Buffered(1)),
                  pl.BlockSpec((m, D_MODEL), row, pipeline_mode=pl.Buffered(1)),
                  pl.BlockSpec((D_MODEL, TF), col),
                  pl.BlockSpec((D_MODEL, TF), col),
                  pl.BlockSpec((TF, D_MODEL), lambda i, j: (j, 0)),
                  pl.BlockSpec((CONV_F, TF), col),
                  pl.BlockSpec((1, TF), col),
                  pl.BlockSpec((1, D_MODEL), lambda i, j: (0, 0)),
                  pl.BlockSpec((None, nb, CONV_F - 1, TF), lambda i, j: (l, 0, 0, j))],
        out_specs=[pl.BlockSpec((nb, m // nb, D_MODEL), lambda i, j: (0, 0, 0)),
                   pl.BlockSpec((None, nb, CONV_F - 1, TF), lambda i, j: (0, 0, 0, j))],
        out_shape=[jax.ShapeDtypeStruct((nb, m // nb, D_MODEL), F32),
                   jax.ShapeDtypeStruct((1, nb, CONV_F - 1, D_FF), F32)],
        scratch_shapes=[pltpu.VMEM(((CONV_F - 1) * nb + m, TF), F32),
                        pltpu.VMEM((m, D_MODEL), F32)],
        compiler_params=_params("arbitrary", "arbitrary"),
        name="ffn_sample",
    )(h2, x1, wa, wg, wd, cw, cb, g_final.reshape(1, D_MODEL), state)


def kernel(x_prompt, x_sample, mem_prompt, cache_mem_k, cache_mem_v, state_conv, state_ffn_conv, g_mix, w_in, ln_v_g, ln_v_b, w_s, b_s, w_pa, conv_w, conv_b, ln_b_g, ln_b_b, w_pb, g_mem, w_k, w_v, w_pc, w_o, g_ffn, w_up, ffn_conv_w, ffn_conv_b, w_down, g_final):
    depth = g_mix.shape[0]
    assert depth == 1
    l = 0
    nbp, lp, _ = x_prompt.shape
    nbs, ls, _ = x_sample.shape
    mp = nbp * lp
    pad_ff = D_FF_PAD - D_FF

    w_in_b = w_in[l][:, :OFF_GATES].astype(BF16)
    cw = jnp.pad(ffn_conv_w[l], ((0, 0), (0, pad_ff)))
    cb = jnp.pad(ffn_conv_b[l], (0, pad_ff)).reshape(1, D_FF_PAD)

    reps = CHUNK // ls
    ws_p = w_s[l]
    bs_p = jnp.repeat(b_s[l].T, D_A // G_A, axis=1)
    bs_s = jnp.repeat(jnp.tile(b_s[l][:, :ls], (1, reps)).T, D_A // G_A, axis=1)

    h_s, a_s, q_s, glu_s, vn_s = _inproj_sample_call(x_sample, g_mix[l], w_in_b, ln_v_g[l], ln_v_b[l], ws_p, bs_s)

    xp = x_prompt.reshape(mp, D_MODEL)
    h, a, bact, q, ctail, w_pa_b, w_pb_b, w_pc_b, w_o_b = _inproj_prompt_call(
        xp, nbp, g_mix[l], w_in_b, ln_v_g[l], ln_v_b[l], ws_p, bs_p, conv_w[l], conv_b[l], ln_b_g[l], ln_b_b[l],
        (w_pa[l], w_pb[l], w_pc[l], w_o[l]))
    k5, v5, kb, vb, w_gate_a = _memkv_call(mem_prompt, g_mem[l], w_k[l], w_v[l], w_in[l])
    cact, w_gate_bc = _attn_prompt_call(q, kb, vb, w_in[l])
    mix, w_up_a, w_up_g, w_down_b, cact_s = _mix_call(
        h, a, bact, cact, w_pa_b, w_pb_b, w_pc_b, w_gate_a, w_gate_bc, "mix_prompt",
        (w_up[l], w_down[l]), (q_s, cache_mem_k, cache_mem_v, l, ls))
    x1, h2 = _oproj_call(xp, mix, w_o_b, g_ffn[l])
    yp, tail = _ffn_prompt_call(h2, x1, w_up_a, w_up_g, w_down_b, cw, cb, g_final, nbp)
    nt = tail.shape[0] // nbp
    ffn_p = tail.reshape(nbp, nt, SUBLANE, D_FF_PAD)[:, nt - 1, SUBLANE - (CONV_F - 1):, :D_FF]
    nt = ctail.shape[0] // nbp
    conv_p = ctail.reshape(nbp, nt, HALO, D_B)[:, nt - 1, HALO - (CONV_B - 1):]

    bact, conv_s = _convb_sample_call(jnp.swapaxes(state_conv, 1, 2), l, glu_s, conv_w[l], conv_b[l],
                                      ln_b_g[l], ln_b_b[l])
    conv_s = jnp.swapaxes(conv_s, 1, 2)
    (mix,) = _mix_call(h_s, a_s, bact, cact_s, w_pa_b, w_pb_b, w_pc_b, w_gate_a, w_gate_bc, "mix_sample")
    x1, h2 = _oproj_sample_call(x_sample, mix, w_o_b, g_ffn[l])
    ys, ffn_s = _ffn_sample_call(h2, x1, w_up_a, w_up_g, w_down_b, cw, cb, g_final, state_ffn_conv, l, nbs)

    return (yp.reshape(nbp, lp, D_MODEL), ys, k5, v5, conv_p[None], ffn_p[None],
            conv_s, ffn_s, vn_s)
```
